```python
import math
import jax, jax.numpy as jnp
from jax import lax
import numpy as np

D_MODEL = 1024
BATCH = 2
SEQ = 8192
DEPTH = 1
DEC_BATCH = 16
DEC_SEQ = 2048
PAST_LEN = 128

CHUNK = 128
GMLP_WIDTH = D_MODEL
GMLP_GROUPS = 8
GMLP_GROUP_DIM = GMLP_WIDTH // GMLP_GROUPS
RET_HEADS = 8
RET_HEAD_DIM = D_MODEL // RET_HEADS
RET_WIDTH = RET_HEADS * RET_HEAD_DIM
ROPE_BASE = 10000.0
N_IN_BLOCKS = 8
N_EXPERTS = 32
TOP_K = 4
D_FF = D_MODEL
SWIGLU_LIMIT = 7.0
SWIGLU_ALPHA = 1.702
MOE_BLOCK = 128
LN_EPS = 1e-5
GN_EPS = 1e-6
DN_ALPHA = (2 * DEPTH) ** 0.25
DN_BETA = (8 * DEPTH) ** -0.25

kernel_name = "hybrid_gmlp_retention_moe_encoder"


def layer_norm(x, w, b):
    xf = x.astype(jnp.float32)
    mu = xf.mean(-1, keepdims=True)
    var = jnp.square(xf - mu).mean(-1, keepdims=True)
    return ((xf - mu) * lax.rsqrt(var + LN_EPS)).astype(x.dtype) * w + b


def rotary(x):
    s, d = x.shape[2], x.shape[3]
    half = d // 2
    inv_freq = jnp.power(ROPE_BASE, -2.0 * jnp.arange(half, dtype=jnp.float32) / d)
    ang = jnp.arange(s, dtype=jnp.float32)[:, None] * inv_freq[None, :]
    cos = jnp.cos(ang).astype(x.dtype)
    sin = jnp.sin(ang).astype(x.dtype)
    x1, x2 = x[..., :half], x[..., half:]
    return jnp.concatenate([x1 * cos - x2 * sin, x2 * cos + x1 * sin], axis=-1)


def gmlp_branch(u, v, ln_w, ln_b, w_s, b_s):
    bsz, s, _ = v.shape
    u = jax.nn.gelu(u, approximate=False)
    v = layer_norm(jax.nn.gelu(v, approximate=False), ln_w, ln_b)
    vc = v.reshape(bsz, s // CHUNK, CHUNK, GMLP_GROUPS, GMLP_GROUP_DIM)
    mixed = jnp.einsum('gpq,bnqgd->bnpgd', w_s, vc) + b_s.T[None, None, :, :, None]
    return u * mixed.reshape(bsz, s, GMLP_WIDTH)


def retention_chunkwise(q, k, v, log_g, strict):
    bsz, nh, s, dk = q.shape
    dv = v.shape[-1]
    nc = s // CHUNK
    qc = q.reshape(bsz, nh, nc, CHUNK, dk)
    kc = k.reshape(bsz, nh, nc, CHUNK, dk)
    vc = v.reshape(bsz, nh, nc, CHUNK, dv)
    idx = jnp.arange(CHUNK, dtype=jnp.float32)
    diff = idx[:, None] - idx[None, :]
    keep = (diff > 0) if strict else (diff >= 0)
    decay = jnp.where(keep[None], jnp.exp(jnp.maximum(diff, 0.0)[None] * log_g[:, None, None]), 0.0)
    decay = decay.astype(q.dtype)
    scores = jnp.einsum('bhncd,bhnmd->bhncm', qc, kc) * decay[None, :, None]
    intra = jnp.einsum('bhncm,bhnme->bhnce', scores, vc)
    zeta = jnp.exp((CHUNK - 1.0 - idx)[None, :] * log_g[:, None]).astype(q.dtype)
    kv = jnp.einsum('bhncd,hc,bhnce->bhnde', kc, zeta, vc)
    chunk_decay = jnp.exp(CHUNK * log_g).astype(q.dtype)[None, :, None, None]

    def step(state, kv_n):
        return state * chunk_decay + kv_n, state

    _, prev = lax.scan(step, jnp.zeros_like(kv[:, :, 0]), jnp.moveaxis(kv, 2, 0))
    prev = jnp.moveaxis(prev, 0, 2)
    xi = jnp.exp((idx + 1.0)[None, :] * log_g[:, None]).astype(q.dtype)
    cross = jnp.einsum('bhncd,bhnde->bhnce', qc, prev) * xi[None, :, None, :, None]
    return (intra + cross).reshape(bsz, nh, s, dv)


def retention_branch(q, k, v, g, theta_f, theta_b, gn_w):
    bsz, s, _ = q.shape

    def heads(t):
        return t.reshape(bsz, s, RET_HEADS, RET_HEAD_DIM).transpose(0, 2, 1, 3)

    qh = rotary(heads(q))
    kh = rotary(heads(k)) * (RET_HEAD_DIM ** -0.5)
    vh = heads(v)
    lg_f = jax.nn.log_sigmoid(theta_f.astype(jnp.float32))
    lg_b = jax.nn.log_sigmoid(theta_b.astype(jnp.float32))
    y_f = retention_chunkwise(qh, kh, vh, lg_f, False)
    flip = lambda t: t[:, :, ::-1]
    y_b = flip(retention_chunkwise(flip(qh), flip(kh), flip(vh), lg_b, True))
    y = (y_f + y_b).astype(jnp.float32)
    mu = y.mean(-1, keepdims=True)
    var = jnp.square(y - mu).mean(-1, keepdims=True)
    y = ((y - mu) * lax.rsqrt(var + GN_EPS)).astype(q.dtype)
    y = y.transpose(0, 2, 1, 3).reshape(bsz, s, RET_WIDTH) * gn_w
    return jax.nn.silu(g) * y


def token_mixer(h, w_in, gmlp_ln_w, gmlp_ln_b, w_spatial, b_spatial, theta_f, theta_b, gn_w, w_out):
    proj = h @ w_in
    u, v_g, q, k, v_r, g_r, gate_a, gate_b = jnp.split(proj, N_IN_BLOCKS, axis=-1)
    a = gmlp_branch(u, v_g, gmlp_ln_w, gmlp_ln_b, w_spatial, b_spatial)
    r = retention_branch(q, k, v_r, g_r, theta_f, theta_b, gn_w)
    merged = jax.nn.sigmoid(gate_a) * a + jax.nn.sigmoid(gate_b) * r
    return merged @ w_out


def moe(h, w_router, b_router, w_gate_up, b_gate_up, w_down, b_down):
    t, d = h.shape
    logits = (h @ w_router + b_router).astype(jnp.float32)
    top_val, top_idx = lax.top_k(logits, TOP_K)
    top_w = jax.nn.softmax(top_val, axis=-1)
    n_assign = t * TOP_K
    n_blocks = -(-n_assign // MOE_BLOCK) + N_EXPERTS
    flat_e = top_idx.reshape(-1).astype(jnp.int32)
    flat_tok = jnp.arange(n_assign, dtype=jnp.int32) // TOP_K
    order = jnp.argsort(flat_e)
    sorted_e = flat_e[order]
    sizes = jnp.bincount(flat_e, length=N_EXPERTS)
    starts = jnp.cumsum(sizes) - sizes
    padded = (sizes + MOE_BLOCK - 1) // MOE_BLOCK * MOE_BLOCK
    pends = jnp.cumsum(padded)
    pstarts = pends - padded
    dest_sorted = pstarts[sorted_e] + jnp.arange(n_assign) - starts[sorted_e]
    buf_tok = jnp.full((n_blocks * MOE_BLOCK,), t, jnp.int32).at[dest_sorted].set(flat_tok[order])
    dest = jnp.zeros((n_assign,), jnp.int32).at[order].set(dest_sorted.astype(jnp.int32))
    block_e = jnp.minimum(jnp.searchsorted(pends, jnp.arange(n_blocks) * MOE_BLOCK, side='right'),
                          N_EXPERTS - 1)
    h_pad = jnp.concatenate([h, jnp.zeros((1, d), h.dtype)], axis=0)
    x_buf = h_pad[buf_tok].reshape(n_blocks, MOE_BLOCK, d)

    def expert_block(args):
        xb, e = args
        gu = xb @ w_gate_up[e] + b_gate_up[e]
        gate, up = gu[:, :D_FF], gu[:, D_FF:]
        gate = jnp.minimum(gate, SWIGLU_LIMIT)
        up = jnp.clip(up, -SWIGLU_LIMIT, SWIGLU_LIMIT)
        act = (up + 1.0) * gate * jax.nn.sigmoid(SWIGLU_ALPHA * gate)
        return act @ w_down[e] + b_down[e]

    y_buf = lax.map(expert_block, (x_buf, block_e)).reshape(-1, d)
    y_tk = y_buf[dest].reshape(t, TOP_K, d)
    return jnp.einsum('tkd,tk->td', y_tk, top_w.astype(h.dtype))


def trunk(x, c, weights):
    (w_ada, b_ada, w_in, gmlp_ln_w, gmlp_ln_b, w_spatial, b_spatial, ret_theta_fwd, ret_theta_bwd,
     ret_gn_w, w_out, ln1_w, ln1_b, w_router, b_router, w_gate_up, b_gate_up, w_down, b_down,
     ln2_w, ln2_b) = weights
    bsz, s, d = x.shape
    for l in range(DEPTH):
        mod = (jax.nn.silu(c) @ w_ada[l] + b_ada[l])[:, None, :]
        sh1, sc1, g1, sh2, sc2, g2 = jnp.split(mod, 6, axis=-1)
        h = x * (1.0 + sc1) + sh1
        mix = token_mixer(h, w_in[l], gmlp_ln_w[l], gmlp_ln_b[l], w_spatial[l], b_spatial[l],
                          ret_theta_fwd[l], ret_theta_bwd[l], ret_gn_w[l], w_out[l])
        x = layer_norm(DN_ALPHA * x + g1 * mix, ln1_w[l], ln1_b[l])
        h = x * (1.0 + sc2) + sh2
        ff = moe(h.reshape(bsz * s, d), w_router[l], b_router[l], w_gate_up[l], b_gate_up[l],
                 w_down[l], b_down[l]).reshape(bsz, s, d)
        x = layer_norm(DN_ALPHA * x + g2 * ff, ln2_w[l], ln2_b[l])
    return x


def setup_inputs(seed: int = 0) -> dict:
    key = jax.random.key(seed)
    ks = jax.random.split(key, 25)
    L, D, E, F, C, G, H = DEPTH, D_MODEL, N_EXPERTS, D_FF, CHUNK, GMLP_GROUPS, RET_HEADS
    f32 = jnp.float32
    nrm = lambda k, shape: jax.random.normal(k, shape, f32)
    col_scale = jnp.repeat(jnp.array([DN_BETA, 1.0, 1.0, 1.0, DN_BETA, 1.0, 1.0, 1.0], f32), D)
    r = jnp.power(2.0, -5.0 - jnp.arange(H, dtype=f32))
    theta0 = jnp.log1p(-r) - jnp.log(r)
    return {
        'x_prompt': nrm(ks[0], (BATCH, SEQ, D)),
        'x_sample': nrm(ks[1], (DEC_BATCH, DEC_SEQ, D)),
        'c_prompt': nrm(ks[2], (BATCH, D)),
        'c_sample': nrm(ks[3], (DEC_BATCH, D)),
        'w_ada': nrm(ks[4], (L, D, 6 * D)) * (0.5 * D ** -0.5),
        'b_ada': nrm(ks[5], (L, 6 * D)) * 0.02,
        'w_in': nrm(ks[6], (L, D, N_IN_BLOCKS * D)) * (D ** -0.5) * col_scale,
        'gmlp_ln_w': 1.0 + 0.02 * nrm(ks[7], (L, GMLP_WIDTH)),
        'gmlp_ln_b': 0.02 * nrm(ks[8], (L, GMLP_WIDTH)),
        'w_spatial': nrm(ks[9], (L, G, C, C)) * (C ** -0.5),
        'b_spatial': 0.02 * nrm(ks[10], (L, G, C)),
        'ret_theta_fwd': theta0[None, :] + 0.01 * nrm(ks[11], (L, H)),
        'ret_theta_bwd': theta0[None, :] + 0.01 * nrm(ks[12], (L, H)),
        'ret_gn_w': 1.0 + 0.02 * nrm(ks[13], (L, RET_WIDTH)),
        'w_out': nrm(ks[14], (L, D, D)) * (D ** -0.5) * DN_BETA,
        'ln1_w': 1.0 + 0.02 * nrm(ks[15], (L, D)),
        'ln1_b': 0.02 * nrm(ks[16], (L, D)),
        'w_router': nrm(ks[17], (L, D, E)) * (D ** -0.5),
        'b_router': 0.01 * nrm(ks[18], (L, E)),
        'w_gate_up': nrm(ks[19], (L, E, D, 2 * F)) * (D ** -0.5) * DN_BETA,
        'b_gate_up': 0.01 * nrm(ks[20], (L, E, 2 * F)),
        'w_down': nrm(ks[21], (L, E, F, D)) * (F ** -0.5) * DN_BETA,
        'b_down': 0.01 * nrm(ks[22], (L, E, D)),
        'ln2_w': 1.0 + 0.02 * nrm(ks[23], (L, D)),
        'ln2_b': 0.02 * nrm(ks[24], (L, D)),
    }


def reference(x_prompt, x_sample, c_prompt, c_sample, w_ada, b_ada, w_in, gmlp_ln_w, gmlp_ln_b,
              w_spatial, b_spatial, ret_theta_fwd, ret_theta_bwd, ret_gn_w, w_out, ln1_w, ln1_b,
              w_router, b_router, w_gate_up, b_gate_up, w_down, b_down, ln2_w, ln2_b):
    weights = (w_ada, b_ada, w_in, gmlp_ln_w, gmlp_ln_b, w_spatial, b_spatial, ret_theta_fwd,
               ret_theta_bwd, ret_gn_w, w_out, ln1_w, ln1_b, w_router, b_router, w_gate_up,
               b_gate_up, w_down, b_down, ln2_w, ln2_b)
    y_prompt = trunk(x_prompt, c_prompt, weights)
    y_sample = trunk(x_sample, c_sample, weights)
    return (y_prompt, y_sample)
```

```python
import functools

import jax
import jax.numpy as jnp
from jax import lax
from jax.experimental import pallas as pl
from jax.experimental.pallas import tpu as pltpu

f32 = jnp.float32
bf16 = jnp.bfloat16
i32 = jnp.int32

D = 1024
CHUNK = 128
HEADS = 8
HD = D // HEADS
N_SEG = 8
E = 32
TOPK = 4
FF = 1024
ROPE_BASE = 10000.0
SWIGLU_LIMIT = 7.0
SWIGLU_ALPHA = 1.702
LN_EPS = 1e-5
GN_EPS = 1e-6
DEPTH = 1
DN_ALPHA = (2 * DEPTH) ** 0.25

LANES = 128
SUBLANES = 8
ROW_TILES = D // LANES

MIX_TOKENS = 256
ROUTE_TOKENS = 512
DISPATCH_TOKENS = 512
COMBINE_TOKENS = 256
EXPERT_ROWS = 256
VMEM_LIMIT = 56 * 1024 * 1024

_HI = lax.Precision.HIGHEST


def _const_spec(shape):
    nd = len(shape)
    return pl.BlockSpec(shape, lambda *_: (0,) * nd, pipeline_mode=pl.Buffered(1))


def _gelu(x):
    return 0.5 * x * (1.0 + lax.erf(x * (2.0 ** -0.5)))


def _normalize(x, eps):
    mu = jnp.mean(x, axis=-1, keepdims=True)
    xc = x - mu
    var = jnp.mean(xc * xc, axis=-1, keepdims=True)
    return xc * lax.rsqrt(var + eps)


def _silu(x):
    return x * jax.nn.sigmoid(x)


def _head(h):
    return slice(h * HD, (h + 1) * HD)


def _adaln_body(c_ref, w_ref, b_ref, o_ref):
    s = _silu(c_ref[...])
    o_ref[...] = jnp.dot(s, w_ref[...], preferred_element_type=f32, precision=_HI) + b_ref[...]


def _adaln(c, w_ada, b_ada):
    rows = c.shape[0]
    return pl.pallas_call(
        _adaln_body,
        grid=(6,),
        in_specs=[pl.BlockSpec((rows, D), lambda j: (0, 0)),
                  pl.BlockSpec((D, D), lambda j: (0, j)),
                  pl.BlockSpec((1, D), lambda j: (0, j))],
        out_specs=pl.BlockSpec((rows, D), lambda j: (0, j)),
        out_shape=jax.ShapeDtypeStruct((rows, 6 * D), f32),
        compiler_params=pltpu.CompilerParams(vmem_limit_bytes=VMEM_LIMIT),
        name="adaln",
    )(c, w_ada, b_ada.reshape(1, 6 * D))


def _mixer_a_body(nck, x_ref, mod_ref, win_ref, glnw_ref, glnb_ref, wsp_ref, bsp_ref,
                  cq_ref, sq_ref, ck_ref, sk_ref, mtab_ref, zf_ref, zb_ref, xib_ref, cdb_ref, gnw_ref,
                  p1_ref, cr_ref, yp_ref, qr_ref, kvf_ref, sb_ref):
    @pl.when(pl.program_id(1) == 0)
    def _():
        sb_ref[...] = jnp.zeros_like(sb_ref)

    h = (x_ref[...] * (1.0 + mod_ref[1:2, :]) + mod_ref[0:1, :]).astype(bf16)

    def seg(j):
        return jnp.dot(h, win_ref[:, j * D:(j + 1) * D], preferred_element_type=f32)

    vn = _normalize(_gelu(seg(1)), LN_EPS) * glnw_ref[...] + glnb_ref[...]
    vnb = vn.astype(bf16)
    ug = _gelu(seg(0))
    ga = jax.nn.sigmoid(seg(6))
    for c in range(nck):
        rows = slice(c * CHUNK, (c + 1) * CHUNK)
        mixed = jnp.concatenate(
            [jnp.dot(wsp_ref[g], vnb[rows, _head(g)], preferred_element_type=f32) for g in range(HEADS)], axis=1)
        p1_ref[rows, :] = ga[rows, :] * (ug[rows, :] * (mixed + bsp_ref[...]))

    q = seg(2)
    k = seg(3)
    v = seg(4).astype(bf16)
    cq, sq, ck, sk = cq_ref[...], sq_ref[...], ck_ref[...], sk_ref[...]
    qr = jnp.concatenate([q[:, _head(h_)] * cq + pltpu.roll(q[:, _head(h_)], HD // 2, 1) * sq
                          for h_ in range(HEADS)], axis=1).astype(bf16)
    kr = jnp.concatenate([k[:, _head(h_)] * ck + pltpu.roll(k[:, _head(h_)], HD // 2, 1) * sk
                          for h_ in range(HEADS)], axis=1)
    qr_ref[...] = qr
    for c in reversed(range(nck)):
        rows = slice(c * CHUNK, (c + 1) * CHUNK)
        kc = kr[rows, :]
        kcb = kc.astype(bf16)
        kzf = (kc * zf_ref[...]).astype(bf16)
        kzb = (kc * zb_ref[...]).astype(bf16)
        sb = sb_ref[...]
        sbb = sb.astype(bf16)
        yps, kvfs, kvbs = [], [], []
        for h_ in range(HEADS):
            hs = _head(h_)
            qh = qr[rows, hs]
            vh = v[rows, hs]
            sc = lax.dot_general(qh, kcb[:, hs], (((1,), (1,)), ((), ())), preferred_element_type=f32)
            sc = (sc * mtab_ref[:, hs]).astype(bf16)
            intra = jnp.dot(sc, vh, preferred_element_type=f32)
            cross = jnp.dot(qh, sbb[:, hs], preferred_element_type=f32) * xib_ref[:, hs]
            yps.append(intra + cross)
            kvfs.append(lax.dot_general(kzf[:, hs], vh, (((0,), (0,)), ((), ())), preferred_element_type=f32))
            kvbs.append(lax.dot_general(kzb[:, hs], vh, (((0,), (0,)), ((), ())), preferred_element_type=f32))
        yp_ref[rows, :] = jnp.concatenate(yps, axis=1)
        kvf_ref[c] = jnp.concatenate(kvfs, axis=1)
        sb_ref[...] = sb * cdb_ref[...] + jnp.concatenate(kvbs, axis=1)

    cr_ref[...] = jax.nn.sigmoid(seg(7)) * _silu(seg(5)) * gnw_ref[...]


def _mixer_a(x, mod, win, glnw, glnb, wsp, bsp, rope, tabs, gnw):
    b, s, _ = x.shape
    t = MIX_TOKENS
    nt, nck = s // t, t // CHUNK
    cq, sq, ck, sk = rope
    mtab, zf, zb, xib, cdb = tabs
    rev = lambda bi, i: (bi, nt - 1 - i, 0)
    rope_spec = pl.BlockSpec((t, HD), lambda bi, i: (nt - 1 - i, 0))
    tok = lambda dt: jax.ShapeDtypeStruct((b, s, D), dt)
    return pl.pallas_call(
        functools.partial(_mixer_a_body, nck),
        grid=(b, nt),
        in_specs=[pl.BlockSpec((None, t, D), rev),
                  pl.BlockSpec((None, 6, D), lambda bi, i: (bi, 0, 0)),
                  _const_spec((D, N_SEG * D)),
                  _const_spec((1, D)), _const_spec((1, D)),
                  _const_spec((HEADS, CHUNK, CHUNK)), _const_spec((CHUNK, D)),
                  rope_spec, rope_spec, rope_spec, rope_spec,
                  _const_spec((CHUNK, D)), _const_spec((CHUNK, D)), _const_spec((CHUNK, D)),
                  _const_spec((CHUNK, D)), _const_spec((1, D)), _const_spec((1, D))],
        out_specs=[pl.BlockSpec((None, t, D), rev)] * 4
                  + [pl.BlockSpec((None, nck, CHUNK, D), lambda bi, i: (bi, nt - 1 - i, 0, 0))],
        out_shape=[tok(f32), tok(f32), tok(f32), tok(bf16),
                   jax.ShapeDtypeStruct((b, s // CHUNK, CHUNK, D), f32)],
        scratch_shapes=[pltpu.VMEM((CHUNK, D), f32)],
        compiler_params=pltpu.CompilerParams(dimension_semantics=("arbitrary", "arbitrary"),
                                             vmem_limit_bytes=VMEM_LIMIT),
        name="mixer_a",
    )(x, mod, win, glnw, glnb, wsp, bsp, cq, sq, ck, sk, mtab, zf, zb, xib, cdb, gnw)


def _mixer_b_body(nck, yp_ref, qr_ref, kvf_ref, p1_ref, cr_ref, x_ref, mod_ref, xif_ref, cdf_ref,
                  wout_ref, ln1w_ref, ln1b_ref, wr_ref, br_ref,
                  x1_ref, h2_ref, lg_ref, sf_ref):
    @pl.when(pl.program_id(1) == 0)
    def _():
        sf_ref[...] = jnp.zeros_like(sf_ref)

    ys = []
    for c in range(nck):
        rows = slice(c * CHUNK, (c + 1) * CHUNK)
        sf = sf_ref[...]
        sfb = sf.astype(bf16)
        cross = jnp.concatenate(
            [jnp.dot(qr_ref[rows, _head(h_)], sfb[:, _head(h_)], preferred_element_type=f32)
             for h_ in range(HEADS)], axis=1)
        ys.append(yp_ref[rows, :] + cross * xif_ref[...])
        sf_ref[...] = sf * cdf_ref[...] + kvf_ref[c]
    y = jnp.concatenate(ys, axis=0)
    yn = jnp.concatenate([_normalize(y[:, _head(h_)], GN_EPS) for h_ in range(HEADS)], axis=1)
    merged = p1_ref[...] + cr_ref[...] * yn
    mix = jnp.dot(merged.astype(bf16), wout_ref[...], preferred_element_type=f32)
    x1 = _normalize(DN_ALPHA * x_ref[...] + mod_ref[2:3, :] * mix, LN_EPS) * ln1w_ref[...] + ln1b_ref[...]
    x1_ref[...] = x1
    h2 = x1 * (1.0 + mod_ref[4:5, :]) + mod_ref[3:4, :]
    for s_ in range(ROW_TILES):
        h2_ref[:, s_, :] = h2[:, s_ * LANES:(s_ + 1) * LANES]
    lg_ref[...] = jnp.dot(h2, wr_ref[...], preferred_element_type=f32, precision=_HI) + br_ref[...]


def _mixer_b(yp, qr, kvf, p1, cr, x, mod, xif, cdf, wout, ln1w, ln1b, wr, br):
    b, s, _ = x.shape
    t = MIX_TOKENS
    nt, nck = s // t, t // CHUNK
    fwd = lambda bi, i: (bi, i, 0)
    flat = lambda bi, i: (bi * nt + i, 0)
    return pl.pallas_call(
        functools.partial(_mixer_b_body, nck),
        grid=(b, nt),
        in_specs=[pl.BlockSpec((None, t, D), fwd), pl.BlockSpec((None, t, D), fwd),
                  pl.BlockSpec((None, nck, CHUNK, D), lambda bi, i: (bi, i, 0, 0)),
                  pl.BlockSpec((None, t, D), fwd), pl.BlockSpec((None, t, D), fwd),
                  pl.BlockSpec((None, t, D), fwd),
                  pl.BlockSpec((None, 6, D), lambda bi, i: (bi, 0, 0)),
                  _const_spec((CHUNK, D)), _const_spec((1, D)),
                  _const_spec((D, D)), _const_spec((1, D)), _const_spec((1, D)),
                  _const_spec((D, LANES)), _const_spec((1, LANES))],
        out_specs=[pl.BlockSpec((None, t, D), fwd),
                   pl.BlockSpec((t, ROW_TILES, LANES), lambda bi, i: (bi * nt + i, 0, 0)),
                   pl.BlockSpec((t, LANES), flat)],
        out_shape=[jax.ShapeDtypeStruct((b, s, D), f32),
                   jax.ShapeDtypeStruct((b * s, ROW_TILES, LANES), f32),
                   jax.ShapeDtypeStruct((b * s, LANES), f32)],
        scratch_shapes=[pltpu.VMEM((CHUNK, D), f32)],
        compiler_params=pltpu.CompilerParams(dimension_semantics=("arbitrary", "arbitrary"),
                                             vmem_limit_bytes=VMEM_LIMIT),
        name="mixer_b",
    )(yp, qr, kvf, p1, cr, x, mod, xif, cdf, wout, ln1w, ln1b, wr, br)


def _route_body(lg_ref, idx_ref, w_ref, rank_ref, cnt_ref, run_ref):
    tt = lg_ref.shape[0]

    @pl.when(pl.program_id(0) == 0)
    def _():
        run_ref[...] = jnp.zeros_like(run_ref)

    l = lg_ref[...].T[:E, :]
    iota_e = lax.broadcasted_iota(i32, (E, tt), 0)
    vals, idxs, hots = [], [], []
    for _ in range(TOPK):
        m = jnp.max(l, axis=0, keepdims=True)
        idx = jnp.min(jnp.where(l == m, iota_e, E), axis=0, keepdims=True)
        hot = iota_e == idx
        l = jnp.where(hot, -jnp.inf, l)
        vals.append(m)
        idxs.append(idx)
        hots.append(hot)
    exps = [jnp.exp(v_ - vals[0]) for v_ in vals]
    tot = exps[0] + exps[1] + exps[2] + exps[3]
    w_ref[...] = jnp.concatenate([e_ / tot for e_ in exps], axis=0)
    idx_ref[...] = jnp.concatenate(idxs, axis=0)

    sel = jnp.zeros((E, tt), f32)
    for hot in hots:
        sel = sel + hot.astype(f32)
    upper = (lax.broadcasted_iota(i32, (tt, tt), 0) < lax.broadcasted_iota(i32, (tt, tt), 1)).astype(bf16)
    run = run_ref[:, 0:1]
    rank_e = jnp.dot(sel.astype(bf16), upper, preferred_element_type=f32) + run
    ranks = [jnp.sum(jnp.where(hot, rank_e, 0.0), axis=0, keepdims=True) for hot in hots]
    rank_ref[...] = jnp.concatenate(ranks, axis=0).astype(i32)
    run_new = run + jnp.sum(sel, axis=1, keepdims=True)
    run_ref[...] = jnp.broadcast_to(run_new, run_ref.shape)
    cnt_ref[...] = jnp.broadcast_to(run_new, cnt_ref.shape)


def _route(logits):
    n = logits.shape[0]
    tt = ROUTE_TOKENS
    col = lambda i: (0, i)
    return pl.pallas_call(
        _route_body,
        grid=(n // tt,),
        in_specs=[pl.BlockSpec((tt, LANES), lambda i: (i, 0))],
        out_specs=[pl.BlockSpec((TOPK, tt), col), pl.BlockSpec((TOPK, tt), col), pl.BlockSpec((TOPK, tt), col),
                   pl.BlockSpec((E, LANES), lambda i: (0, 0))],
        out_shape=[jax.ShapeDtypeStruct((TOPK, n), i32), jax.ShapeDtypeStruct((TOPK, n), f32),
                   jax.ShapeDtypeStruct((TOPK, n), i32), jax.ShapeDtypeStruct((E, LANES), f32)],
        scratch_shapes=[pltpu.VMEM((E, LANES), f32)],
        compiler_params=pltpu.CompilerParams(dimension_semantics=("arbitrary",), vmem_limit_bytes=VMEM_LIMIT),
        name="route",
    )(logits)


def _row_copy(src, dst, sem):
    return pltpu.make_async_copy(src, dst, sem)


def _dispatch_body(n_pad_rows, cnt_ref, pst_ref, pad_ref, dest_ref, h2_ref, xbuf_ref, zrow_ref, sem, zsem):
    tt = dest_ref.shape[0] // TOPK
    i = pl.program_id(0)
    n_rows = xbuf_ref.shape[0]

    def issue(j, carry):
        src = h2_ref.at[i * tt + j]
        for k_ in range(TOPK):
            _row_copy(src, xbuf_ref.at[dest_ref[k_ * tt + j]], sem).start()
        return carry

    lax.fori_loop(0, tt, issue, 0)

    @pl.when(i == pl.num_programs(0) - 1)
    def _():
        zrow_ref[...] = jnp.zeros_like(zrow_ref)

        def zero_row(r, carry):
            _row_copy(zrow_ref, xbuf_ref.at[r], zsem).start()
            return carry

        def per_expert(e_, carry):
            lax.fori_loop(pst_ref[e_] + cnt_ref[e_], pst_ref[e_] + pad_ref[e_], zero_row, 0)
            return carry

        lax.fori_loop(0, E, per_expert, 0)
        lax.fori_loop(pst_ref[E - 1] + pad_ref[E - 1], n_rows, zero_row, 0)
        _row_copy(xbuf_ref.at[pl.ds(0, n_pad_rows)], xbuf_ref.at[pl.ds(0, n_pad_rows)], zsem).wait()

    _row_copy(xbuf_ref.at[pl.ds(0, TOPK * tt)], xbuf_ref.at[pl.ds(0, TOPK * tt)], sem).wait()


def _dispatch(h2, dest_tiles, counts, pstarts, padded, n_rows):
    n = h2.shape[0]
    tt = DISPATCH_TOKENS
    return pl.pallas_call(
        functools.partial(_dispatch_body, n_rows - n * TOPK),
        grid_spec=pltpu.PrefetchScalarGridSpec(
            num_scalar_prefetch=3,
            grid=(n // tt,),
            in_specs=[pl.BlockSpec((TOPK * tt,), lambda i, *_: (i,), memory_space=pltpu.SMEM),
                      pl.BlockSpec(memory_space=pl.ANY)],
            out_specs=pl.BlockSpec(memory_space=pl.ANY),
            scratch_shapes=[pltpu.VMEM((ROW_TILES, LANES), f32),
                            pltpu.SemaphoreType.DMA(()), pltpu.SemaphoreType.DMA(())]),
        out_shape=jax.ShapeDtypeStruct((n_rows, ROW_TILES, LANES), f32),
        compiler_params=pltpu.CompilerParams(dimension_semantics=("arbitrary",), vmem_limit_bytes=VMEM_LIMIT),
        name="dispatch",
    )(counts, pstarts, padded, dest_tiles, h2)


def _expert_body(be_ref, x_ref, wgu_ref, bgu_ref, wd_ref, bd_ref, y_ref):
    x = jnp.concatenate([x_ref[:, s_, :] for s_ in range(ROW_TILES)], axis=1).astype(bf16)
    gu = jnp.dot(x, wgu_ref[...], preferred_element_type=f32) + bgu_ref[...]
    gate = jnp.minimum(gu[:, :FF], SWIGLU_LIMIT)
    up = jnp.clip(gu[:, FF:], -SWIGLU_LIMIT, SWIGLU_LIMIT)
    act = (up + 1.0) * gate * jax.nn.sigmoid(SWIGLU_ALPHA * gate)
    y = jnp.dot(act.astype(bf16), wd_ref[...], preferred_element_type=f32) + bd_ref[...]
    for s_ in range(ROW_TILES):
        y_ref[:, s_, :] = y[:, s_ * LANES:(s_ + 1) * LANES]


def _experts(block_e, xbuf, wgu, bgu, wd, bd):
    n_rows = xbuf.shape[0]
    r = EXPERT_ROWS
    row_spec = pl.BlockSpec((r, ROW_TILES, LANES), lambda b_, be: (b_, 0, 0))
    return pl.pallas_call(
        _expert_body,
        grid_spec=pltpu.PrefetchScalarGridSpec(
            num_scalar_prefetch=1,
            grid=(n_rows // r,),
            in_specs=[row_spec,
                      pl.BlockSpec((None, D, 2 * FF), lambda b_, be: (be[b_], 0, 0)),
                      pl.BlockSpec((None, 1, 2 * FF), lambda b_, be: (be[b_], 0, 0)),
                      pl.BlockSpec((None, FF, D), lambda b_, be: (be[b_], 0, 0)),
                      pl.BlockSpec((None, 1, D), lambda b_, be: (be[b_], 0, 0))],
            out_specs=row_spec),
        out_shape=jax.ShapeDtypeStruct((n_rows, ROW_TILES, LANES), f32),
        compiler_params=pltpu.CompilerParams(dimension_semantics=("arbitrary",), vmem_limit_bytes=VMEM_LIMIT),
        name="experts",
    )(block_e, xbuf, wgu, bgu, wd, bd)


def _combine_body(dest_ref, ybuf_ref, w_ref, x1_ref, mod_ref, ln2w_ref, ln2b_ref, o_ref, rows_ref, sem):
    tt = x1_ref.shape[0]

    def issue(j, carry):
        _row_copy(ybuf_ref.at[dest_ref[j]], rows_ref.at[j], sem).start()
        return carry

    lax.fori_loop(0, TOPK * tt, issue, 0)
    _row_copy(ybuf_ref.at[pl.ds(0, TOPK * tt)], rows_ref, sem).wait()

    w = w_ref[...]
    ff = jnp.zeros((tt, D), f32)
    for k_ in range(TOPK):
        yk = jnp.concatenate([rows_ref[k_ * tt:(k_ + 1) * tt, s_, :] for s_ in range(ROW_TILES)], axis=1)
        ff = ff + yk * w[:, k_:k_ + 1]
    z = DN_ALPHA * x1_ref[...] + mod_ref[5:6, :] * ff
    o_ref[...] = _normalize(z, LN_EPS) * ln2w_ref[...] + ln2b_ref[...]


def _combine(dest_tiles, ybuf, w_t, x1, mod, ln2w, ln2b, seq):
    n = x1.shape[0]
    tt = COMBINE_TOKENS
    return pl.pallas_call(
        _combine_body,
        grid=(n // tt,),
        in_specs=[pl.BlockSpec((TOPK * tt,), lambda i: (i,), memory_space=pltpu.SMEM),
                  pl.BlockSpec(memory_space=pl.ANY),
                  pl.BlockSpec((tt, TOPK), lambda i: (i, 0)),
                  pl.BlockSpec((tt, D), lambda i: (i, 0)),
                  pl.BlockSpec((None, 6, D), lambda i: ((i * tt) // seq, 0, 0)),
                  _const_spec((1, D)), _const_spec((1, D))],
        out_specs=pl.BlockSpec((tt, D), lambda i: (i, 0)),
        out_shape=jax.ShapeDtypeStruct((n, D), f32),
        scratch_shapes=[pltpu.VMEM((TOPK * tt, ROW_TILES, LANES), f32), pltpu.SemaphoreType.DMA(())],
        compiler_params=pltpu.CompilerParams(dimension_semantics=("arbitrary",), vmem_limit_bytes=VMEM_LIMIT),
        name="combine",
    )(dest_tiles, ybuf, w_t, x1, mod, ln2w, ln2b)


def _tile_major(a, tt):
    n = a.shape[1]
    return a.reshape(TOPK, n // tt, tt).transpose(1, 0, 2).reshape(-1)


def _rope_tables(s):
    half = HD // 2
    inv_freq = jnp.power(ROPE_BASE, -2.0 * jnp.arange(half, dtype=f32) / HD)
    ang = jnp.arange(s, dtype=f32)[:, None] * inv_freq[None, :]
    cos, sin = jnp.cos(ang), jnp.sin(ang)
    cos2 = jnp.concatenate([cos, cos], axis=1)
    sin2 = jnp.concatenate([-sin, sin], axis=1)
    scale = HD ** -0.5
    return cos2, sin2, cos2 * scale, sin2 * scale


def _retention_tables(theta_f, theta_b):
    lg_f = jax.nn.log_sigmoid(theta_f.astype(f32))
    lg_b = jax.nn.log_sigmoid(theta_b.astype(f32))
    idx = jnp.arange(CHUNK, dtype=f32)
    diff = idx[:, None] - idx[None, :]
    dec_f = jnp.where(diff >= 0, jnp.exp(jnp.maximum(diff, 0.0)[None] * lg_f[:, None, None]), 0.0)
    dec_b = jnp.where(diff < 0, jnp.exp(jnp.maximum(-diff, 0.0)[None] * lg_b[:, None, None]), 0.0)
    mtab = (dec_f + dec_b).transpose(1, 0, 2).reshape(CHUNK, D)
    lanes = lambda t: jnp.repeat(t.T, HD, axis=1)
    zf = lanes(jnp.exp((CHUNK - 1.0 - idx)[None, :] * lg_f[:, None]))
    zb = lanes(jnp.exp(idx[None, :] * lg_b[:, None]))
    xif = lanes(jnp.exp((idx + 1.0)[None, :] * lg_f[:, None]))
    xib = lanes(jnp.exp((CHUNK - idx)[None, :] * lg_b[:, None]))
    cdf = jnp.repeat(jnp.exp(CHUNK * lg_f), HD)[None, :]
    cdb = jnp.repeat(jnp.exp(CHUNK * lg_b), HD)[None, :]
    return (mtab, zf, zb, xib, cdb), (xif, cdf)


def _trunk(x, mod, p):
    b, s, _ = x.shape
    n = b * s
    p1, cr, yp, qr, kvf = _mixer_a(x, mod, p["win"], p["glnw"], p["glnb"], p["wsp"], p["bsp"],
                                   _rope_tables(s), p["tabs_a"], p["gnw"])
    xif, cdf = p["tabs_b"]
    x1, h2, logits = _mixer_b(yp, qr, kvf, p1, cr, x, mod, xif, cdf, p["wout"], p["ln1w"], p["ln1b"],
                              p["wr"], p["br"])
    idx, w, rank, cnt = _route(logits)

    counts = cnt[:, 0].astype(i32)
    padded = (counts + EXPERT_ROWS - 1) // EXPERT_ROWS * EXPERT_ROWS
    pends = jnp.cumsum(padded)
    pstarts = pends - padded
    n_blocks = n * TOPK // EXPERT_ROWS + E
    n_rows = n_blocks * EXPERT_ROWS
    block_e = jnp.minimum(jnp.searchsorted(pends, jnp.arange(n_blocks, dtype=i32) * EXPERT_ROWS, side="right"),
                          E - 1).astype(i32)
    dest = pstarts[idx] + rank

    xbuf = _dispatch(h2, _tile_major(dest, DISPATCH_TOKENS), counts, pstarts.astype(i32), padded.astype(i32), n_rows)
    ybuf = _experts(block_e, xbuf, p["wgu"], p["bgu"], p["wd"], p["bd"])
    out = _combine(_tile_major(dest, COMBINE_TOKENS), ybuf, w.T, x1.reshape(n, D), mod, p["ln2w"], p["ln2b"], s)
    return out.reshape(b, s, D)


def kernel(x_prompt, x_sample, c_prompt, c_sample, w_ada, b_ada, w_in, gmlp_ln_w, gmlp_ln_b, w_spatial, b_spatial, ret_theta_fwd, ret_theta_bwd, ret_gn_w, w_out, ln1_w, ln1_b, w_router, b_router, w_gate_up, b_gate_up, w_down, b_down, ln2_w, ln2_b):
    l = 0
    bp, bs = x_prompt.shape[0], x_sample.shape[0]
    c_all = jnp.concatenate([c_prompt, c_sample], axis=0)
    c_rows = -(-c_all.shape[0] // SUBLANES) * SUBLANES
    c_all = jnp.pad(c_all, ((0, c_rows - c_all.shape[0]), (0, 0)))
    mod = _adaln(c_all, w_ada[l], b_ada[l]).reshape(c_rows, 6, D)

    tabs_a, tabs_b = _retention_tables(ret_theta_fwd[l], ret_theta_bwd[l])
    row = lambda a: a.reshape(1, -1)
    p = dict(
        win=w_in[l].astype(bf16), glnw=row(gmlp_ln_w[l]), glnb=row(gmlp_ln_b[l]),
        wsp=w_spatial[l].astype(bf16), bsp=jnp.repeat(b_spatial[l].T, HD, axis=1),
        tabs_a=tabs_a, tabs_b=tabs_b, gnw=row(ret_gn_w[l]),
        wout=w_out[l].astype(bf16), ln1w=row(ln1_w[l]), ln1b=row(ln1_b[l]),
        wr=jnp.pad(w_router[l], ((0, 0), (0, LANES - E))), br=jnp.pad(row(b_router[l]), ((0, 0), (0, LANES - E))),
        wgu=w_gate_up[l].astype(bf16), bgu=b_gate_up[l].reshape(E, 1, 2 * FF),
        wd=w_down[l].astype(bf16), bd=b_down[l].reshape(E, 1, D),
        ln2w=row(ln2_w[l]), ln2b=row(ln2_b[l]),
    )
    y_prompt = _trunk(x_prompt, mod[:bp], p)
    y_sample = _trunk(x_sample, mod[bp:bp + bs], p)
    return (y_prompt, y_sample)
```

```python
import functools

import numpy as np
import jax
import jax.numpy as jnp
from jax import lax
from jax.experimental import pallas as pl
from jax.experimental.pallas import tpu as pltpu

f32 = jnp.float32
bf16 = jnp.bfloat16
i32 = jnp.int32

D = 1024
CHUNK = 128
HEADS = 8
HD = D // HEADS
N_SEG = 8
E = 32
TOPK = 4
FF = 1024
ROPE_BASE = 10000.0
SWIGLU_LIMIT = 7.0
SWIGLU_ALPHA = 1.702
LN_EPS = 1e-5
GN_EPS = 1e-6
DEPTH = 1
DN_ALPHA = (2 * DEPTH) ** 0.25

LANES = 128
SUBLANES = 8
ROW_TILES = D // LANES

MIX_TOKENS = 256
ROUTE_TOKENS = 512
DISPATCH_TOKENS = 512
COMBINE_TOKENS = 256
EXPERT_ROWS = 256
VMEM_LIMIT = 56 * 1024 * 1024

_HI = lax.Precision.HIGHEST


def _const_spec(shape):
    nd = len(shape)
    return pl.BlockSpec(shape, lambda *_: (0,) * nd, pipeline_mode=pl.Buffered(1))


def _gelu(x):
    return 0.5 * x * (1.0 + lax.erf(x * (2.0 ** -0.5)))


def _normalize(x, eps):
    mu = jnp.mean(x, axis=-1, keepdims=True)
    xc = x - mu
    var = jnp.mean(xc * xc, axis=-1, keepdims=True)
    return xc * lax.rsqrt(var + eps)


def _silu(x):
    return x * jax.nn.sigmoid(x)


def _head(h):
    return slice(h * HD, (h + 1) * HD)


def _adaln_body(c_ref, w_ref, b_ref, o_ref):
    s = _silu(c_ref[...])
    o_ref[...] = jnp.dot(s, w_ref[...], preferred_element_type=f32, precision=_HI) + b_ref[...]


def _adaln(c, w_ada, b_ada):
    rows = c.shape[0]
    return pl.pallas_call(
        _adaln_body,
        grid=(6,),
        in_specs=[pl.BlockSpec((rows, D), lambda j: (0, 0)),
                  pl.BlockSpec((D, D), lambda j: (0, j)),
                  pl.BlockSpec((1, D), lambda j: (0, j))],
        out_specs=pl.BlockSpec((rows, D), lambda j: (0, j)),
        out_shape=jax.ShapeDtypeStruct((rows, 6 * D), f32),
        compiler_params=pltpu.CompilerParams(vmem_limit_bytes=VMEM_LIMIT),
        name="adaln",
    )(c, w_ada, b_ada.reshape(1, 6 * D))


def _mixer_a_body(nck, x_ref, mod_ref, win_ref, glnw_ref, glnb_ref, wsp_ref, bsp_ref,
                  cos_ref, sin_ref, mtab_ref, zf_ref, zb_ref, xib_ref, cdb_ref, gnw_ref,
                  p1_ref, cr_ref, yp_ref, qr_ref, kvf_ref, sb_ref):
    @pl.when(pl.program_id(1) == 0)
    def _():
        sb_ref[...] = jnp.zeros_like(sb_ref)

    h = (x_ref[...] * (1.0 + mod_ref[1:2, :]) + mod_ref[0:1, :]).astype(bf16)

    def seg(j):
        return jnp.dot(h, win_ref[:, j * D:(j + 1) * D], preferred_element_type=f32)

    vn = _normalize(_gelu(seg(1)), LN_EPS) * glnw_ref[...] + glnb_ref[...]
    vnb = vn.astype(bf16)
    ug = _gelu(seg(0))
    ga = jax.nn.sigmoid(seg(6))
    for c in range(nck):
        rows = slice(c * CHUNK, (c + 1) * CHUNK)
        mixed = jnp.concatenate(
            [jnp.dot(wsp_ref[g], vnb[rows, _head(g)], preferred_element_type=f32) for g in range(HEADS)], axis=1)
        p1_ref[rows, :] = ga[rows, :] * (ug[rows, :] * (mixed + bsp_ref[...]))

    q = seg(2)
    k = seg(3)
    v = seg(4).astype(bf16)
    cos, sin = cos_ref[...], sin_ref[...]

    def rotary(t):
        return jnp.concatenate([t[:, _head(h_)] * cos + pltpu.roll(t[:, _head(h_)], HD // 2, 1) * sin
                                for h_ in range(HEADS)], axis=1)

    qr = rotary(q).astype(bf16)
    kr = rotary(k)
    qr_ref[...] = qr
    for c in reversed(range(nck)):
        rows = slice(c * CHUNK, (c + 1) * CHUNK)
        kc = kr[rows, :]
        kcb = kc.astype(bf16)
        kzf = (kc * zf_ref[...]).astype(bf16)
        kzb = (kc * zb_ref[...]).astype(bf16)
        sb = sb_ref[...]
        sbb = sb.astype(bf16)
        yps, kvfs, kvbs = [], [], []
        for h_ in range(HEADS):
            hs = _head(h_)
            qh = qr[rows, hs]
            vh = v[rows, hs]
            sc = lax.dot_general(qh, kcb[:, hs], (((1,), (1,)), ((), ())), preferred_element_type=f32)
            sc = (sc * mtab_ref[:, hs]).astype(bf16)
            intra = jnp.dot(sc, vh, preferred_element_type=f32)
            cross = jnp.dot(qh, sbb[:, hs], preferred_element_type=f32) * xib_ref[:, hs]
            yps.append(intra + cross)
            kvfs.append(lax.dot_general(kzf[:, hs], vh, (((0,), (0,)), ((), ())), preferred_element_type=f32))
            kvbs.append(lax.dot_general(kzb[:, hs], vh, (((0,), (0,)), ((), ())), preferred_element_type=f32))
        yp_ref[rows, :] = jnp.concatenate(yps, axis=1)
        kvf_ref[c] = jnp.concatenate(kvfs, axis=1)
        sb_ref[...] = sb * cdb_ref[...] + jnp.concatenate(kvbs, axis=1)

    cr_ref[...] = jax.nn.sigmoid(seg(7)) * _silu(seg(5)) * gnw_ref[...]


def _mixer_a(x, mod, win, glnw, glnb, wsp, bsp, rope, tabs, gnw):
    b, s, _ = x.shape
    t = MIX_TOKENS
    nt, nck = s // t, t // CHUNK
    cos, sin = rope
    mtab, zf, zb, xib, cdb = tabs
    rev = lambda bi, i: (bi, nt - 1 - i, 0)
    rope_spec = pl.BlockSpec((t, HD), lambda bi, i: (nt - 1 - i, 0))
    tok = lambda dt: jax.ShapeDtypeStruct((b, s, D), dt)
    return pl.pallas_call(
        functools.partial(_mixer_a_body, nck),
        grid=(b, nt),
        in_specs=[pl.BlockSpec((None, t, D), rev),
                  pl.BlockSpec((None, 6, D), lambda bi, i: (bi, 0, 0)),
                  _const_spec((D, N_SEG * D)),
                  _const_spec((1, D)), _const_spec((1, D)),
                  _const_spec((HEADS, CHUNK, CHUNK)), _const_spec((CHUNK, D)),
                  rope_spec, rope_spec,
                  _const_spec((CHUNK, D)), _const_spec((CHUNK, D)), _const_spec((CHUNK, D)),
                  _const_spec((CHUNK, D)), _const_spec((1, D)), _const_spec((1, D))],
        out_specs=[pl.BlockSpec((None, t, D), rev)] * 4
                  + [pl.BlockSpec((None, nck, CHUNK, D), lambda bi, i: (bi, nt - 1 - i, 0, 0))],
        out_shape=[tok(f32), tok(f32), tok(f32), tok(bf16),
                   jax.ShapeDtypeStruct((b, s // CHUNK, CHUNK, D), f32)],
        scratch_shapes=[pltpu.VMEM((CHUNK, D), f32)],
        compiler_params=pltpu.CompilerParams(dimension_semantics=("arbitrary", "arbitrary"),
                                             vmem_limit_bytes=VMEM_LIMIT),
        name="mixer_a",
    )(x, mod, win, glnw, glnb, wsp, bsp, cos, sin, mtab, zf, zb, xib, cdb, gnw)


def _mixer_b_body(nck, yp_ref, qr_ref, kvf_ref, p1_ref, cr_ref, x_ref, mod_ref, xif_ref, cdf_ref,
                  wout_ref, ln1w_ref, ln1b_ref, wr_ref, br_ref,
                  x1_ref, h2_ref, lg_ref, sf_ref):
    @pl.when(pl.program_id(1) == 0)
    def _():
        sf_ref[...] = jnp.zeros_like(sf_ref)

    ys = []
    for c in range(nck):
        rows = slice(c * CHUNK, (c + 1) * CHUNK)
        sf = sf_ref[...]
        sfb = sf.astype(bf16)
        cross = jnp.concatenate(
            [jnp.dot(qr_ref[rows, _head(h_)], sfb[:, _head(h_)], preferred_element_type=f32)
             for h_ in range(HEADS)], axis=1)
        ys.append(yp_ref[rows, :] + cross * xif_ref[...])
        sf_ref[...] = sf * cdf_ref[...] + kvf_ref[c]
    y = jnp.concatenate(ys, axis=0)
    yn = jnp.concatenate([_normalize(y[:, _head(h_)], GN_EPS) for h_ in range(HEADS)], axis=1)
    merged = p1_ref[...] + cr_ref[...] * yn
    mix = jnp.dot(merged.astype(bf16), wout_ref[...], preferred_element_type=f32)
    x1 = _normalize(DN_ALPHA * x_ref[...] + mod_ref[2:3, :] * mix, LN_EPS) * ln1w_ref[...] + ln1b_ref[...]
    x1_ref[...] = x1
    h2 = x1 * (1.0 + mod_ref[4:5, :]) + mod_ref[3:4, :]
    for s_ in range(ROW_TILES):
        h2_ref[:, s_, :] = h2[:, s_ * LANES:(s_ + 1) * LANES]
    lg_ref[...] = jnp.dot(h2, wr_ref[...], preferred_element_type=f32, precision=_HI) + br_ref[...]


def _mixer_b(yp, qr, kvf, p1, cr, x, mod, xif, cdf, wout, ln1w, ln1b, wr, br):
    b, s, _ = x.shape
    t = MIX_TOKENS
    nt, nck = s // t, t // CHUNK
    fwd = lambda bi, i: (bi, i, 0)
    flat = lambda bi, i: (bi * nt + i, 0)
    return pl.pallas_call(
        functools.partial(_mixer_b_body, nck),
        grid=(b, nt),
        in_specs=[pl.BlockSpec((None, t, D), fwd), pl.BlockSpec((None, t, D), fwd),
                  pl.BlockSpec((None, nck, CHUNK, D), lambda bi, i: (bi, i, 0, 0)),
                  pl.BlockSpec((None, t, D), fwd), pl.BlockSpec((None, t, D), fwd),
                  pl.BlockSpec((None, t, D), fwd),
                  pl.BlockSpec((None, 6, D), lambda bi, i: (bi, 0, 0)),
                  _const_spec((CHUNK, D)), _const_spec((1, D)),
                  _const_spec((D, D)), _const_spec((1, D)), _const_spec((1, D)),
                  _const_spec((D, LANES)), _const_spec((1, LANES))],
        out_specs=[pl.BlockSpec((None, t, D), fwd),
                   pl.BlockSpec((t, ROW_TILES, LANES), lambda bi, i: (bi * nt + i, 0, 0)),
                   pl.BlockSpec((t, LANES), flat)],
        out_shape=[jax.ShapeDtypeStruct((b, s, D), f32),
                   jax.ShapeDtypeStruct((b * s, ROW_TILES, LANES), f32),
                   jax.ShapeDtypeStruct((b * s, LANES), f32)],
        scratch_shapes=[pltpu.VMEM((CHUNK, D), f32)],
        compiler_params=pltpu.CompilerParams(dimension_semantics=("arbitrary", "arbitrary"),
                                             vmem_limit_bytes=VMEM_LIMIT),
        name="mixer_b",
    )(yp, qr, kvf, p1, cr, x, mod, xif, cdf, wout, ln1w, ln1b, wr, br)


def _route_body(lg_ref, idx_ref, w_ref, rank_ref, cnt_ref, run_ref):
    tt = lg_ref.shape[0]

    @pl.when(pl.program_id(0) == 0)
    def _():
        run_ref[...] = jnp.zeros_like(run_ref)

    l = lg_ref[...].T[:E, :]
    iota_e = lax.broadcasted_iota(i32, (E, tt), 0)
    vals, idxs, hots = [], [], []
    for _ in range(TOPK):
        m = jnp.max(l, axis=0, keepdims=True)
        idx = jnp.min(jnp.where(l == m, iota_e, E), axis=0, keepdims=True)
        hot = iota_e == idx
        l = jnp.where(hot, -jnp.inf, l)
        vals.append(m)
        idxs.append(idx)
        hots.append(hot)
    exps = [jnp.exp(v_ - vals[0]) for v_ in vals]
    tot = exps[0] + exps[1] + exps[2] + exps[3]
    w_ref[...] = jnp.concatenate([e_ / tot for e_ in exps], axis=0)
    idx_ref[...] = jnp.concatenate(idxs, axis=0)

    sel = jnp.zeros((E, tt), f32)
    for hot in hots:
        sel = sel + hot.astype(f32)
    upper = (lax.broadcasted_iota(i32, (tt, tt), 0) < lax.broadcasted_iota(i32, (tt, tt), 1)).astype(bf16)
    run = run_ref[:, 0:1]
    rank_e = jnp.dot(sel.astype(bf16), upper, preferred_element_type=f32) + run
    ranks = [jnp.sum(jnp.where(hot, rank_e, 0.0), axis=0, keepdims=True) for hot in hots]
    rank_ref[...] = jnp.concatenate(ranks, axis=0).astype(i32)
    run_new = run + jnp.sum(sel, axis=1, keepdims=True)
    run_ref[...] = jnp.broadcast_to(run_new, run_ref.shape)
    cnt_ref[...] = jnp.broadcast_to(run_new, cnt_ref.shape)


def _route(logits):
    n = logits.shape[0]
    tt = ROUTE_TOKENS
    col = lambda i: (0, i)
    return pl.pallas_call(
        _route_body,
        grid=(n // tt,),
        in_specs=[pl.BlockSpec((tt, LANES), lambda i: (i, 0))],
        out_specs=[pl.BlockSpec((TOPK, tt), col), pl.BlockSpec((TOPK, tt), col), pl.BlockSpec((TOPK, tt), col),
                   pl.BlockSpec((E, LANES), lambda i: (0, 0))],
        out_shape=[jax.ShapeDtypeStruct((TOPK, n), i32), jax.ShapeDtypeStruct((TOPK, n), f32),
                   jax.ShapeDtypeStruct((TOPK, n), i32), jax.ShapeDtypeStruct((E, LANES), f32)],
        scratch_shapes=[pltpu.VMEM((E, LANES), f32)],
        compiler_params=pltpu.CompilerParams(dimension_semantics=("arbitrary",), vmem_limit_bytes=VMEM_LIMIT),
        name="route",
    )(logits)


def _row_copy(src, dst, sem):
    return pltpu.make_async_copy(src, dst, sem)


def _dispatch_body(n_pad_rows, cnt_ref, pst_ref, pad_ref, dest_ref, h2_ref, xbuf_ref, zrow_ref, sem, zsem):
    tt = dest_ref.shape[0] // TOPK
    i = pl.program_id(0)
    n_rows = xbuf_ref.shape[0]

    def issue(j, carry):
        src = h2_ref.at[j]
        for k_ in range(TOPK):
            _row_copy(src, xbuf_ref.at[dest_ref[k_ * tt + j]], sem).start()
        return carry

    lax.fori_loop(0, tt, issue, 0)

    @pl.when(i == pl.num_programs(0) - 1)
    def _():
        zrow_ref[...] = jnp.zeros_like(zrow_ref)

        def zero_row(r, carry):
            _row_copy(zrow_ref, xbuf_ref.at[r], zsem).start()
            return carry

        def per_expert(e_, carry):
            lax.fori_loop(pst_ref[e_] + cnt_ref[e_], pst_ref[e_] + pad_ref[e_], zero_row, 0)
            return carry

        lax.fori_loop(0, E, per_expert, 0)
        lax.fori_loop(pst_ref[E - 1] + pad_ref[E - 1], n_rows, zero_row, 0)
        _row_copy(xbuf_ref.at[pl.ds(0, n_pad_rows)], xbuf_ref.at[pl.ds(0, n_pad_rows)], zsem).wait()

    _row_copy(xbuf_ref.at[pl.ds(0, TOPK * tt)], xbuf_ref.at[pl.ds(0, TOPK * tt)], sem).wait()


def _dispatch(h2, dest_tiles, counts, pstarts, padded, n_rows):
    n = h2.shape[0]
    tt = DISPATCH_TOKENS
    return pl.pallas_call(
        functools.partial(_dispatch_body, n_rows - n * TOPK),
        grid_spec=pltpu.PrefetchScalarGridSpec(
            num_scalar_prefetch=3,
            grid=(n // tt,),
            in_specs=[pl.BlockSpec((TOPK * tt,), lambda i, *_: (i,), memory_space=pltpu.SMEM),
                      pl.BlockSpec((tt, ROW_TILES, LANES), lambda i, *_: (i, 0, 0))],
            out_specs=pl.BlockSpec(memory_space=pl.ANY),
            scratch_shapes=[pltpu.VMEM((ROW_TILES, LANES), f32),
                            pltpu.SemaphoreType.DMA(()), pltpu.SemaphoreType.DMA(())]),
        out_shape=jax.ShapeDtypeStruct((n_rows, ROW_TILES, LANES), f32),
        compiler_params=pltpu.CompilerParams(dimension_semantics=("arbitrary",), vmem_limit_bytes=VMEM_LIMIT),
        name="dispatch",
    )(counts, pstarts, padded, dest_tiles, h2)


def _expert_body(be_ref, x_ref, wgu_ref, bgu_ref, wd_ref, bd_ref, y_ref):
    x = jnp.concatenate([x_ref[:, s_, :] for s_ in range(ROW_TILES)], axis=1).astype(bf16)
    gu = jnp.dot(x, wgu_ref[...], preferred_element_type=f32) + bgu_ref[...]
    gate = jnp.minimum(gu[:, :FF], SWIGLU_LIMIT)
    up = jnp.clip(gu[:, FF:], -SWIGLU_LIMIT, SWIGLU_LIMIT)
    act = (up + 1.0) * gate * jax.nn.sigmoid(SWIGLU_ALPHA * gate)
    y = jnp.dot(act.astype(bf16), wd_ref[...], preferred_element_type=f32) + bd_ref[...]
    for s_ in range(ROW_TILES):
        y_ref[:, s_, :] = y[:, s_ * LANES:(s_ + 1) * LANES]


def _experts(block_e, xbuf, wgu, bgu, wd, bd):
    n_rows = xbuf.shape[0]
    r = EXPERT_ROWS
    row_spec = pl.BlockSpec((r, ROW_TILES, LANES), lambda b_, be: (b_, 0, 0))
    return pl.pallas_call(
        _expert_body,
        grid_spec=pltpu.PrefetchScalarGridSpec(
            num_scalar_prefetch=1,
            grid=(n_rows // r,),
            in_specs=[row_spec,
                      pl.BlockSpec((None, D, 2 * FF), lambda b_, be: (be[b_], 0, 0)),
                      pl.BlockSpec((None, 1, 2 * FF), lambda b_, be: (be[b_], 0, 0)),
                      pl.BlockSpec((None, FF, D), lambda b_, be: (be[b_], 0, 0)),
                      pl.BlockSpec((None, 1, D), lambda b_, be: (be[b_], 0, 0))],
            out_specs=row_spec),
        out_shape=jax.ShapeDtypeStruct((n_rows, ROW_TILES, LANES), f32),
        compiler_params=pltpu.CompilerParams(dimension_semantics=("arbitrary",), vmem_limit_bytes=VMEM_LIMIT),
        name="experts",
    )(block_e, xbuf, wgu, bgu, wd, bd)


def _combine_body(dest_ref, ybuf_ref, w_ref, x1_ref, mod_ref, ln2w_ref, ln2b_ref, o_ref, rows_ref, sem):
    tt = x1_ref.shape[0]

    def issue(j, carry):
        _row_copy(ybuf_ref.at[dest_ref[j]], rows_ref.at[j], sem).start()
        return carry

    lax.fori_loop(0, TOPK * tt, issue, 0)
    _row_copy(ybuf_ref.at[pl.ds(0, TOPK * tt)], rows_ref, sem).wait()

    w = w_ref[...]
    ff = jnp.zeros((tt, D), f32)
    for k_ in range(TOPK):
        yk = jnp.concatenate([rows_ref[k_ * tt:(k_ + 1) * tt, s_, :] for s_ in range(ROW_TILES)], axis=1)
        ff = ff + yk * w[:, k_:k_ + 1]
    z = DN_ALPHA * x1_ref[...] + mod_ref[5:6, :] * ff
    o_ref[...] = _normalize(z, LN_EPS) * ln2w_ref[...] + ln2b_ref[...]


def _combine(dest_tiles, ybuf, w_t, x1, mod, ln2w, ln2b, seq):
    n = x1.shape[0]
    tt = COMBINE_TOKENS
    return pl.pallas_call(
        _combine_body,
        grid=(n // tt,),
        in_specs=[pl.BlockSpec((TOPK * tt,), lambda i: (i,), memory_space=pltpu.SMEM),
                  pl.BlockSpec(memory_space=pl.ANY),
                  pl.BlockSpec((tt, TOPK), lambda i: (i, 0)),
                  pl.BlockSpec((tt, D), lambda i: (i, 0)),
                  pl.BlockSpec((None, 6, D), lambda i: ((i * tt) // seq, 0, 0)),
                  _const_spec((1, D)), _const_spec((1, D))],
        out_specs=pl.BlockSpec((tt, D), lambda i: (i, 0)),
        out_shape=jax.ShapeDtypeStruct((n, D), f32),
        scratch_shapes=[pltpu.VMEM((TOPK * tt, ROW_TILES, LANES), f32), pltpu.SemaphoreType.DMA(())],
        compiler_params=pltpu.CompilerParams(dimension_semantics=("arbitrary",), vmem_limit_bytes=VMEM_LIMIT),
        name="combine",
    )(dest_tiles, ybuf, w_t, x1, mod, ln2w, ln2b)


def _tile_major(a, tt):
    n = a.shape[1]
    return a.reshape(TOPK, n // tt, tt).transpose(1, 0, 2).reshape(-1)


def _rope_tables(s):
    half = HD // 2
    nf32 = np.float32
    inv_freq = np.power(nf32(ROPE_BASE), (nf32(-2.0) * np.arange(half, dtype=nf32) / nf32(HD)).astype(nf32)).astype(nf32)
    ang = (np.arange(s, dtype=nf32)[:, None] * inv_freq[None, :]).astype(nf32).astype(np.float64)
    cos, sin = np.cos(ang).astype(nf32), np.sin(ang).astype(nf32)
    return jnp.asarray(np.concatenate([cos, cos], axis=1)), jnp.asarray(np.concatenate([-sin, sin], axis=1))


def _retention_tables(theta_f, theta_b):
    lg_f = jax.nn.log_sigmoid(theta_f.astype(f32))
    lg_b = jax.nn.log_sigmoid(theta_b.astype(f32))
    idx = jnp.arange(CHUNK, dtype=f32)
    diff = idx[:, None] - idx[None, :]
    dec_f = jnp.where(diff >= 0, jnp.exp(jnp.maximum(diff, 0.0)[None] * lg_f[:, None, None]), 0.0)
    dec_b = jnp.where(diff < 0, jnp.exp(jnp.maximum(-diff, 0.0)[None] * lg_b[:, None, None]), 0.0)
    kscale = HD ** -0.5
    mtab = ((dec_f + dec_b) * kscale).transpose(1, 0, 2).reshape(CHUNK, D)
    lanes = lambda t: jnp.repeat(t.T, HD, axis=1)
    zf = lanes(jnp.exp((CHUNK - 1.0 - idx)[None, :] * lg_f[:, None]) * kscale)
    zb = lanes(jnp.exp(idx[None, :] * lg_b[:, None]) * kscale)
    xif = lanes(jnp.exp((idx + 1.0)[None, :] * lg_f[:, None]))
    xib = lanes(jnp.exp((CHUNK - idx)[None, :] * lg_b[:, None]))
    cdf = jnp.repeat(jnp.exp(CHUNK * lg_f), HD)[None, :]
    cdb = jnp.repeat(jnp.exp(CHUNK * lg_b), HD)[None, :]
    return (mtab, zf, zb, xib, cdb), (xif, cdf)


def _trunk(x, mod, p):
    b, s, _ = x.shape
    n = b * s
    p1, cr, yp, qr, kvf = _mixer_a(x, mod, p["win"], p["glnw"], p["glnb"], p["wsp"], p["bsp"],
                                   _rope_tables(s), p["tabs_a"], p["gnw"])
    xif, cdf = p["tabs_b"]
    x1, h2, logits = _mixer_b(yp, qr, kvf, p1, cr, x, mod, xif, cdf, p["wout"], p["ln1w"], p["ln1b"],
                              p["wr"], p["br"])
    idx, w, rank, cnt = _route(logits)

    counts = cnt[:, 0].astype(i32)
    padded = (counts + EXPERT_ROWS - 1) // EXPERT_ROWS * EXPERT_ROWS
    pends = jnp.cumsum(padded)
    pstarts = pends - padded
    n_blocks = n * TOPK // EXPERT_ROWS + E
    n_rows = n_blocks * EXPERT_ROWS
    block_start = jnp.arange(n_blocks, dtype=i32) * EXPERT_ROWS
    block_e = jnp.minimum(jnp.sum((pends[None, :] <= block_start[:, None]).astype(i32), axis=1), E - 1)
    dest = rank
    for e_ in range(E):
        dest = dest + jnp.where(idx == e_, pstarts[e_], 0)

    xbuf = _dispatch(h2, _tile_major(dest, DISPATCH_TOKENS), counts, pstarts.astype(i32), padded.astype(i32), n_rows)
    ybuf = _experts(block_e, xbuf, p["wgu"], p["bgu"], p["wd"], p["bd"])
    out = _combine(_tile_major(dest, COMBINE_TOKENS), ybuf, w.T, x1.reshape(n, D), mod, p["ln2w"], p["ln2b"], s)
    return out.reshape(b, s, D)


def kernel(x_prompt, x_sample, c_prompt, c_sample, w_ada, b_ada, w_in, gmlp_ln_w, gmlp_ln_b, w_spatial, b_spatial, ret_theta_fwd, ret_theta_bwd, ret_gn_w, w_out, ln1_w, ln1_b, w_router, b_router, w_gate_up, b_gate_up, w_down, b_down, ln2_w, ln2_b):
    l = 0
    bp, bs = x_prompt.shape[0], x_sample.shape[0]
    c_all = jnp.concatenate([c_prompt, c_sample], axis=0)
    c_rows = -(-c_all.shape[0] // SUBLANES) * SUBLANES
    c_all = jnp.pad(c_all, ((0, c_rows - c_all.shape[0]), (0, 0)))
    mod = _adaln(c_all, w_ada[l], b_ada[l]).reshape(c_rows, 6, D)

    tabs_a, tabs_b = _retention_tables(ret_theta_fwd[l], ret_theta_bwd[l])
    row = lambda a: a.reshape(1, -1)
    p = dict(
        win=w_in[l].astype(bf16), glnw=row(gmlp_ln_w[l]), glnb=row(gmlp_ln_b[l]),
        wsp=w_spatial[l].astype(bf16), bsp=jnp.repeat(b_spatial[l].T, HD, axis=1),
        tabs_a=tabs_a, tabs_b=tabs_b, gnw=row(ret_gn_w[l]),
        wout=w_out[l].astype(bf16), ln1w=row(ln1_w[l]), ln1b=row(ln1_b[l]),
        wr=jnp.pad(w_router[l], ((0, 0), (0, LANES - E))), br=jnp.pad(row(b_router[l]), ((0, 0), (0, LANES - E))),
        wgu=w_gate_up[l].astype(bf16), bgu=b_gate_up[l].reshape(E, 1, 2 * FF),
        wd=w_down[l].astype(bf16), bd=b_down[l].reshape(E, 1, D),
        ln2w=row(ln2_w[l]), ln2b=row(ln2_b[l]),
    )
    y_prompt = _trunk(x_prompt, mod[:bp], p)
    y_sample = _trunk(x_sample, mod[bp:bp + bs], p)
    return (y_prompt, y_sample)
```

```python
import functools

import numpy as np
import jax
import jax.numpy as jnp
from jax import lax
from jax.experimental import pallas as pl
from jax.experimental.pallas import tpu as pltpu

f32 = jnp.float32
bf16 = jnp.bfloat16
i32 = jnp.int32

D = 1024
CHUNK = 128
HEADS = 8
HD = D // HEADS
N_SEG = 8
E = 32
TOPK = 4
FF = 1024
ROPE_BASE = 10000.0
SWIGLU_LIMIT = 7.0
SWIGLU_ALPHA = 1.702
LN_EPS = 1e-5
GN_EPS = 1e-6
DEPTH = 1
DN_ALPHA = (2 * DEPTH) ** 0.25

LANES = 128
SUBLANES = 8
ROW_TILES = D // LANES

MIX_TOKENS = 256
ROUTE_TOKENS = 512
DISPATCH_TOKENS = 512
COMBINE_TOKENS = 256
EXPERT_ROWS = 256
VMEM_LIMIT = 56 * 1024 * 1024

_HI = lax.Precision.HIGHEST


def _const_spec(shape):
    nd = len(shape)
    return pl.BlockSpec(shape, lambda *_: (0,) * nd, pipeline_mode=pl.Buffered(1))


def _gelu(x):
    return 0.5 * x * (1.0 + lax.erf(x * (2.0 ** -0.5)))


def _normalize(x, eps):
    mu = jnp.mean(x, axis=-1, keepdims=True)
    xc = x - mu
    var = jnp.mean(xc * xc, axis=-1, keepdims=True)
    return xc * lax.rsqrt(var + eps)


def _silu(x):
    return x * jax.nn.sigmoid(x)


def _head(h):
    return slice(h * HD, (h + 1) * HD)


def _adaln_body(c_ref, w_ref, b_ref, o_ref):
    s = _silu(c_ref[...])
    o_ref[...] = jnp.dot(s, w_ref[...], preferred_element_type=f32, precision=_HI) + b_ref[...]


def _adaln(c, w_ada, b_ada):
    rows = c.shape[0]
    return pl.pallas_call(
        _adaln_body,
        grid=(6,),
        in_specs=[pl.BlockSpec((rows, D), lambda j: (0, 0)),
                  pl.BlockSpec((D, D), lambda j: (0, j)),
                  pl.BlockSpec((1, D), lambda j: (0, j))],
        out_specs=pl.BlockSpec((rows, D), lambda j: (0, j)),
        out_shape=jax.ShapeDtypeStruct((rows, 6 * D), f32),
        compiler_params=pltpu.CompilerParams(vmem_limit_bytes=VMEM_LIMIT),
        name="adaln",
    )(c, w_ada, b_ada.reshape(1, 6 * D))


def _mixer_a_body(nck, x_ref, mod_ref, win_ref, glnw_ref, glnb_ref, wsp_ref, bsp_ref,
                  cos_ref, sin_ref, mtab_ref, zf_ref, zb_ref, xib_ref, cdb_ref, gnw_ref,
                  p1_ref, cr_ref, yp_ref, qr_ref, kvf_ref, sb_ref):
    @pl.when(pl.program_id(1) == 0)
    def _():
        sb_ref[...] = jnp.zeros_like(sb_ref)

    h = (x_ref[...] * (1.0 + mod_ref[1:2, :]) + mod_ref[0:1, :]).astype(bf16)

    def seg(j):
        return jnp.dot(h, win_ref[:, j * D:(j + 1) * D], preferred_element_type=f32)

    vn = _normalize(_gelu(seg(1)), LN_EPS) * glnw_ref[...] + glnb_ref[...]
    vnb = vn.astype(bf16)
    ug = _gelu(seg(0))
    ga = jax.nn.sigmoid(seg(6))
    for c in range(nck):
        rows = slice(c * CHUNK, (c + 1) * CHUNK)
        mixed = jnp.concatenate(
            [jnp.dot(wsp_ref[g], vnb[rows, _head(g)], preferred_element_type=f32) for g in range(HEADS)], axis=1)
        p1_ref[rows, :] = ga[rows, :] * (ug[rows, :] * (mixed + bsp_ref[...]))

    q = seg(2)
    k = seg(3)
    v = seg(4).astype(bf16)
    cos, sin = cos_ref[...], sin_ref[...]

    def rotary(t):
        return jnp.concatenate([t[:, _head(h_)] * cos + pltpu.roll(t[:, _head(h_)], HD // 2, 1) * sin
                                for h_ in range(HEADS)], axis=1)

    qr = rotary(q).astype(bf16)
    kr = rotary(k)
    qr_ref[...] = qr
    for c in reversed(range(nck)):
        rows = slice(c * CHUNK, (c + 1) * CHUNK)
        kc = kr[rows, :]
        kcb = kc.astype(bf16)
        kzf = (kc * zf_ref[...]).astype(bf16)
        kzb = (kc * zb_ref[...]).astype(bf16)
        sb = sb_ref[...]
        sbb = sb.astype(bf16)
        yps, kvfs, kvbs = [], [], []
        for h_ in range(HEADS):
            hs = _head(h_)
            qh = qr[rows, hs]
            vh = v[rows, hs]
            sc = lax.dot_general(qh, kcb[:, hs], (((1,), (1,)), ((), ())), preferred_element_type=f32)
            sc = (sc * mtab_ref[:, hs]).astype(bf16)
            intra = jnp.dot(sc, vh, preferred_element_type=f32)
            cross = jnp.dot(qh, sbb[:, hs], preferred_element_type=f32) * xib_ref[:, hs]
            yps.append(intra + cross)
            kvfs.append(lax.dot_general(kzf[:, hs], vh, (((0,), (0,)), ((), ())), preferred_element_type=f32))
            kvbs.append(lax.dot_general(kzb[:, hs], vh, (((0,), (0,)), ((), ())), preferred_element_type=f32))
        yp_ref[rows, :] = jnp.concatenate(yps, axis=1)
        kvf_ref[c] = jnp.concatenate(kvfs, axis=1)
        sb_ref[...] = sb * cdb_ref[...] + jnp.concatenate(kvbs, axis=1)

    cr_ref[...] = jax.nn.sigmoid(seg(7)) * _silu(seg(5)) * gnw_ref[...]


def _mixer_a(x, mod, win, glnw, glnb, wsp, bsp, rope, tabs, gnw):
    b, s, _ = x.shape
    t = MIX_TOKENS
    nt, nck = s // t, t // CHUNK
    cos, sin = rope
    mtab, zf, zb, xib, cdb = tabs
    rev = lambda bi, i: (bi, nt - 1 - i, 0)
    rope_spec = pl.BlockSpec((t, HD), lambda bi, i: (nt - 1 - i, 0))
    tok = lambda dt: jax.ShapeDtypeStruct((b, s, D), dt)
    return pl.pallas_call(
        functools.partial(_mixer_a_body, nck),
        grid=(b, nt),
        in_specs=[pl.BlockSpec((None, t, D), rev),
                  pl.BlockSpec((None, 6, D), lambda bi, i: (bi, 0, 0)),
                  _const_spec((D, N_SEG * D)),
                  _const_spec((1, D)), _const_spec((1, D)),
                  _const_spec((HEADS, CHUNK, CHUNK)), _const_spec((CHUNK, D)),
                  rope_spec, rope_spec,
                  _const_spec((CHUNK, D)), _const_spec((CHUNK, D)), _const_spec((CHUNK, D)),
                  _const_spec((CHUNK, D)), _const_spec((1, D)), _const_spec((1, D))],
        out_specs=[pl.BlockSpec((None, t, D), rev)] * 4
                  + [pl.BlockSpec((None, nck, CHUNK, D), lambda bi, i: (bi, nt - 1 - i, 0, 0))],
        out_shape=[tok(f32), tok(f32), tok(f32), tok(bf16),
                   jax.ShapeDtypeStruct((b, s // CHUNK, CHUNK, D), f32)],
        scratch_shapes=[pltpu.VMEM((CHUNK, D), f32)],
        compiler_params=pltpu.CompilerParams(dimension_semantics=("arbitrary", "arbitrary"),
                                             vmem_limit_bytes=VMEM_LIMIT),
        name="mixer_a",
    )(x, mod, win, glnw, glnb, wsp, bsp, cos, sin, mtab, zf, zb, xib, cdb, gnw)


def _mixer_b_body(nck, yp_ref, qr_ref, kvf_ref, p1_ref, cr_ref, x_ref, mod_ref, xif_ref, cdf_ref,
                  wout_ref, ln1w_ref, ln1b_ref, wr_ref, br_ref,
                  x1_ref, h2_ref, lg_ref, sf_ref):
    @pl.when(pl.program_id(1) == 0)
    def _():
        sf_ref[...] = jnp.zeros_like(sf_ref)

    ys = []
    for c in range(nck):
        rows = slice(c * CHUNK, (c + 1) * CHUNK)
        sf = sf_ref[...]
        sfb = sf.astype(bf16)
        cross = jnp.concatenate(
            [jnp.dot(qr_ref[rows, _head(h_)], sfb[:, _head(h_)], preferred_element_type=f32)
             for h_ in range(HEADS)], axis=1)
        ys.append(yp_ref[rows, :] + cross * xif_ref[...])
        sf_ref[...] = sf * cdf_ref[...] + kvf_ref[c]
    y = jnp.concatenate(ys, axis=0)
    yn = jnp.concatenate([_normalize(y[:, _head(h_)], GN_EPS) for h_ in range(HEADS)], axis=1)
    merged = p1_ref[...] + cr_ref[...] * yn
    mix = jnp.dot(merged.astype(bf16), wout_ref[...], preferred_element_type=f32)
    x1 = _normalize(DN_ALPHA * x_ref[...] + mod_ref[2:3, :] * mix, LN_EPS) * ln1w_ref[...] + ln1b_ref[...]
    x1_ref[...] = x1
    h2 = x1 * (1.0 + mod_ref[4:5, :]) + mod_ref[3:4, :]
    h2_ref[...] = h2
    lg_ref[...] = jnp.dot(h2, wr_ref[...], preferred_element_type=f32, precision=_HI) + br_ref[...]


def _mixer_b(yp, qr, kvf, p1, cr, x, mod, xif, cdf, wout, ln1w, ln1b, wr, br):
    b, s, _ = x.shape
    t = MIX_TOKENS
    nt, nck = s // t, t // CHUNK
    fwd = lambda bi, i: (bi, i, 0)
    flat = lambda bi, i: (bi * nt + i, 0)
    return pl.pallas_call(
        functools.partial(_mixer_b_body, nck),
        grid=(b, nt),
        in_specs=[pl.BlockSpec((None, t, D), fwd), pl.BlockSpec((None, t, D), fwd),
                  pl.BlockSpec((None, nck, CHUNK, D), lambda bi, i: (bi, i, 0, 0)),
                  pl.BlockSpec((None, t, D), fwd), pl.BlockSpec((None, t, D), fwd),
                  pl.BlockSpec((None, t, D), fwd),
                  pl.BlockSpec((None, 6, D), lambda bi, i: (bi, 0, 0)),
                  _const_spec((CHUNK, D)), _const_spec((1, D)),
                  _const_spec((D, D)), _const_spec((1, D)), _const_spec((1, D)),
                  _const_spec((D, LANES)), _const_spec((1, LANES))],
        out_specs=[pl.BlockSpec((None, t, D), fwd),
                   pl.BlockSpec((t, D), flat),
                   pl.BlockSpec((t, LANES), flat)],
        out_shape=[jax.ShapeDtypeStruct((b, s, D), f32),
                   jax.ShapeDtypeStruct((b * s, D), f32),
                   jax.ShapeDtypeStruct((b * s, LANES), f32)],
        scratch_shapes=[pltpu.VMEM((CHUNK, D), f32)],
        compiler_params=pltpu.CompilerParams(dimension_semantics=("arbitrary", "arbitrary"),
                                             vmem_limit_bytes=VMEM_LIMIT),
        name="mixer_b",
    )(yp, qr, kvf, p1, cr, x, mod, xif, cdf, wout, ln1w, ln1b, wr, br)


def _route_body(lg_ref, idx_ref, w_ref, rank_ref, cnt_ref, run_ref):
    tt = lg_ref.shape[0]

    @pl.when(pl.program_id(0) == 0)
    def _():
        run_ref[...] = jnp.zeros_like(run_ref)

    l = lg_ref[...].T[:E, :]
    iota_e = lax.broadcasted_iota(i32, (E, tt), 0)
    vals, idxs, hots = [], [], []
    for _ in range(TOPK):
        m = jnp.max(l, axis=0, keepdims=True)
        idx = jnp.min(jnp.where(l == m, iota_e, E), axis=0, keepdims=True)
        hot = iota_e == idx
        l = jnp.where(hot, -jnp.inf, l)
        vals.append(m)
        idxs.append(idx)
        hots.append(hot)
    exps = [jnp.exp(v_ - vals[0]) for v_ in vals]
    tot = exps[0] + exps[1] + exps[2] + exps[3]
    w_ref[...] = jnp.concatenate([e_ / tot for e_ in exps], axis=0)
    idx_ref[...] = jnp.concatenate(idxs, axis=0)

    sel = jnp.zeros((E, tt), f32)
    for hot in hots:
        sel = sel + hot.astype(f32)
    upper = (lax.broadcasted_iota(i32, (tt, tt), 0) < lax.broadcasted_iota(i32, (tt, tt), 1)).astype(bf16)
    run = run_ref[:, 0:1]
    rank_e = jnp.dot(sel.astype(bf16), upper, preferred_element_type=f32) + run
    ranks = [jnp.sum(jnp.where(hot, rank_e, 0.0), axis=0, keepdims=True) for hot in hots]
    rank_ref[...] = jnp.concatenate(ranks, axis=0).astype(i32)
    run_new = run + jnp.sum(sel, axis=1, keepdims=True)
    run_ref[...] = jnp.broadcast_to(run_new, run_ref.shape)
    cnt_ref[...] = jnp.broadcast_to(run_new, cnt_ref.shape)


def _route(logits):
    n = logits.shape[0]
    tt = ROUTE_TOKENS
    col = lambda i: (0, i)
    return pl.pallas_call(
        _route_body,
        grid=(n // tt,),
        in_specs=[pl.BlockSpec((tt, LANES), lambda i: (i, 0))],
        out_specs=[pl.BlockSpec((TOPK, tt), col), pl.BlockSpec((TOPK, tt), col), pl.BlockSpec((TOPK, tt), col),
                   pl.BlockSpec((E, LANES), lambda i: (0, 0))],
        out_shape=[jax.ShapeDtypeStruct((TOPK, n), i32), jax.ShapeDtypeStruct((TOPK, n), f32),
                   jax.ShapeDtypeStruct((TOPK, n), i32), jax.ShapeDtypeStruct((E, LANES), f32)],
        scratch_shapes=[pltpu.VMEM((E, LANES), f32)],
        compiler_params=pltpu.CompilerParams(dimension_semantics=("arbitrary",), vmem_limit_bytes=VMEM_LIMIT),
        name="route",
    )(logits)


def _row_copy(src, dst, sem):
    return pltpu.make_async_copy(src, dst, sem)


def _dispatch_body(n_pad_rows, cnt_ref, pst_ref, pad_ref, dest_ref, h2_ref, xbuf_ref, zrow_ref, sem, zsem):
    tt = dest_ref.shape[0] // TOPK
    i = pl.program_id(0)
    n_rows = xbuf_ref.shape[0]

    def issue(j, carry):
        src = h2_ref.at[pl.ds(j, 1)]
        for k_ in range(TOPK):
            _row_copy(src, xbuf_ref.at[pl.ds(dest_ref[k_ * tt + j], 1)], sem).start()
        return carry

    lax.fori_loop(0, tt, issue, 0, unroll=2)

    @pl.when(i == pl.num_programs(0) - 1)
    def _():
        zrow_ref[...] = jnp.zeros_like(zrow_ref)

        def zero_row(r, carry):
            _row_copy(zrow_ref.at[pl.ds(0, 1)], xbuf_ref.at[pl.ds(r, 1)], zsem).start()
            return carry

        def per_expert(e_, carry):
            lax.fori_loop(pst_ref[e_] + cnt_ref[e_], pst_ref[e_] + pad_ref[e_], zero_row, 0)
            return carry

        lax.fori_loop(0, E, per_expert, 0)
        lax.fori_loop(pst_ref[E - 1] + pad_ref[E - 1], n_rows, zero_row, 0)
        _row_copy(xbuf_ref.at[pl.ds(0, n_pad_rows)], xbuf_ref.at[pl.ds(0, n_pad_rows)], zsem).wait()

    _row_copy(xbuf_ref.at[pl.ds(0, TOPK * tt)], xbuf_ref.at[pl.ds(0, TOPK * tt)], sem).wait()


def _dispatch(h2, dest_tiles, counts, pstarts, padded, n_rows):
    n = h2.shape[0]
    tt = DISPATCH_TOKENS
    return pl.pallas_call(
        functools.partial(_dispatch_body, n_rows - n * TOPK),
        grid_spec=pltpu.PrefetchScalarGridSpec(
            num_scalar_prefetch=3,
            grid=(n // tt,),
            in_specs=[pl.BlockSpec((TOPK * tt,), lambda i, *_: (i,), memory_space=pltpu.SMEM),
                      pl.BlockSpec((tt, D), lambda i, *_: (i, 0))],
            out_specs=pl.BlockSpec(memory_space=pl.ANY),
            scratch_shapes=[pltpu.VMEM((SUBLANES, D), f32),
                            pltpu.SemaphoreType.DMA(()), pltpu.SemaphoreType.DMA(())]),
        out_shape=jax.ShapeDtypeStruct((n_rows, D), f32),
        compiler_params=pltpu.CompilerParams(dimension_semantics=("arbitrary",), vmem_limit_bytes=VMEM_LIMIT),
        name="dispatch",
    )(counts, pstarts, padded, dest_tiles, h2)


def _expert_body(be_ref, x_ref, wgu_ref, bgu_ref, wd_ref, bd_ref, y_ref):
    gu = jnp.dot(x_ref[...].astype(bf16), wgu_ref[...], preferred_element_type=f32) + bgu_ref[...]
    gate = jnp.minimum(gu[:, :FF], SWIGLU_LIMIT)
    up = jnp.clip(gu[:, FF:], -SWIGLU_LIMIT, SWIGLU_LIMIT)
    act = (up + 1.0) * gate * jax.nn.sigmoid(SWIGLU_ALPHA * gate)
    y_ref[...] = jnp.dot(act.astype(bf16), wd_ref[...], preferred_element_type=f32) + bd_ref[...]


def _experts(block_e, xbuf, wgu, bgu, wd, bd):
    n_rows = xbuf.shape[0]
    r = EXPERT_ROWS
    row_spec = pl.BlockSpec((r, D), lambda b_, be: (b_, 0))
    return pl.pallas_call(
        _expert_body,
        grid_spec=pltpu.PrefetchScalarGridSpec(
            num_scalar_prefetch=1,
            grid=(n_rows // r,),
            in_specs=[row_spec,
                      pl.BlockSpec((None, D, 2 * FF), lambda b_, be: (be[b_], 0, 0)),
                      pl.BlockSpec((None, 1, 2 * FF), lambda b_, be: (be[b_], 0, 0)),
                      pl.BlockSpec((None, FF, D), lambda b_, be: (be[b_], 0, 0)),
                      pl.BlockSpec((None, 1, D), lambda b_, be: (be[b_], 0, 0))],
            out_specs=row_spec),
        out_shape=jax.ShapeDtypeStruct((n_rows, D), f32),
        compiler_params=pltpu.CompilerParams(dimension_semantics=("arbitrary",), vmem_limit_bytes=VMEM_LIMIT),
        name="experts",
    )(block_e, xbuf, wgu, bgu, wd, bd)


def _combine_body(dest_ref, dest_next_ref, ybuf_ref, w_ref, x1_ref, mod_ref, ln2w_ref, ln2b_ref, o_ref,
                  rows_ref, sem):
    tt = x1_ref.shape[0]
    i = pl.program_id(0)
    slot = i % 2

    def gather(idx_ref, slot_):
        def issue(j, carry):
            _row_copy(ybuf_ref.at[pl.ds(idx_ref[j], 1)], rows_ref.at[slot_, pl.ds(j, 1)], sem.at[slot_]).start()
            return carry

        lax.fori_loop(0, TOPK * tt, issue, 0, unroll=4)

    @pl.when(i == 0)
    def _():
        gather(dest_ref, 0)

    @pl.when(i + 1 < pl.num_programs(0))
    def _():
        gather(dest_next_ref, 1 - slot)

    _row_copy(ybuf_ref.at[pl.ds(0, TOPK * tt)], rows_ref.at[slot], sem.at[slot]).wait()

    w = w_ref[...]
    ff = jnp.zeros((tt, D), f32)
    for k_ in range(TOPK):
        ff = ff + rows_ref[slot, k_ * tt:(k_ + 1) * tt, :] * w[:, k_:k_ + 1]
    z = DN_ALPHA * x1_ref[...] + mod_ref[5:6, :] * ff
    o_ref[...] = _normalize(z, LN_EPS) * ln2w_ref[...] + ln2b_ref[...]


def _combine(dest_tiles, ybuf, w_t, x1, mod, ln2w, ln2b, seq):
    n = x1.shape[0]
    tt = COMBINE_TOKENS
    last = n // tt - 1
    return pl.pallas_call(
        _combine_body,
        grid=(n // tt,),
        in_specs=[pl.BlockSpec((TOPK * tt,), lambda i: (i,), memory_space=pltpu.SMEM),
                  pl.BlockSpec((TOPK * tt,), lambda i: (jnp.minimum(i + 1, last),), memory_space=pltpu.SMEM),
                  pl.BlockSpec(memory_space=pl.ANY),
                  pl.BlockSpec((tt, TOPK), lambda i: (i, 0)),
                  pl.BlockSpec((tt, D), lambda i: (i, 0)),
                  pl.BlockSpec((None, 6, D), lambda i: ((i * tt) // seq, 0, 0)),
                  _const_spec((1, D)), _const_spec((1, D))],
        out_specs=pl.BlockSpec((tt, D), lambda i: (i, 0)),
        out_shape=jax.ShapeDtypeStruct((n, D), f32),
        scratch_shapes=[pltpu.VMEM((2, TOPK * tt, D), f32), pltpu.SemaphoreType.DMA((2,))],
        compiler_params=pltpu.CompilerParams(dimension_semantics=("arbitrary",), vmem_limit_bytes=VMEM_LIMIT),
        name="combine",
    )(dest_tiles, dest_tiles, ybuf, w_t, x1, mod, ln2w, ln2b)


def _tile_major(a, tt):
    n = a.shape[1]
    return a.reshape(TOPK, n // tt, tt).transpose(1, 0, 2).reshape(-1)


def _rope_tables(s):
    half = HD // 2
    nf32 = np.float32
    inv_freq = np.power(nf32(ROPE_BASE), (nf32(-2.0) * np.arange(half, dtype=nf32) / nf32(HD)).astype(nf32)).astype(nf32)
    ang = (np.arange(s, dtype=nf32)[:, None] * inv_freq[None, :]).astype(nf32).astype(np.float64)
    cos, sin = np.cos(ang).astype(nf32), np.sin(ang).astype(nf32)
    return jnp.asarray(np.concatenate([cos, cos], axis=1)), jnp.asarray(np.concatenate([-sin, sin], axis=1))


def _retention_tables(theta_f, theta_b):
    lg_f = jax.nn.log_sigmoid(theta_f.astype(f32))
    lg_b = jax.nn.log_sigmoid(theta_b.astype(f32))
    idx = jnp.arange(CHUNK, dtype=f32)
    diff = idx[:, None] - idx[None, :]
    dec_f = jnp.where(diff >= 0, jnp.exp(jnp.maximum(diff, 0.0)[None] * lg_f[:, None, None]), 0.0)
    dec_b = jnp.where(diff < 0, jnp.exp(jnp.maximum(-diff, 0.0)[None] * lg_b[:, None, None]), 0.0)
    kscale = HD ** -0.5
    mtab = ((dec_f + dec_b) * kscale).transpose(1, 0, 2).reshape(CHUNK, D)
    lanes = lambda t: jnp.repeat(t.T, HD, axis=1)
    zf = lanes(jnp.exp((CHUNK - 1.0 - idx)[None, :] * lg_f[:, None]) * kscale)
    zb = lanes(jnp.exp(idx[None, :] * lg_b[:, None]) * kscale)
    xif = lanes(jnp.exp((idx + 1.0)[None, :] * lg_f[:, None]))
    xib = lanes(jnp.exp((CHUNK - idx)[None, :] * lg_b[:, None]))
    cdf = jnp.repeat(jnp.exp(CHUNK * lg_f), HD)[None, :]
    cdb = jnp.repeat(jnp.exp(CHUNK * lg_b), HD)[None, :]
    return (mtab, zf, zb, xib, cdb), (xif, cdf)


def _trunk(x, mod, p):
    b, s, _ = x.shape
    n = b * s
    p1, cr, yp, qr, kvf = _mixer_a(x, mod, p["win"], p["glnw"], p["glnb"], p["wsp"], p["bsp"],
                                   _rope_tables(s), p["tabs_a"], p["gnw"])
    xif, cdf = p["tabs_b"]
    x1, h2, logits = _mixer_b(yp, qr, kvf, p1, cr, x, mod, xif, cdf, p["wout"], p["ln1w"], p["ln1b"],
                              p["wr"], p["br"])
    idx, w, rank, cnt = _route(logits)

    counts = cnt[:, 0].astype(i32)
    padded = (counts + EXPERT_ROWS - 1) // EXPERT_ROWS * EXPERT_ROWS
    pends = jnp.cumsum(padded)
    pstarts = pends - padded
    n_blocks = n * TOPK // EXPERT_ROWS + E
    n_rows = n_blocks * EXPERT_ROWS
    block_start = jnp.arange(n_blocks, dtype=i32) * EXPERT_ROWS
    block_e = jnp.minimum(jnp.sum((pends[None, :] <= block_start[:, None]).astype(i32), axis=1), E - 1)
    dest = rank
    for e_ in range(E):
        dest = dest + jnp.where(idx == e_, pstarts[e_], 0)

    xbuf = _dispatch(h2, _tile_major(dest, DISPATCH_TOKENS), counts, pstarts.astype(i32), padded.astype(i32), n_rows)
    ybuf = _experts(block_e, xbuf, p["wgu"], p["bgu"], p["wd"], p["bd"])
    out = _combine(_tile_major(dest, COMBINE_TOKENS), ybuf, w.T, x1.reshape(n, D), mod, p["ln2w"], p["ln2b"], s)
    return out.reshape(b, s, D)


def kernel(x_prompt, x_sample, c_prompt, c_sample, w_ada, b_ada, w_in, gmlp_ln_w, gmlp_ln_b, w_spatial, b_spatial, ret_theta_fwd, ret_theta_bwd, ret_gn_w, w_out, ln1_w, ln1_b, w_router, b_router, w_gate_up, b_gate_up, w_down, b_down, ln2_w, ln2_b):
    l = 0
    bp, bs = x_prompt.shape[0], x_sample.shape[0]
    c_all = jnp.concatenate([c_prompt, c_sample], axis=0)
    c_rows = -(-c_all.shape[0] // SUBLANES) * SUBLANES
    c_all = jnp.pad(c_all, ((0, c_rows - c_all.shape[0]), (0, 0)))
    mod = _adaln(c_all, w_ada[l], b_ada[l]).reshape(c_rows, 6, D)

    tabs_a, tabs_b = _retention_tables(ret_theta_fwd[l], ret_theta_bwd[l])
    row = lambda a: a.reshape(1, -1)
    p = dict(
        win=w_in[l].astype(bf16), glnw=row(gmlp_ln_w[l]), glnb=row(gmlp_ln_b[l]),
        wsp=w_spatial[l].astype(bf16), bsp=jnp.repeat(b_spatial[l].T, HD, axis=1),
        tabs_a=tabs_a, tabs_b=tabs_b, gnw=row(ret_gn_w[l]),
        wout=w_out[l].astype(bf16), ln1w=row(ln1_w[l]), ln1b=row(ln1_b[l]),
        wr=jnp.pad(w_router[l], ((0, 0), (0, LANES - E))), br=jnp.pad(row(b_router[l]), ((0, 0), (0, LANES - E))),
        wgu=w_gate_up[l].astype(bf16), bgu=b_gate_up[l].reshape(E, 1, 2 * FF),
        wd=w_down[l].astype(bf16), bd=b_down[l].reshape(E, 1, D),
        ln2w=row(ln2_w[l]), ln2b=row(ln2_b[l]),
    )
    y_prompt = _trunk(x_prompt, mod[:bp], p)
    y_sample = _trunk(x_sample, mod[bp:bp + bs], p)
    return (y_prompt, y_sample)
```

```python
import functools

import numpy as np
import jax
import jax.numpy as jnp
from jax import lax
from jax.experimental import pallas as pl
from jax.experimental.pallas import tpu as pltpu

f32 = jnp.float32
bf16 = jnp.bfloat16
i32 = jnp.int32

D = 1024
CHUNK = 128
HEADS = 8
HD = D // HEADS
N_SEG = 8
E = 32
TOPK = 4
FF = 1024
ROPE_BASE = 10000.0
SWIGLU_LIMIT = 7.0
SWIGLU_ALPHA = 1.702
LN_EPS = 1e-5
GN_EPS = 1e-6
DEPTH = 1
DN_ALPHA = (2 * DEPTH) ** 0.25

LANES = 128
SUBLANES = 8

MIX_TOKENS = 256
SORT_TOKENS = 256
UNIT = SUBLANES
SORT_ROWS = -(-(SORT_TOKENS * TOPK + E * (UNIT - 1)) // LANES) * LANES
MAX_UNITS = SORT_ROWS // UNIT
EXPERT_ROWS = 512
VMEM_LIMIT = 56 * 1024 * 1024

_HI = lax.Precision.HIGHEST


def _const_spec(shape):
    nd = len(shape)
    return pl.BlockSpec(shape, lambda *_: (0,) * nd, pipeline_mode=pl.Buffered(1))


def _gelu(x):
    return 0.5 * x * (1.0 + lax.erf(x * (2.0 ** -0.5)))


def _normalize(x, eps):
    mu = jnp.mean(x, axis=-1, keepdims=True)
    xc = x - mu
    var = jnp.mean(xc * xc, axis=-1, keepdims=True)
    return xc * lax.rsqrt(var + eps)


def _silu(x):
    return x * jax.nn.sigmoid(x)


def _head(h):
    return slice(h * HD, (h + 1) * HD)


def _adaln_body(c_ref, w_ref, b_ref, o_ref):
    s = _silu(c_ref[...])
    o_ref[...] = jnp.dot(s, w_ref[...], preferred_element_type=f32, precision=_HI) + b_ref[...]


def _adaln(c, w_ada, b_ada):
    rows = c.shape[0]
    return pl.pallas_call(
        _adaln_body,
        grid=(6,),
        in_specs=[pl.BlockSpec((rows, D), lambda j: (0, 0)),
                  pl.BlockSpec((D, D), lambda j: (0, j)),
                  pl.BlockSpec((1, D), lambda j: (0, j))],
        out_specs=pl.BlockSpec((rows, D), lambda j: (0, j)),
        out_shape=jax.ShapeDtypeStruct((rows, 6 * D), f32),
        compiler_params=pltpu.CompilerParams(vmem_limit_bytes=VMEM_LIMIT),
        name="adaln",
    )(c, w_ada, b_ada.reshape(1, 6 * D))


def _mixer_a_body(nck, x_ref, mod_ref, win_ref, glnw_ref, glnb_ref, wsp_ref, bsp_ref,
                  cos_ref, sin_ref, mtab_ref, zf_ref, zb_ref, xib_ref, cdb_ref, gnw_ref,
                  p1_ref, cr_ref, yp_ref, qr_ref, kvf_ref, sb_ref):
    @pl.when(pl.program_id(1) == 0)
    def _():
        sb_ref[...] = jnp.zeros_like(sb_ref)

    h = (x_ref[...] * (1.0 + mod_ref[1:2, :]) + mod_ref[0:1, :]).astype(bf16)

    def seg(j):
        return jnp.dot(h, win_ref[:, j * D:(j + 1) * D], preferred_element_type=f32)

    vn = _normalize(_gelu(seg(1)), LN_EPS) * glnw_ref[...] + glnb_ref[...]
    vnb = vn.astype(bf16)
    ug = _gelu(seg(0))
    ga = jax.nn.sigmoid(seg(6))
    for c in range(nck):
        rows = slice(c * CHUNK, (c + 1) * CHUNK)
        mixed = jnp.concatenate(
            [jnp.dot(wsp_ref[g], vnb[rows, _head(g)], preferred_element_type=f32) for g in range(HEADS)], axis=1)
        p1_ref[rows, :] = ga[rows, :] * (ug[rows, :] * (mixed + bsp_ref[...]))

    q = seg(2)
    k = seg(3)
    v = seg(4).astype(bf16)
    cos, sin = cos_ref[...], sin_ref[...]

    def rotary(t):
        return jnp.concatenate([t[:, _head(h_)] * cos + pltpu.roll(t[:, _head(h_)], HD // 2, 1) * sin
                                for h_ in range(HEADS)], axis=1)

    qr = rotary(q).astype(bf16)
    kr = rotary(k)
    qr_ref[...] = qr
    for c in reversed(range(nck)):
        rows = slice(c * CHUNK, (c + 1) * CHUNK)
        kc = kr[rows, :]
        kcb = kc.astype(bf16)
        kzf = (kc * zf_ref[...]).astype(bf16)
        kzb = (kc * zb_ref[...]).astype(bf16)
        sb = sb_ref[...]
        sbb = sb.astype(bf16)
        yps, kvfs, kvbs = [], [], []
        for h_ in range(HEADS):
            hs = _head(h_)
            qh = qr[rows, hs]
            vh = v[rows, hs]
            sc = lax.dot_general(qh, kcb[:, hs], (((1,), (1,)), ((), ())), preferred_element_type=f32)
            sc = (sc * mtab_ref[:, hs]).astype(bf16)
            intra = jnp.dot(sc, vh, preferred_element_type=f32)
            cross = jnp.dot(qh, sbb[:, hs], preferred_element_type=f32) * xib_ref[:, hs]
            yps.append(intra + cross)
            kvfs.append(lax.dot_general(kzf[:, hs], vh, (((0,), (0,)), ((), ())), preferred_element_type=f32))
            kvbs.append(lax.dot_general(kzb[:, hs], vh, (((0,), (0,)), ((), ())), preferred_element_type=f32))
        yp_ref[rows, :] = jnp.concatenate(yps, axis=1)
        kvf_ref[c] = jnp.concatenate(kvfs, axis=1)
        sb_ref[...] = sb * cdb_ref[...] + jnp.concatenate(kvbs, axis=1)

    cr_ref[...] = jax.nn.sigmoid(seg(7)) * _silu(seg(5)) * gnw_ref[...]


def _mixer_a(x, mod, win, glnw, glnb, wsp, bsp, rope, tabs, gnw):
    b, s, _ = x.shape
    t = MIX_TOKENS
    nt, nck = s // t, t // CHUNK
    cos, sin = rope
    mtab, zf, zb, xib, cdb = tabs
    rev = lambda bi, i: (bi, nt - 1 - i, 0)
    rope_spec = pl.BlockSpec((t, HD), lambda bi, i: (nt - 1 - i, 0))
    tok = lambda dt: jax.ShapeDtypeStruct((b, s, D), dt)
    return pl.pallas_call(
        functools.partial(_mixer_a_body, nck),
        grid=(b, nt),
        in_specs=[pl.BlockSpec((None, t, D), rev),
                  pl.BlockSpec((None, 6, D), lambda bi, i: (bi, 0, 0)),
                  _const_spec((D, N_SEG * D)),
                  _const_spec((1, D)), _const_spec((1, D)),
                  _const_spec((HEADS, CHUNK, CHUNK)), _const_spec((CHUNK, D)),
                  rope_spec, rope_spec,
                  _const_spec((CHUNK, D)), _const_spec((CHUNK, D)), _const_spec((CHUNK, D)),
                  _const_spec((CHUNK, D)), _const_spec((1, D)), _const_spec((1, D))],
        out_specs=[pl.BlockSpec((None, t, D), rev)] * 4
                  + [pl.BlockSpec((None, nck, CHUNK, D), lambda bi, i: (bi, nt - 1 - i, 0, 0))],
        out_shape=[tok(f32), tok(f32), tok(f32), tok(bf16),
                   jax.ShapeDtypeStruct((b, s // CHUNK, CHUNK, D), f32)],
        scratch_shapes=[pltpu.VMEM((CHUNK, D), f32)],
        compiler_params=pltpu.CompilerParams(dimension_semantics=("arbitrary", "arbitrary"),
                                             vmem_limit_bytes=VMEM_LIMIT),
        name="mixer_a",
    )(x, mod, win, glnw, glnb, wsp, bsp, cos, sin, mtab, zf, zb, xib, cdb, gnw)


def _mixer_b_body(nck, yp_ref, qr_ref, kvf_ref, p1_ref, cr_ref, x_ref, mod_ref, xif_ref, cdf_ref,
                  wout_ref, ln1w_ref, ln1b_ref, wr_ref, br_ref,
                  x1_ref, h2_ref, lg_ref, sf_ref):
    @pl.when(pl.program_id(1) == 0)
    def _():
        sf_ref[...] = jnp.zeros_like(sf_ref)

    ys = []
    for c in range(nck):
        rows = slice(c * CHUNK, (c + 1) * CHUNK)
        sf = sf_ref[...]
        sfb = sf.astype(bf16)
        cross = jnp.concatenate(
            [jnp.dot(qr_ref[rows, _head(h_)], sfb[:, _head(h_)], preferred_element_type=f32)
             for h_ in range(HEADS)], axis=1)
        ys.append(yp_ref[rows, :] + cross * xif_ref[...])
        sf_ref[...] = sf * cdf_ref[...] + kvf_ref[c]
    y = jnp.concatenate(ys, axis=0)
    yn = jnp.concatenate([_normalize(y[:, _head(h_)], GN_EPS) for h_ in range(HEADS)], axis=1)
    merged = p1_ref[...] + cr_ref[...] * yn
    mix = jnp.dot(merged.astype(bf16), wout_ref[...], preferred_element_type=f32)
    x1 = _normalize(DN_ALPHA * x_ref[...] + mod_ref[2:3, :] * mix, LN_EPS) * ln1w_ref[...] + ln1b_ref[...]
    x1_ref[...] = x1
    h2 = x1 * (1.0 + mod_ref[4:5, :]) + mod_ref[3:4, :]
    h2_ref[...] = h2
    h2_hi = h2.astype(bf16)
    h2_lo = (h2 - h2_hi.astype(f32)).astype(bf16)
    hi_terms = jnp.dot(h2_hi, wr_ref[...], preferred_element_type=f32)
    lo_term = jnp.dot(h2_lo, wr_ref[:, :LANES], preferred_element_type=f32)
    lg_ref[...] = hi_terms[:, :LANES] + hi_terms[:, LANES:] + lo_term + br_ref[...]


def _mixer_b(yp, qr, kvf, p1, cr, x, mod, xif, cdf, wout, ln1w, ln1b, wr, br):
    b, s, _ = x.shape
    t = MIX_TOKENS
    nt, nck = s // t, t // CHUNK
    fwd = lambda bi, i: (bi, i, 0)
    flat = lambda bi, i: (bi * nt + i, 0)
    return pl.pallas_call(
        functools.partial(_mixer_b_body, nck),
        grid=(b, nt),
        in_specs=[pl.BlockSpec((None, t, D), fwd), pl.BlockSpec((None, t, D), fwd),
                  pl.BlockSpec((None, nck, CHUNK, D), lambda bi, i: (bi, i, 0, 0)),
                  pl.BlockSpec((None, t, D), fwd), pl.BlockSpec((None, t, D), fwd),
                  pl.BlockSpec((None, t, D), fwd),
                  pl.BlockSpec((None, 6, D), lambda bi, i: (bi, 0, 0)),
                  _const_spec((CHUNK, D)), _const_spec((1, D)),
                  _const_spec((D, D)), _const_spec((1, D)), _const_spec((1, D)),
                  _const_spec((D, 2 * LANES)), _const_spec((1, LANES))],
        out_specs=[pl.BlockSpec((None, t, D), fwd),
                   pl.BlockSpec((t, D), flat),
                   pl.BlockSpec((t, LANES), flat)],
        out_shape=[jax.ShapeDtypeStruct((b, s, D), f32),
                   jax.ShapeDtypeStruct((b * s, D), f32),
                   jax.ShapeDtypeStruct((b * s, LANES), f32)],
        scratch_shapes=[pltpu.VMEM((CHUNK, D), f32)],
        compiler_params=pltpu.CompilerParams(dimension_semantics=("arbitrary", "arbitrary"),
                                             vmem_limit_bytes=VMEM_LIMIT),
        name="mixer_b",
    )(yp, qr, kvf, p1, cr, x, mod, xif, cdf, wout, ln1w, ln1b, wr, br)


def _route_body(lg_ref, idx_ref, w_ref, rank_ref, cnt_ref):
    tt = lg_ref.shape[0]
    l = lg_ref[...].T[:E, :]
    iota_e = lax.broadcasted_iota(i32, (E, tt), 0)
    vals, idxs, hots = [], [], []
    for _ in range(TOPK):
        m = jnp.max(l, axis=0, keepdims=True)
        idx = jnp.min(jnp.where(l == m, iota_e, E), axis=0, keepdims=True)
        hot = iota_e == idx
        l = jnp.where(hot, -jnp.inf, l)
        vals.append(m)
        idxs.append(idx)
        hots.append(hot)
    exps = [jnp.exp(v_ - vals[0]) for v_ in vals]
    tot = exps[0] + exps[1] + exps[2] + exps[3]
    w_ref[...] = jnp.concatenate([e_ / tot for e_ in exps], axis=0)
    idx_ref[...] = jnp.concatenate(idxs, axis=0)

    sel = jnp.zeros((E, tt), f32)
    for hot in hots:
        sel = sel + hot.astype(f32)
    upper = (lax.broadcasted_iota(i32, (tt, tt), 0) < lax.broadcasted_iota(i32, (tt, tt), 1)).astype(bf16)
    rank_e = jnp.dot(sel.astype(bf16), upper, preferred_element_type=f32)
    ranks = [jnp.sum(jnp.where(hot, rank_e, 0.0), axis=0, keepdims=True) for hot in hots]
    rank_ref[...] = jnp.concatenate(ranks, axis=0).astype(i32)
    cnt_ref[...] = jnp.broadcast_to(jnp.sum(sel, axis=1, keepdims=True), cnt_ref.shape)


def _route(logits):
    n = logits.shape[0]
    tt = SORT_TOKENS
    col = lambda i: (0, i)
    return pl.pallas_call(
        _route_body,
        grid=(n // tt,),
        in_specs=[pl.BlockSpec((tt, LANES), lambda i: (i, 0))],
        out_specs=[pl.BlockSpec((TOPK, tt), col), pl.BlockSpec((TOPK, tt), col), pl.BlockSpec((TOPK, tt), col),
                   pl.BlockSpec((None, E, LANES), lambda i: (i, 0, 0))],
        out_shape=[jax.ShapeDtypeStruct((TOPK, n), i32), jax.ShapeDtypeStruct((TOPK, n), f32),
                   jax.ShapeDtypeStruct((TOPK, n), i32), jax.ShapeDtypeStruct((n // tt, E, LANES), f32)],
        compiler_params=pltpu.CompilerParams(dimension_semantics=("arbitrary",), vmem_limit_bytes=VMEM_LIMIT),
        name="route",
    )(logits)


def _unit_copy(src, dst, sem):
    return pltpu.make_async_copy(src, dst, sem)


def _start_units(n_units, start):
    def even(t, carry):
        start(2 * t, 0)
        return carry

    def odd(t, carry):
        start(2 * t + 1, 1)
        return carry

    lax.fori_loop(0, (n_units + 1) // 2, even, 0)
    lax.fori_loop(0, n_units // 2, odd, 0)


def _wait_units(n_units, wait_rows):
    for bit in reversed(range(MAX_UNITS.bit_length())):
        @pl.when((n_units >> bit) & 1 == 1)
        def _():
            wait_rows((1 << bit) * UNIT)


def _dispatch_body(srow_ref, tunits_ref, zstart_ref, zunits_ref,
                   lpos_ref, h2_ref, xbuf_ref, xs_ref, zero_ref, sem, zsem):
    tt = h2_ref.shape[0]
    i = pl.program_id(0)
    last = pl.num_programs(0) - 1
    slot = i % 2

    def start(j, priority):
        lrow = pl.multiple_of(j * UNIT, UNIT)
        srow = pl.multiple_of(srow_ref[i * MAX_UNITS + j], UNIT)
        _unit_copy(xs_ref.at[slot, pl.ds(lrow, UNIT)], xbuf_ref.at[pl.ds(srow, UNIT)],
                   sem.at[slot]).start(priority=priority)

    def drain(tile, slot_):
        _wait_units(tunits_ref[tile], lambda r: _unit_copy(
            xs_ref.at[slot_, pl.ds(0, r)], xbuf_ref.at[pl.ds(0, r)], sem.at[slot_]).wait())

    @pl.when(i >= 2)
    def _():
        drain(i - 2, slot)

    lpos = lpos_ref[...]
    rows = lax.broadcasted_iota(i32, (SORT_ROWS, tt), 0)
    perm = jnp.zeros((SORT_ROWS, tt), f32)
    for k_ in range(TOPK):
        perm = jnp.where(rows == lpos[k_:k_ + 1, :], 1.0, perm)
    xs_ref[slot] = jnp.dot(perm.astype(bf16), h2_ref[...].astype(bf16), preferred_element_type=f32)
    _start_units(tunits_ref[i], start)

    @pl.when(i == last)
    def _():
        zero_ref[...] = jnp.zeros_like(zero_ref)

        def zero_copy(srow):
            return _unit_copy(zero_ref, xbuf_ref.at[pl.ds(srow, UNIT)], zsem)

        def fill(e_, carry):
            def start_one(u, c):
                zero_copy(pl.multiple_of(zstart_ref[e_] + u * UNIT, UNIT)).start()
                return c

            lax.fori_loop(0, zunits_ref[e_], start_one, 0)
            return carry

        def fill_wait(e_, carry):
            def wait_one(u, c):
                zero_copy(0).wait()
                return c

            lax.fori_loop(0, zunits_ref[e_], wait_one, 0)
            return carry

        lax.fori_loop(0, E + 1, fill, 0)
        lax.fori_loop(0, E + 1, fill_wait, 0)
        drain(i, slot)

        @pl.when(i >= 1)
        def _():
            drain(i - 1, 1 - slot)


def _dispatch(h2, lpos, srow, tunits, zstart, zunits, n_rows):
    n = h2.shape[0]
    tt = SORT_TOKENS
    return pl.pallas_call(
        _dispatch_body,
        grid_spec=pltpu.PrefetchScalarGridSpec(
            num_scalar_prefetch=4,
            grid=(n // tt,),
            in_specs=[pl.BlockSpec((TOPK, tt), lambda i, *_: (0, i)),
                      pl.BlockSpec((tt, D), lambda i, *_: (i, 0))],
            out_specs=pl.BlockSpec(memory_space=pl.ANY),
            scratch_shapes=[pltpu.VMEM((2, SORT_ROWS, D), f32), pltpu.VMEM((UNIT, D), f32),
                            pltpu.SemaphoreType.DMA((2,)), pltpu.SemaphoreType.DMA(())]),
        out_shape=jax.ShapeDtypeStruct((n_rows, D), f32),
        compiler_params=pltpu.CompilerParams(dimension_semantics=("arbitrary",), vmem_limit_bytes=VMEM_LIMIT),
        name="dispatch",
    )(srow, tunits, zstart, zunits, lpos, h2)


def _expert_body(be_ref, nused_ref, x_ref, wgu_ref, bgu_ref, wd_ref, bd_ref, y_ref):
    @pl.when(pl.program_id(0) >= nused_ref[0])
    def _():
        y_ref[...] = jnp.zeros_like(y_ref)

    @pl.when(pl.program_id(0) < nused_ref[0])
    def _():
        gu = jnp.dot(x_ref[...].astype(bf16), wgu_ref[...], preferred_element_type=f32) + bgu_ref[...]
        gate = jnp.minimum(gu[:, :FF], SWIGLU_LIMIT)
        up = jnp.clip(gu[:, FF:], -SWIGLU_LIMIT, SWIGLU_LIMIT)
        act = (up + 1.0) * gate * jax.nn.sigmoid(SWIGLU_ALPHA * gate)
        y_ref[...] = jnp.dot(act.astype(bf16), wd_ref[...], preferred_element_type=f32) + bd_ref[...]


def _experts(block_e, n_used, xbuf, wgu, bgu, wd, bd):
    n_rows = xbuf.shape[0]
    r = EXPERT_ROWS
    blk = lambda b_, nu: jnp.minimum(b_, nu[0] - 1)
    wspec = lambda shape: pl.BlockSpec((None,) + shape, lambda b_, be, nu: (be[blk(b_, nu)], 0, 0))
    return pl.pallas_call(
        _expert_body,
        grid_spec=pltpu.PrefetchScalarGridSpec(
            num_scalar_prefetch=2,
            grid=(n_rows // r,),
            in_specs=[pl.BlockSpec((r, D), lambda b_, be, nu: (blk(b_, nu), 0)),
                      wspec((D, 2 * FF)), wspec((1, 2 * FF)), wspec((FF, D)), wspec((1, D))],
            out_specs=pl.BlockSpec((r, D), lambda b_, be, nu: (b_, 0))),
        out_shape=jax.ShapeDtypeStruct((n_rows, D), f32),
        compiler_params=pltpu.CompilerParams(dimension_semantics=("arbitrary",), vmem_limit_bytes=VMEM_LIMIT),
        name="experts",
    )(block_e, n_used, xbuf, wgu, bgu, wd, bd)


def _combine_body(srow_ref, tunits_ref,
                  lpos_ref, w_ref, x1_ref, mod_ref, ln2w_ref, ln2b_ref, ybuf_ref, o_ref, ys_ref, sem):
    tt = x1_ref.shape[0]
    i = pl.program_id(0)
    slot = i % 2

    def fetch(tile, slot_):
        def start(j, priority):
            lrow = pl.multiple_of(j * UNIT, UNIT)
            srow = pl.multiple_of(srow_ref[tile * MAX_UNITS + j], UNIT)
            _unit_copy(ybuf_ref.at[pl.ds(srow, UNIT)], ys_ref.at[slot_, pl.ds(lrow, UNIT)],
                       sem.at[slot_]).start(priority=priority)

        _start_units(tunits_ref[tile], start)

    @pl.when(i == 0)
    def _():
        ys_ref[...] = jnp.zeros_like(ys_ref)
        fetch(0, 0)

    @pl.when(i + 1 < pl.num_programs(0))
    def _():
        fetch(i + 1, 1 - slot)

    _wait_units(tunits_ref[i], lambda r: _unit_copy(
        ybuf_ref.at[pl.ds(0, r)], ys_ref.at[slot, pl.ds(0, r)], sem.at[slot]).wait())

    lpos, w = lpos_ref[...], w_ref[...]
    cols = lax.broadcasted_iota(i32, (tt, SORT_ROWS), 1)
    wm = jnp.zeros((tt, SORT_ROWS), f32)
    for k_ in range(TOPK):
        wm = jnp.where(cols == lpos[:, k_:k_ + 1], w[:, k_:k_ + 1], wm)
    ff = jnp.dot(wm.astype(bf16), ys_ref[slot].astype(bf16), preferred_element_type=f32)
    z = DN_ALPHA * x1_ref[...] + mod_ref[5:6, :] * ff
    o_ref[...] = _normalize(z, LN_EPS) * ln2w_ref[...] + ln2b_ref[...]


def _combine(srow, tunits, lpos_t, w_t, x1, mod, ln2w, ln2b, ybuf, seq):
    n = x1.shape[0]
    tt = SORT_TOKENS
    tok = lambda width: pl.BlockSpec((tt, width), lambda i, *_: (i, 0))
    return pl.pallas_call(
        _combine_body,
        grid_spec=pltpu.PrefetchScalarGridSpec(
            num_scalar_prefetch=2,
            grid=(n // tt,),
            in_specs=[tok(TOPK), tok(TOPK), tok(D),
                      pl.BlockSpec((None, 6, D), lambda i, *_: ((i * tt) // seq, 0, 0)),
                      _const_spec((1, D)), _const_spec((1, D)),
                      pl.BlockSpec(memory_space=pl.ANY)],
            out_specs=tok(D),
            scratch_shapes=[pltpu.VMEM((2, SORT_ROWS, D), f32), pltpu.SemaphoreType.DMA((2,))]),
        out_shape=jax.ShapeDtypeStruct((n, D), f32),
        compiler_params=pltpu.CompilerParams(dimension_semantics=("arbitrary",), vmem_limit_bytes=VMEM_LIMIT),
        name="combine",
    )(srow, tunits, lpos_t, w_t, x1, mod, ln2w, ln2b, ybuf)


def _rope_tables(s):
    half = HD // 2
    nf32 = np.float32
    inv_freq = np.power(nf32(ROPE_BASE), (nf32(-2.0) * np.arange(half, dtype=nf32) / nf32(HD)).astype(nf32)).astype(nf32)
    ang = (np.arange(s, dtype=nf32)[:, None] * inv_freq[None, :]).astype(nf32).astype(np.float64)
    cos, sin = np.cos(ang).astype(nf32), np.sin(ang).astype(nf32)
    return jnp.asarray(np.concatenate([cos, cos], axis=1)), jnp.asarray(np.concatenate([-sin, sin], axis=1))


def _retention_tables(theta_f, theta_b):
    lg_f = jax.nn.log_sigmoid(theta_f.astype(f32))
    lg_b = jax.nn.log_sigmoid(theta_b.astype(f32))
    idx = jnp.arange(CHUNK, dtype=f32)
    diff = idx[:, None] - idx[None, :]
    dec_f = jnp.where(diff >= 0, jnp.exp(jnp.maximum(diff, 0.0)[None] * lg_f[:, None, None]), 0.0)
    dec_b = jnp.where(diff < 0, jnp.exp(jnp.maximum(-diff, 0.0)[None] * lg_b[:, None, None]), 0.0)
    kscale = HD ** -0.5
    mtab = ((dec_f + dec_b) * kscale).transpose(1, 0, 2).reshape(CHUNK, D)
    lanes = lambda t: jnp.repeat(t.T, HD, axis=1)
    zf = lanes(jnp.exp((CHUNK - 1.0 - idx)[None, :] * lg_f[:, None]) * kscale)
    zb = lanes(jnp.exp(idx[None, :] * lg_b[:, None]) * kscale)
    xif = lanes(jnp.exp((idx + 1.0)[None, :] * lg_f[:, None]))
    xib = lanes(jnp.exp((CHUNK - idx)[None, :] * lg_b[:, None]))
    cdf = jnp.repeat(jnp.exp(CHUNK * lg_f), HD)[None, :]
    cdb = jnp.repeat(jnp.exp(CHUNK * lg_b), HD)[None, :]
    return (mtab, zf, zb, xib, cdb), (xif, cdf)


def _hi_lo(a):
    hi = a.astype(bf16)
    return jnp.concatenate([hi, (a - hi.astype(f32)).astype(bf16)], axis=1)


def _trunk(x, mod, p):
    b, s, _ = x.shape
    n = b * s
    p1, cr, yp, qr, kvf = _mixer_a(x, mod, p["win"], p["glnw"], p["glnb"], p["wsp"], p["bsp"],
                                   _rope_tables(s), p["tabs_a"], p["gnw"])
    xif, cdf = p["tabs_b"]
    x1, h2, logits = _mixer_b(yp, qr, kvf, p1, cr, x, mod, xif, cdf, p["wout"], p["ln1w"], p["ln1b"],
                              p["wr"], p["br"])
    idx, w, lrank, cnt = _route(logits)

    tt = SORT_TOKENS
    nt = n // tt
    counts = cnt[:, :, 0].astype(i32)
    run = (counts + UNIT - 1) // UNIT * UNIT
    loff = jnp.cumsum(run, axis=1) - run
    total = jnp.sum(run, axis=0)
    padded = (total + EXPERT_ROWS - 1) // EXPERT_ROWS * EXPERT_ROWS
    pends = jnp.cumsum(padded)
    pstarts = pends - padded
    seg = pstarts[None, :] + jnp.cumsum(run, axis=0) - run
    n_blocks = (n * TOPK + nt * E * (UNIT - 1)) // EXPERT_ROWS + E
    n_used = (pends[E - 1:] // EXPERT_ROWS).astype(i32)
    block_start = jnp.arange(n_blocks, dtype=i32) * EXPERT_ROWS
    block_e = jnp.minimum(jnp.sum((pends[None, :] <= block_start[:, None]).astype(i32), axis=1), E - 1)
    lpos = lrank.reshape(TOPK, nt, tt)
    idx3 = idx.reshape(TOPK, nt, tt)
    for e_ in range(E):
        lpos = lpos + jnp.where(idx3 == e_, loff[None, :, e_, None], 0)
    lpos = lpos.reshape(TOPK, n)
    ustart = jnp.arange(MAX_UNITS, dtype=i32) * UNIT
    run_of = jnp.sum(((loff + run)[:, None, :] <= ustart[None, :, None]).astype(i32), axis=2)
    srow = jnp.broadcast_to(ustart[None, :], (nt, MAX_UNITS))
    for e_ in range(E):
        srow = srow + jnp.where(run_of == e_, (seg - loff)[:, e_, None], 0)
    flat = lambda a: a.reshape(-1).astype(i32)
    tunits = flat(jnp.sum(run, axis=1) // UNIT)

    n_rows = n_blocks * EXPERT_ROWS
    zstart = jnp.concatenate([pstarts + total, pends[E - 1:]])
    zunits = jnp.concatenate([padded - total, n_rows - pends[E - 1:]]) // UNIT
    xbuf = _dispatch(h2, lpos, flat(srow), tunits, flat(zstart), flat(zunits), n_rows)
    ybuf = _experts(block_e.astype(i32), n_used, xbuf, p["wgu"], p["bgu"], p["wd"], p["bd"])
    out = _combine(flat(srow), tunits, lpos.T, w.T, x1.reshape(n, D), mod, p["ln2w"], p["ln2b"], ybuf, s)
    return out.reshape(b, s, D)


def kernel(x_prompt, x_sample, c_prompt, c_sample, w_ada, b_ada, w_in, gmlp_ln_w, gmlp_ln_b, w_spatial, b_spatial, ret_theta_fwd, ret_theta_bwd, ret_gn_w, w_out, ln1_w, ln1_b, w_router, b_router, w_gate_up, b_gate_up, w_down, b_down, ln2_w, ln2_b):
    l = 0
    bp, bs = x_prompt.shape[0], x_sample.shape[0]
    c_all = jnp.concatenate([c_prompt, c_sample], axis=0)
    c_rows = -(-c_all.shape[0] // SUBLANES) * SUBLANES
    c_all = jnp.pad(c_all, ((0, c_rows - c_all.shape[0]), (0, 0)))
    mod = _adaln(c_all, w_ada[l], b_ada[l]).reshape(c_rows, 6, D)

    tabs_a, tabs_b = _retention_tables(ret_theta_fwd[l], ret_theta_bwd[l])
    row = lambda a: a.reshape(1, -1)
    p = dict(
        win=w_in[l].astype(bf16), glnw=row(gmlp_ln_w[l]), glnb=row(gmlp_ln_b[l]),
        wsp=w_spatial[l].astype(bf16), bsp=jnp.repeat(b_spatial[l].T, HD, axis=1),
        tabs_a=tabs_a, tabs_b=tabs_b, gnw=row(ret_gn_w[l]),
        wout=w_out[l].astype(bf16), ln1w=row(ln1_w[l]), ln1b=row(ln1_b[l]),
        wr=_hi_lo(jnp.pad(w_router[l], ((0, 0), (0, LANES - E)))), br=jnp.pad(row(b_router[l]), ((0, 0), (0, LANES - E))),
        wgu=w_gate_up[l].astype(bf16), bgu=b_gate_up[l].reshape(E, 1, 2 * FF),
        wd=w_down[l].astype(bf16), bd=b_down[l].reshape(E, 1, D),
        ln2w=row(ln2_w[l]), ln2b=row(ln2_b[l]),
    )
    y_prompt = _trunk(x_prompt, mod[:bp], p)
    y_sample = _trunk(x_sample, mod[bp:bp + bs], p)
    return (y_prompt, y_sample)
```

```python
import functools

import numpy as np
import jax
import jax.numpy as jnp
from jax import lax
from jax.experimental import pallas as pl
from jax.experimental.pallas import tpu as pltpu

f32 = jnp.float32
bf16 = jnp.bfloat16
i32 = jnp.int32

D = 1024
CHUNK = 128
HEADS = 8
HD = D // HEADS
N_SEG = 8
E = 32
TOPK = 4
FF = 1024
ROPE_BASE = 10000.0
SWIGLU_LIMIT = 7.0
SWIGLU_ALPHA = 1.702
LN_EPS = 1e-5
GN_EPS = 1e-6
DEPTH = 1
DN_ALPHA = (2 * DEPTH) ** 0.25

LANES = 128
SUBLANES = 8

MIX_TOKENS = 256
SORT_TOKENS = 256
UNIT = SUBLANES
SORT_ROWS = -(-(SORT_TOKENS * TOPK + E * (UNIT - 1)) // LANES) * LANES
MAX_UNITS = SORT_ROWS // UNIT
ROUTE_TILES = 4
EXPERT_ROWS = 512
VMEM_LIMIT = 56 * 1024 * 1024

_HI = lax.Precision.HIGHEST


def _const_spec(shape):
    nd = len(shape)
    return pl.BlockSpec(shape, lambda *_: (0,) * nd, pipeline_mode=pl.Buffered(1))


def _gelu(x):
    return 0.5 * x * (1.0 + lax.erf(x * (2.0 ** -0.5)))


def _normalize(x, eps):
    mu = jnp.mean(x, axis=-1, keepdims=True)
    xc = x - mu
    var = jnp.mean(xc * xc, axis=-1, keepdims=True)
    return xc * lax.rsqrt(var + eps)


def _silu(x):
    return x * jax.nn.sigmoid(x)


def _head(h):
    return slice(h * HD, (h + 1) * HD)


def _adaln_body(c_ref, w_ref, b_ref, o_ref):
    s = _silu(c_ref[...])
    o_ref[...] = jnp.dot(s, w_ref[...], preferred_element_type=f32, precision=_HI) + b_ref[...]


def _adaln(c, w_ada, b_ada):
    rows = c.shape[0]
    return pl.pallas_call(
        _adaln_body,
        grid=(6,),
        in_specs=[pl.BlockSpec((rows, D), lambda j: (0, 0)),
                  pl.BlockSpec((D, D), lambda j: (0, j)),
                  pl.BlockSpec((1, D), lambda j: (0, j))],
        out_specs=pl.BlockSpec((rows, D), lambda j: (0, j)),
        out_shape=jax.ShapeDtypeStruct((rows, 6 * D), f32),
        compiler_params=pltpu.CompilerParams(vmem_limit_bytes=VMEM_LIMIT),
        name="adaln",
    )(c, w_ada, b_ada.reshape(1, 6 * D))


def _mixer_a_body(nck, x_ref, mod_ref, win_ref, glnw_ref, glnb_ref, wsp_ref, bsp_ref,
                  cos_ref, sin_ref, mtab_ref, zf_ref, zb_ref, xib_ref, cdb_ref, gnw_ref,
                  p1_ref, cr_ref, yp_ref, qr_ref, kvf_ref, sb_ref):
    @pl.when(pl.program_id(1) == 0)
    def _():
        sb_ref[...] = jnp.zeros_like(sb_ref)

    h = (x_ref[...] * (1.0 + mod_ref[1:2, :]) + mod_ref[0:1, :]).astype(bf16)

    def seg(j):
        return jnp.dot(h, win_ref[:, j * D:(j + 1) * D], preferred_element_type=f32)

    vn = _normalize(_gelu(seg(1)), LN_EPS) * glnw_ref[...] + glnb_ref[...]
    vnb = vn.astype(bf16)
    ug = _gelu(seg(0))
    ga = jax.nn.sigmoid(seg(6))
    for c in range(nck):
        rows = slice(c * CHUNK, (c + 1) * CHUNK)
        mixed = jnp.concatenate(
            [jnp.dot(wsp_ref[g], vnb[rows, _head(g)], preferred_element_type=f32) for g in range(HEADS)], axis=1)
        p1_ref[rows, :] = (ga[rows, :] * (ug[rows, :] * (mixed + bsp_ref[...]))).astype(bf16)

    q = seg(2)
    k = seg(3)
    v = seg(4).astype(bf16)
    cos, sin = cos_ref[...], sin_ref[...]

    def rotary(t):
        return jnp.concatenate([t[:, _head(h_)] * cos + pltpu.roll(t[:, _head(h_)], HD // 2, 1) * sin
                                for h_ in range(HEADS)], axis=1)

    qf = rotary(q)
    kr = rotary(k)
    qr = qf.astype(bf16)
    qr_ref[...] = qr
    for c in reversed(range(nck)):
        rows = slice(c * CHUNK, (c + 1) * CHUNK)
        kc = kr[rows, :]
        kcb = kc.astype(bf16)
        kzf = (kc * zf_ref[...]).astype(bf16)
        kzb = (kc * zb_ref[...]).astype(bf16)
        qx = (qf[rows, :] * xib_ref[...]).astype(bf16)
        sb = sb_ref[...]
        sbb = sb.astype(bf16)
        yps, kvfs, kvbs = [], [], []
        for h_ in range(HEADS):
            hs = _head(h_)
            vh = v[rows, hs]
            sc = lax.dot_general(qr[rows, hs], kcb[:, hs], (((1,), (1,)), ((), ())), preferred_element_type=f32)
            sc = (sc * mtab_ref[:, hs]).astype(bf16)
            yps.append(jnp.dot(sc, vh, preferred_element_type=f32)
                       + jnp.dot(qx[:, hs], sbb[:, hs], preferred_element_type=f32))
            kvfs.append(lax.dot_general(kzf[:, hs], vh, (((0,), (0,)), ((), ())), preferred_element_type=f32))
            kvbs.append(lax.dot_general(kzb[:, hs], vh, (((0,), (0,)), ((), ())), preferred_element_type=f32))
        yp_ref[rows, :] = jnp.concatenate(yps, axis=1).astype(bf16)
        kvf_ref[c] = jnp.concatenate(kvfs, axis=1)
        sb_ref[...] = sb * cdb_ref[...] + jnp.concatenate(kvbs, axis=1)

    cr_ref[...] = (jax.nn.sigmoid(seg(7)) * _silu(seg(5)) * gnw_ref[...]).astype(bf16)


def _mixer_a(x, mod, win, glnw, glnb, wsp, bsp, rope, tabs, gnw):
    b, s, _ = x.shape
    t = MIX_TOKENS
    nt, nck = s // t, t // CHUNK
    cos, sin = rope
    mtab, zf, zb, xib, cdb = tabs
    rev = lambda bi, i: (bi, nt - 1 - i, 0)
    rope_spec = pl.BlockSpec((t, HD), lambda bi, i: (nt - 1 - i, 0))
    tok = lambda dt: jax.ShapeDtypeStruct((b, s, D), dt)
    return pl.pallas_call(
        functools.partial(_mixer_a_body, nck),
        grid=(b, nt),
        in_specs=[pl.BlockSpec((None, t, D), rev),
                  pl.BlockSpec((None, 6, D), lambda bi, i: (bi, 0, 0)),
                  _const_spec((D, N_SEG * D)),
                  _const_spec((1, D)), _const_spec((1, D)),
                  _const_spec((HEADS, CHUNK, CHUNK)), _const_spec((CHUNK, D)),
                  rope_spec, rope_spec,
                  _const_spec((CHUNK, D)), _const_spec((CHUNK, D)), _const_spec((CHUNK, D)),
                  _const_spec((CHUNK, D)), _const_spec((1, D)), _const_spec((1, D))],
        out_specs=[pl.BlockSpec((None, t, D), rev)] * 4
                  + [pl.BlockSpec((None, nck, CHUNK, D), lambda bi, i: (bi, nt - 1 - i, 0, 0))],
        out_shape=[tok(bf16), tok(bf16), tok(bf16), tok(bf16),
                   jax.ShapeDtypeStruct((b, s // CHUNK, CHUNK, D), f32)],
        scratch_shapes=[pltpu.VMEM((CHUNK, D), f32)],
        compiler_params=pltpu.CompilerParams(dimension_semantics=("arbitrary", "arbitrary"),
                                             vmem_limit_bytes=VMEM_LIMIT),
        name="mixer_a",
    )(x, mod, win, glnw, glnb, wsp, bsp, cos, sin, mtab, zf, zb, xib, cdb, gnw)


def _mixer_b_body(nck, yp_ref, qr_ref, kvf_ref, p1_ref, cr_ref, x_ref, mod_ref, xif_ref, cdf_ref,
                  wout_ref, ln1w_ref, ln1b_ref, wr_ref, br_ref,
                  x1_ref, h2_ref, lg_ref, sf_ref):
    @pl.when(pl.program_id(1) == 0)
    def _():
        sf_ref[...] = jnp.zeros_like(sf_ref)

    ys = []
    for c in range(nck):
        rows = slice(c * CHUNK, (c + 1) * CHUNK)
        sf = sf_ref[...]
        sfb = sf.astype(bf16)
        cross = jnp.concatenate(
            [jnp.dot(qr_ref[rows, _head(h_)], sfb[:, _head(h_)], preferred_element_type=f32)
             for h_ in range(HEADS)], axis=1)
        ys.append(yp_ref[rows, :].astype(f32) + cross * xif_ref[...])
        sf_ref[...] = sf * cdf_ref[...] + kvf_ref[c]
    y = jnp.concatenate(ys, axis=0)
    yn = jnp.concatenate([_normalize(y[:, _head(h_)], GN_EPS) for h_ in range(HEADS)], axis=1)
    merged = p1_ref[...].astype(f32) + cr_ref[...].astype(f32) * yn
    mix = jnp.dot(merged.astype(bf16), wout_ref[...], preferred_element_type=f32)
    x1 = _normalize(DN_ALPHA * x_ref[...] + mod_ref[2:3, :] * mix, LN_EPS) * ln1w_ref[...] + ln1b_ref[...]
    x1_ref[...] = x1
    h2 = x1 * (1.0 + mod_ref[4:5, :]) + mod_ref[3:4, :]
    h2_hi = h2.astype(bf16)
    h2_ref[...] = h2_hi
    h2_lo = (h2 - h2_hi.astype(f32)).astype(bf16)
    hi_terms = jnp.dot(h2_hi, wr_ref[...], preferred_element_type=f32)
    lo_term = jnp.dot(h2_lo, wr_ref[:, :LANES], preferred_element_type=f32)
    lg_ref[...] = hi_terms[:, :LANES] + hi_terms[:, LANES:] + lo_term + br_ref[...]


def _mixer_b(yp, qr, kvf, p1, cr, x, mod, xif, cdf, wout, ln1w, ln1b, wr, br):
    b, s, _ = x.shape
    t = MIX_TOKENS
    nt, nck = s // t, t // CHUNK
    fwd = lambda bi, i: (bi, i, 0)
    flat = lambda bi, i: (bi * nt + i, 0)
    return pl.pallas_call(
        functools.partial(_mixer_b_body, nck),
        grid=(b, nt),
        in_specs=[pl.BlockSpec((None, t, D), fwd), pl.BlockSpec((None, t, D), fwd),
                  pl.BlockSpec((None, nck, CHUNK, D), lambda bi, i: (bi, i, 0, 0)),
                  pl.BlockSpec((None, t, D), fwd), pl.BlockSpec((None, t, D), fwd),
                  pl.BlockSpec((None, t, D), fwd),
                  pl.BlockSpec((None, 6, D), lambda bi, i: (bi, 0, 0)),
                  _const_spec((CHUNK, D)), _const_spec((1, D)),
                  _const_spec((D, D)), _const_spec((1, D)), _const_spec((1, D)),
                  _const_spec((D, 2 * LANES)), _const_spec((1, LANES))],
        out_specs=[pl.BlockSpec((None, t, D), fwd),
                   pl.BlockSpec((t, D), flat),
                   pl.BlockSpec((t, LANES), flat)],
        out_shape=[jax.ShapeDtypeStruct((b, s, D), f32),
                   jax.ShapeDtypeStruct((b * s, D), bf16),
                   jax.ShapeDtypeStruct((b * s, LANES), f32)],
        scratch_shapes=[pltpu.VMEM((CHUNK, D), f32)],
        compiler_params=pltpu.CompilerParams(dimension_semantics=("arbitrary", "arbitrary"),
                                             vmem_limit_bytes=VMEM_LIMIT),
        name="mixer_b",
    )(yp, qr, kvf, p1, cr, x, mod, xif, cdf, wout, ln1w, ln1b, wr, br)


def _route_body(lg_ref, idx_ref, w_ref, rank_ref, cnt_ref):
    tt = SORT_TOKENS
    for j in range(lg_ref.shape[0] // tt):
        cols = slice(j * tt, (j + 1) * tt)
        idx, w, rank, cnt = _route_tile(lg_ref[cols, :])
        idx_ref[:, cols] = idx
        w_ref[:, cols] = w
        rank_ref[:, cols] = rank
        cnt_ref[j] = cnt


def _route_tile(logits):
    tt = logits.shape[0]
    l = logits.T[:E, :]
    iota_e = lax.broadcasted_iota(i32, (E, tt), 0)
    vals, idxs, hots = [], [], []
    for _ in range(TOPK):
        m = jnp.max(l, axis=0, keepdims=True)
        idx = jnp.min(jnp.where(l == m, iota_e, E), axis=0, keepdims=True)
        hot = iota_e == idx
        l = jnp.where(hot, -jnp.inf, l)
        vals.append(m)
        idxs.append(idx)
        hots.append(hot)
    exps = [jnp.exp(v_ - vals[0]) for v_ in vals]
    tot = exps[0] + exps[1] + exps[2] + exps[3]
    w = jnp.concatenate([e_ / tot for e_ in exps], axis=0)

    sel = jnp.zeros((E, tt), f32)
    for hot in hots:
        sel = sel + hot.astype(f32)
    upper = (lax.broadcasted_iota(i32, (tt, tt), 0) < lax.broadcasted_iota(i32, (tt, tt), 1)).astype(bf16)
    rank_e = jnp.dot(sel.astype(bf16), upper, preferred_element_type=f32)
    ranks = [jnp.sum(jnp.where(hot, rank_e, 0.0), axis=0, keepdims=True) for hot in hots]
    cnt = jnp.broadcast_to(jnp.sum(sel, axis=1, keepdims=True), (E, LANES))
    return jnp.concatenate(idxs, axis=0), w, jnp.concatenate(ranks, axis=0).astype(i32), cnt


def _route(logits):
    n = logits.shape[0]
    tt = SORT_TOKENS * ROUTE_TILES
    col = lambda i: (0, i)
    return pl.pallas_call(
        _route_body,
        grid=(n // tt,),
        in_specs=[pl.BlockSpec((tt, LANES), lambda i: (i, 0))],
        out_specs=[pl.BlockSpec((TOPK, tt), col), pl.BlockSpec((TOPK, tt), col), pl.BlockSpec((TOPK, tt), col),
                   pl.BlockSpec((ROUTE_TILES, E, LANES), lambda i: (i, 0, 0))],
        out_shape=[jax.ShapeDtypeStruct((TOPK, n), i32), jax.ShapeDtypeStruct((TOPK, n), f32),
                   jax.ShapeDtypeStruct((TOPK, n), i32), jax.ShapeDtypeStruct((n // SORT_TOKENS, E, LANES), f32)],
        compiler_params=pltpu.CompilerParams(dimension_semantics=("arbitrary",), vmem_limit_bytes=VMEM_LIMIT),
        name="route",
    )(logits)


def _unit_copy(src, dst, sem):
    return pltpu.make_async_copy(src, dst, sem)


def _start_units(n_units, start):
    def even(t, carry):
        start(2 * t, 0)
        return carry

    def odd(t, carry):
        start(2 * t + 1, 1)
        return carry

    lax.fori_loop(0, (n_units + 1) // 2, even, 0)
    lax.fori_loop(0, n_units // 2, odd, 0)


def _wait_units(n_units, wait_rows):
    for bit in reversed(range(MAX_UNITS.bit_length())):
        @pl.when((n_units >> bit) & 1 == 1)
        def _():
            wait_rows((1 << bit) * UNIT)


def _dispatch_body(srow_ref, tunits_ref, zstart_ref, zunits_ref,
                   lpos_ref, h2_ref, xbuf_ref, xs_ref, zero_ref, sem, zsem):
    tt = h2_ref.shape[0]
    i = pl.program_id(0)
    last = pl.num_programs(0) - 1
    slot = i % 2

    def start(j, priority):
        lrow = pl.multiple_of(j * UNIT, UNIT)
        srow = pl.multiple_of(srow_ref[i * MAX_UNITS + j], UNIT)
        _unit_copy(xs_ref.at[slot, pl.ds(lrow, UNIT)], xbuf_ref.at[pl.ds(srow, UNIT)],
                   sem.at[slot]).start(priority=priority)

    def drain(tile, slot_):
        _wait_units(tunits_ref[tile], lambda r: _unit_copy(
            xs_ref.at[slot_, pl.ds(0, r)], xbuf_ref.at[pl.ds(0, r)], sem.at[slot_]).wait())

    @pl.when(i >= 2)
    def _():
        drain(i - 2, slot)

    lpos = lpos_ref[...]
    rows = lax.broadcasted_iota(i32, (SORT_ROWS, tt), 0)
    perm = jnp.zeros((SORT_ROWS, tt), f32)
    for k_ in range(TOPK):
        perm = jnp.where(rows == lpos[k_:k_ + 1, :], 1.0, perm)
    xs_ref[slot] = jnp.dot(perm.astype(bf16), h2_ref[...], preferred_element_type=f32)
    _start_units(tunits_ref[i], start)

    @pl.when(i == last)
    def _():
        zero_ref[...] = jnp.zeros_like(zero_ref)

        def zero_copy(srow):
            return _unit_copy(zero_ref, xbuf_ref.at[pl.ds(srow, UNIT)], zsem)

        def fill(e_, carry):
            def start_one(u, c):
                zero_copy(pl.multiple_of(zstart_ref[e_] + u * UNIT, UNIT)).start()
                return c

            lax.fori_loop(0, zunits_ref[e_], start_one, 0)
            return carry

        def fill_wait(e_, carry):
            def wait_one(u, c):
                zero_copy(0).wait()
                return c

            lax.fori_loop(0, zunits_ref[e_], wait_one, 0)
            return carry

        lax.fori_loop(0, E + 1, fill, 0)
        lax.fori_loop(0, E + 1, fill_wait, 0)
        drain(i, slot)

        @pl.when(i >= 1)
        def _():
            drain(i - 1, 1 - slot)


def _dispatch(h2, lpos, srow, tunits, zstart, zunits, n_rows):
    n = h2.shape[0]
    tt = SORT_TOKENS
    return pl.pallas_call(
        _dispatch_body,
        grid_spec=pltpu.PrefetchScalarGridSpec(
            num_scalar_prefetch=4,
            grid=(n // tt,),
            in_specs=[pl.BlockSpec((TOPK, tt), lambda i, *_: (0, i)),
                      pl.BlockSpec((tt, D), lambda i, *_: (i, 0))],
            out_specs=pl.BlockSpec(memory_space=pl.ANY),
            scratch_shapes=[pltpu.VMEM((2, SORT_ROWS, D), f32), pltpu.VMEM((UNIT, D), f32),
                            pltpu.SemaphoreType.DMA((2,)), pltpu.SemaphoreType.DMA(())]),
        out_shape=jax.ShapeDtypeStruct((n_rows, D), f32),
        compiler_params=pltpu.CompilerParams(dimension_semantics=("arbitrary",), vmem_limit_bytes=VMEM_LIMIT),
        name="dispatch",
    )(srow, tunits, zstart, zunits, lpos, h2)


def _expert_body(be_ref, nused_ref, x_ref, wgu_ref, bgu_ref, wd_ref, bd_ref, y_ref):
    @pl.when(pl.program_id(0) >= nused_ref[0])
    def _():
        y_ref[...] = jnp.zeros_like(y_ref)

    @pl.when(pl.program_id(0) < nused_ref[0])
    def _():
        gu = jnp.dot(x_ref[...].astype(bf16), wgu_ref[...], preferred_element_type=f32) + bgu_ref[...]
        gate = jnp.minimum(gu[:, :FF], SWIGLU_LIMIT)
        up = jnp.clip(gu[:, FF:], -SWIGLU_LIMIT, SWIGLU_LIMIT)
        act = (up + 1.0) * gate * jax.nn.sigmoid(SWIGLU_ALPHA * gate)
        y_ref[...] = jnp.dot(act.astype(bf16), wd_ref[...], preferred_element_type=f32) + bd_ref[...]


def _experts(block_e, n_used, xbuf, wgu, bgu, wd, bd):
    n_rows = xbuf.shape[0]
    r = EXPERT_ROWS
    blk = lambda b_, nu: jnp.minimum(b_, nu[0] - 1)
    wspec = lambda shape: pl.BlockSpec((None,) + shape, lambda b_, be, nu: (be[blk(b_, nu)], 0, 0))
    return pl.pallas_call(
        _expert_body,
        grid_spec=pltpu.PrefetchScalarGridSpec(
            num_scalar_prefetch=2,
            grid=(n_rows // r,),
            in_specs=[pl.BlockSpec((r, D), lambda b_, be, nu: (blk(b_, nu), 0)),
                      wspec((D, 2 * FF)), wspec((1, 2 * FF)), wspec((FF, D)), wspec((1, D))],
            out_specs=pl.BlockSpec((r, D), lambda b_, be, nu: (b_, 0))),
        out_shape=jax.ShapeDtypeStruct((n_rows, D), f32),
        compiler_params=pltpu.CompilerParams(dimension_semantics=("arbitrary",), vmem_limit_bytes=VMEM_LIMIT),
        name="experts",
    )(block_e, n_used, xbuf, wgu, bgu, wd, bd)


def _combine_body(srow_ref, tunits_ref,
                  lpos_ref, w_ref, x1_ref, mod_ref, ln2w_ref, ln2b_ref, ybuf_ref, o_ref, ys_ref, sem):
    tt = x1_ref.shape[0]
    i = pl.program_id(0)
    slot = i % 2

    def fetch(tile, slot_):
        def start(j, priority):
            lrow = pl.multiple_of(j * UNIT, UNIT)
            srow = pl.multiple_of(srow_ref[tile * MAX_UNITS + j], UNIT)
            _unit_copy(ybuf_ref.at[pl.ds(srow, UNIT)], ys_ref.at[slot_, pl.ds(lrow, UNIT)],
                       sem.at[slot_]).start(priority=priority)

        _start_units(tunits_ref[tile], start)

    @pl.when(i == 0)
    def _():
        ys_ref[...] = jnp.zeros_like(ys_ref)
        fetch(0, 0)

    @pl.when(i + 1 < pl.num_programs(0))
    def _():
        fetch(i + 1, 1 - slot)

    _wait_units(tunits_ref[i], lambda r: _unit_copy(
        ybuf_ref.at[pl.ds(0, r)], ys_ref.at[slot, pl.ds(0, r)], sem.at[slot]).wait())

    lpos, w = lpos_ref[...], w_ref[...]
    cols = lax.broadcasted_iota(i32, (tt, SORT_ROWS), 1)
    wm = jnp.zeros((tt, SORT_ROWS), f32)
    for k_ in range(TOPK):
        wm = jnp.where(cols == lpos[:, k_:k_ + 1], w[:, k_:k_ + 1], wm)
    ff = jnp.dot(wm.astype(bf16), ys_ref[slot].astype(bf16), preferred_element_type=f32)
    z = DN_ALPHA * x1_ref[...] + mod_ref[5:6, :] * ff
    o_ref[...] = _normalize(z, LN_EPS) * ln2w_ref[...] + ln2b_ref[...]


def _combine(srow, tunits, lpos_t, w_t, x1, mod, ln2w, ln2b, ybuf, seq):
    n = x1.shape[0]
    tt = SORT_TOKENS
    tok = lambda width: pl.BlockSpec((tt, width), lambda i, *_: (i, 0))
    return pl.pallas_call(
        _combine_body,
        grid_spec=pltpu.PrefetchScalarGridSpec(
            num_scalar_prefetch=2,
            grid=(n // tt,),
            in_specs=[tok(TOPK), tok(TOPK), tok(D),
                      pl.BlockSpec((None, 6, D), lambda i, *_: ((i * tt) // seq, 0, 0)),
                      _const_spec((1, D)), _const_spec((1, D)),
                      pl.BlockSpec(memory_space=pl.ANY)],
            out_specs=tok(D),
            scratch_shapes=[pltpu.VMEM((2, SORT_ROWS, D), f32), pltpu.SemaphoreType.DMA((2,))]),
        out_shape=jax.ShapeDtypeStruct((n, D), f32),
        compiler_params=pltpu.CompilerParams(dimension_semantics=("arbitrary",), vmem_limit_bytes=VMEM_LIMIT),
        name="combine",
    )(srow, tunits, lpos_t, w_t, x1, mod, ln2w, ln2b, ybuf)


def _rope_tables(s):
    half = HD // 2
    nf32 = np.float32
    inv_freq = np.power(nf32(ROPE_BASE), (nf32(-2.0) * np.arange(half, dtype=nf32) / nf32(HD)).astype(nf32)).astype(nf32)
    ang = (np.arange(s, dtype=nf32)[:, None] * inv_freq[None, :]).astype(nf32).astype(np.float64)
    cos, sin = np.cos(ang).astype(nf32), np.sin(ang).astype(nf32)
    return jnp.asarray(np.concatenate([cos, cos], axis=1)), jnp.asarray(np.concatenate([-sin, sin], axis=1))


def _retention_tables(theta_f, theta_b):
    lg_f = jax.nn.log_sigmoid(theta_f.astype(f32))
    lg_b = jax.nn.log_sigmoid(theta_b.astype(f32))
    idx = jnp.arange(CHUNK, dtype=f32)
    diff = idx[:, None] - idx[None, :]
    dec_f = jnp.where(diff >= 0, jnp.exp(jnp.maximum(diff, 0.0)[None] * lg_f[:, None, None]), 0.0)
    dec_b = jnp.where(diff < 0, jnp.exp(jnp.maximum(-diff, 0.0)[None] * lg_b[:, None, None]), 0.0)
    kscale = HD ** -0.5
    mtab = ((dec_f + dec_b) * kscale).transpose(1, 0, 2).reshape(CHUNK, D)
    lanes = lambda t: jnp.repeat(t.T, HD, axis=1)
    zf = lanes(jnp.exp((CHUNK - 1.0 - idx)[None, :] * lg_f[:, None]) * kscale)
    zb = lanes(jnp.exp(idx[None, :] * lg_b[:, None]) * kscale)
    xif = lanes(jnp.exp((idx + 1.0)[None, :] * lg_f[:, None]))
    xib = lanes(jnp.exp((CHUNK - idx)[None, :] * lg_b[:, None]))
    cdf = jnp.repeat(jnp.exp(CHUNK * lg_f), HD)[None, :]
    cdb = jnp.repeat(jnp.exp(CHUNK * lg_b), HD)[None, :]
    return (mtab, zf, zb, xib, cdb), (xif, cdf)


def _hi_lo(a):
    hi = a.astype(bf16)
    return jnp.concatenate([hi, (a - hi.astype(f32)).astype(bf16)], axis=1)


def _trunk(x, mod, p):
    b, s, _ = x.shape
    n = b * s
    p1, cr, yp, qr, kvf = _mixer_a(x, mod, p["win"], p["glnw"], p["glnb"], p["wsp"], p["bsp"],
                                   _rope_tables(s), p["tabs_a"], p["gnw"])
    xif, cdf = p["tabs_b"]
    x1, h2, logits = _mixer_b(yp, qr, kvf, p1, cr, x, mod, xif, cdf, p["wout"], p["ln1w"], p["ln1b"],
                              p["wr"], p["br"])
    idx, w, lrank, cnt = _route(logits)

    tt = SORT_TOKENS
    nt = n // tt
    counts = cnt[:, :, 0].astype(i32)
    run = (counts + UNIT - 1) // UNIT * UNIT
    loff = jnp.cumsum(run, axis=1) - run
    total = jnp.sum(run, axis=0)
    padded = (total + EXPERT_ROWS - 1) // EXPERT_ROWS * EXPERT_ROWS
    pends = jnp.cumsum(padded)
    pstarts = pends - padded
    seg = pstarts[None, :] + jnp.cumsum(run, axis=0) - run
    n_blocks = (n * TOPK + nt * E * (UNIT - 1)) // EXPERT_ROWS + E
    n_used = (pends[E - 1:] // EXPERT_ROWS).astype(i32)
    block_start = jnp.arange(n_blocks, dtype=i32) * EXPERT_ROWS
    block_e = jnp.minimum(jnp.sum((pends[None, :] <= block_start[:, None]).astype(i32), axis=1), E - 1)
    lpos = lrank.reshape(TOPK, nt, tt)
    idx3 = idx.reshape(TOPK, nt, tt)
    for e_ in range(E):
        lpos = lpos + jnp.where(idx3 == e_, loff[None, :, e_, None], 0)
    lpos = lpos.reshape(TOPK, n)
    ustart = jnp.arange(MAX_UNITS, dtype=i32) * UNIT
    run_of = jnp.sum(((loff + run)[:, None, :] <= ustart[None, :, None]).astype(i32), axis=2)
    srow = jnp.broadcast_to(ustart[None, :], (nt, MAX_UNITS))
    for e_ in range(E):
        srow = srow + jnp.where(run_of == e_, (seg - loff)[:, e_, None], 0)
    flat = lambda a: a.reshape(-1).astype(i32)
    tunits = flat(jnp.sum(run, axis=1) // UNIT)

    n_rows = n_blocks * EXPERT_ROWS
    zstart = jnp.concatenate([pstarts + total, pends[E - 1:]])
    zunits = jnp.concatenate([padded - total, n_rows - pends[E - 1:]]) // UNIT
    xbuf = _dispatch(h2, lpos, flat(srow), tunits, flat(zstart), flat(zunits), n_rows)
    ybuf = _experts(block_e.astype(i32), n_used, xbuf, p["wgu"], p["bgu"], p["wd"], p["bd"])
    out = _combine(flat(srow), tunits, lpos.T, w.T, x1.reshape(n, D), mod, p["ln2w"], p["ln2b"], ybuf, s)
    return out.reshape(b, s, D)


def kernel(x_prompt, x_sample, c_prompt, c_sample, w_ada, b_ada, w_in, gmlp_ln_w, gmlp_ln_b, w_spatial, b_spatial, ret_theta_fwd, ret_theta_bwd, ret_gn_w, w_out, ln1_w, ln1_b, w_router, b_router, w_gate_up, b_gate_up, w_down, b_down, ln2_w, ln2_b):
    l = 0
    bp, bs = x_prompt.shape[0], x_sample.shape[0]
    c_all = jnp.concatenate([c_prompt, c_sample], axis=0)
    c_rows = -(-c_all.shape[0] // SUBLANES) * SUBLANES
    c_all = jnp.pad(c_all, ((0, c_rows - c_all.shape[0]), (0, 0)))
    mod = _adaln(c_all, w_ada[l], b_ada[l]).reshape(c_rows, 6, D)

    tabs_a, tabs_b = _retention_tables(ret_theta_fwd[l], ret_theta_bwd[l])
    row = lambda a: a.reshape(1, -1)
    p = dict(
        win=w_in[l].astype(bf16), glnw=row(gmlp_ln_w[l]), glnb=row(gmlp_ln_b[l]),
        wsp=w_spatial[l].astype(bf16), bsp=jnp.repeat(b_spatial[l].T, HD, axis=1),
        tabs_a=tabs_a, tabs_b=tabs_b, gnw=row(ret_gn_w[l]),
        wout=w_out[l].astype(bf16), ln1w=row(ln1_w[l]), ln1b=row(ln1_b[l]),
        wr=_hi_lo(jnp.pad(w_router[l], ((0, 0), (0, LANES - E)))), br=jnp.pad(row(b_router[l]), ((0, 0), (0, LANES - E))),
        wgu=w_gate_up[l].astype(bf16), bgu=b_gate_up[l].reshape(E, 1, 2 * FF),
        wd=w_down[l].astype(bf16), bd=b_down[l].reshape(E, 1, D),
        ln2w=row(ln2_w[l]), ln2b=row(ln2_b[l]),
    )
    y_prompt = _trunk(x_prompt, mod[:bp], p)
    y_sample = _trunk(x_sample, mod[bp:bp + bs], p)
    return (y_prompt, y_sample)
```

```python
import functools

import numpy as np
import jax
import jax.numpy as jnp
from jax import lax
from jax.experimental import pallas as pl
from jax.experimental.pallas import tpu as pltpu

f32 = jnp.float32
bf16 = jnp.bfloat16
i32 = jnp.int32

D = 1024
CHUNK = 128
HEADS = 8
HD = D // HEADS
N_SEG = 8
E = 32
TOPK = 4
FF = 1024
ROPE_BASE = 10000.0
SWIGLU_LIMIT = 7.0
SWIGLU_ALPHA = 1.702
LN_EPS = 1e-5
GN_EPS = 1e-6
DEPTH = 1
DN_ALPHA = (2 * DEPTH) ** 0.25

LANES = 128
SUBLANES = 8

MIX_TOKENS = 512
SORT_TOKENS = 512
UNIT = SUBLANES
SORT_ROWS = -(-(SORT_TOKENS * TOPK + E * (UNIT - 1)) // LANES) * LANES
MAX_UNITS = SORT_ROWS // UNIT
ROUTE_TILES = 4
EXPERT_ROWS = 512
VMEM_LIMIT = 56 * 1024 * 1024

_HI = lax.Precision.HIGHEST


def _const_spec(shape):
    nd = len(shape)
    return pl.BlockSpec(shape, lambda *_: (0,) * nd, pipeline_mode=pl.Buffered(1))


def _gelu(x):
    return 0.5 * x * (1.0 + lax.erf(x * (2.0 ** -0.5)))


def _normalize(x, eps):
    mu = jnp.mean(x, axis=-1, keepdims=True)
    xc = x - mu
    var = jnp.mean(xc * xc, axis=-1, keepdims=True)
    return xc * lax.rsqrt(var + eps)


def _silu(x):
    return x * jax.nn.sigmoid(x)


def _head(h):
    return slice(h * HD, (h + 1) * HD)


def _adaln_body(c_ref, w_ref, b_ref, o_ref):
    s = _silu(c_ref[...])
    o_ref[...] = jnp.dot(s, w_ref[...], preferred_element_type=f32, precision=_HI) + b_ref[...]


def _adaln(c, w_ada, b_ada):
    rows = c.shape[0]
    return pl.pallas_call(
        _adaln_body,
        grid=(6,),
        in_specs=[pl.BlockSpec((rows, D), lambda j: (0, 0)),
                  pl.BlockSpec((D, D), lambda j: (0, j)),
                  pl.BlockSpec((1, D), lambda j: (0, j))],
        out_specs=pl.BlockSpec((rows, D), lambda j: (0, j)),
        out_shape=jax.ShapeDtypeStruct((rows, 6 * D), f32),
        compiler_params=pltpu.CompilerParams(vmem_limit_bytes=VMEM_LIMIT),
        name="adaln",
    )(c, w_ada, b_ada.reshape(1, 6 * D))


def _mixer_a_body(nck, x_ref, mod_ref, win_ref, glnw_ref, glnb_ref, wsp_ref, bsp_ref,
                  cos_ref, sin_ref, mtab_ref, zf_ref, zb_ref, xib_ref, cdb_ref, gnw_ref,
                  p1_ref, cr_ref, yp_ref, qr_ref, kvf_ref, sb_ref):
    @pl.when(pl.program_id(1) == 0)
    def _():
        sb_ref[...] = jnp.zeros_like(sb_ref)

    h = (x_ref[...] * (1.0 + mod_ref[1:2, :]) + mod_ref[0:1, :]).astype(bf16)

    def seg(j):
        return jnp.dot(h, win_ref[:, j * D:(j + 1) * D], preferred_element_type=f32)

    vn = _normalize(_gelu(seg(1)), LN_EPS) * glnw_ref[...] + glnb_ref[...]
    vnb = vn.astype(bf16)
    ug = _gelu(seg(0))
    ga = jax.nn.sigmoid(seg(6))
    for c in range(nck):
        rows = slice(c * CHUNK, (c + 1) * CHUNK)
        mixed = jnp.concatenate(
            [jnp.dot(wsp_ref[g], vnb[rows, _head(g)], preferred_element_type=f32) for g in range(HEADS)], axis=1)
        p1_ref[rows, :] = (ga[rows, :] * (ug[rows, :] * (mixed + bsp_ref[...]))).astype(bf16)

    q = seg(2)
    k = seg(3)
    v = seg(4).astype(bf16)
    cos, sin = cos_ref[...], sin_ref[...]

    def rotary(t):
        return jnp.concatenate([t[:, _head(h_)] * cos + pltpu.roll(t[:, _head(h_)], HD // 2, 1) * sin
                                for h_ in range(HEADS)], axis=1)

    qf = rotary(q)
    kr = rotary(k)
    qr = qf.astype(bf16)
    qr_ref[...] = qr
    for c in reversed(range(nck)):
        rows = slice(c * CHUNK, (c + 1) * CHUNK)
        kc = kr[rows, :]
        kcb = kc.astype(bf16)
        kzf = (kc * zf_ref[...]).astype(bf16)
        kzb = (kc * zb_ref[...]).astype(bf16)
        qx = (qf[rows, :] * xib_ref[...]).astype(bf16)
        sb = sb_ref[...]
        sbb = sb.astype(bf16)
        yps, kvfs, kvbs = [], [], []
        for h_ in range(HEADS):
            hs = _head(h_)
            vh = v[rows, hs]
            sc = lax.dot_general(qr[rows, hs], kcb[:, hs], (((1,), (1,)), ((), ())), preferred_element_type=f32)
            sc = (sc * mtab_ref[:, hs]).astype(bf16)
            yps.append(jnp.dot(sc, vh, preferred_element_type=f32)
                       + jnp.dot(qx[:, hs], sbb[:, hs], preferred_element_type=f32))
            kvfs.append(lax.dot_general(kzf[:, hs], vh, (((0,), (0,)), ((), ())), preferred_element_type=f32))
            kvbs.append(lax.dot_general(kzb[:, hs], vh, (((0,), (0,)), ((), ())), preferred_element_type=f32))
        yp_ref[rows, :] = jnp.concatenate(yps, axis=1).astype(bf16)
        kvf_ref[c] = jnp.concatenate(kvfs, axis=1)
        sb_ref[...] = sb * cdb_ref[...] + jnp.concatenate(kvbs, axis=1)

    cr_ref[...] = (jax.nn.sigmoid(seg(7)) * _silu(seg(5)) * gnw_ref[...]).astype(bf16)


def _mixer_a(x, mod, win, glnw, glnb, wsp, bsp, rope, tabs, gnw):
    b, s, _ = x.shape
    t = MIX_TOKENS
    nt, nck = s // t, t // CHUNK
    cos, sin = rope
    mtab, zf, zb, xib, cdb = tabs
    rev = lambda bi, i: (bi, nt - 1 - i, 0)
    rope_spec = pl.BlockSpec((t, HD), lambda bi, i: (nt - 1 - i, 0))
    tok = lambda dt: jax.ShapeDtypeStruct((b, s, D), dt)
    return pl.pallas_call(
        functools.partial(_mixer_a_body, nck),
        grid=(b, nt),
        in_specs=[pl.BlockSpec((None, t, D), rev),
                  pl.BlockSpec((None, 6, D), lambda bi, i: (bi, 0, 0)),
                  _const_spec((D, N_SEG * D)),
                  _const_spec((1, D)), _const_spec((1, D)),
                  _const_spec((HEADS, CHUNK, CHUNK)), _const_spec((CHUNK, D)),
                  rope_spec, rope_spec,
                  _const_spec((CHUNK, D)), _const_spec((CHUNK, D)), _const_spec((CHUNK, D)),
                  _const_spec((CHUNK, D)), _const_spec((1, D)), _const_spec((1, D))],
        out_specs=[pl.BlockSpec((None, t, D), rev)] * 4
                  + [pl.BlockSpec((None, nck, CHUNK, D), lambda bi, i: (bi, nt - 1 - i, 0, 0))],
        out_shape=[tok(bf16), tok(bf16), tok(bf16), tok(bf16),
                   jax.ShapeDtypeStruct((b, s // CHUNK, CHUNK, D), f32)],
        scratch_shapes=[pltpu.VMEM((CHUNK, D), f32)],
        compiler_params=pltpu.CompilerParams(dimension_semantics=("arbitrary", "arbitrary"),
                                             vmem_limit_bytes=VMEM_LIMIT),
        name="mixer_a",
    )(x, mod, win, glnw, glnb, wsp, bsp, cos, sin, mtab, zf, zb, xib, cdb, gnw)


def _mixer_b_body(nck, yp_ref, qr_ref, kvf_ref, p1_ref, cr_ref, x_ref, mod_ref, xif_ref, cdf_ref,
                  wout_ref, ln1w_ref, ln1b_ref, wr_ref, br_ref,
                  x1_ref, h2_ref, lg_ref, sf_ref):
    @pl.when(pl.program_id(1) == 0)
    def _():
        sf_ref[...] = jnp.zeros_like(sf_ref)

    ys = []
    for c in range(nck):
        rows = slice(c * CHUNK, (c + 1) * CHUNK)
        sf = sf_ref[...]
        sfb = sf.astype(bf16)
        cross = jnp.concatenate(
            [jnp.dot(qr_ref[rows, _head(h_)], sfb[:, _head(h_)], preferred_element_type=f32)
             for h_ in range(HEADS)], axis=1)
        ys.append(yp_ref[rows, :].astype(f32) + cross * xif_ref[...])
        sf_ref[...] = sf * cdf_ref[...] + kvf_ref[c]
    y = jnp.concatenate(ys, axis=0)
    yn = jnp.concatenate([_normalize(y[:, _head(h_)], GN_EPS) for h_ in range(HEADS)], axis=1)
    merged = p1_ref[...].astype(f32) + cr_ref[...].astype(f32) * yn
    mix = jnp.dot(merged.astype(bf16), wout_ref[...], preferred_element_type=f32)
    x1 = _normalize(DN_ALPHA * x_ref[...] + mod_ref[2:3, :] * mix, LN_EPS) * ln1w_ref[...] + ln1b_ref[...]
    x1_ref[...] = x1
    h2 = x1 * (1.0 + mod_ref[4:5, :]) + mod_ref[3:4, :]
    h2_hi = h2.astype(bf16)
    h2_ref[...] = h2_hi
    h2_lo = (h2 - h2_hi.astype(f32)).astype(bf16)
    hi_terms = jnp.dot(h2_hi, wr_ref[...], preferred_element_type=f32)
    lo_term = jnp.dot(h2_lo, wr_ref[:, :LANES], preferred_element_type=f32)
    lg_ref[...] = hi_terms[:, :LANES] + hi_terms[:, LANES:] + lo_term + br_ref[...]


def _mixer_b(yp, qr, kvf, p1, cr, x, mod, xif, cdf, wout, ln1w, ln1b, wr, br):
    b, s, _ = x.shape
    t = MIX_TOKENS
    nt, nck = s // t, t // CHUNK
    fwd = lambda bi, i: (bi, i, 0)
    flat = lambda bi, i: (bi * nt + i, 0)
    return pl.pallas_call(
        functools.partial(_mixer_b_body, nck),
        grid=(b, nt),
        in_specs=[pl.BlockSpec((None, t, D), fwd), pl.BlockSpec((None, t, D), fwd),
                  pl.BlockSpec((None, nck, CHUNK, D), lambda bi, i: (bi, i, 0, 0)),
                  pl.BlockSpec((None, t, D), fwd), pl.BlockSpec((None, t, D), fwd),
                  pl.BlockSpec((None, t, D), fwd),
                  pl.BlockSpec((None, 6, D), lambda bi, i: (bi, 0, 0)),
                  _const_spec((CHUNK, D)), _const_spec((1, D)),
                  _const_spec((D, D)), _const_spec((1, D)), _const_spec((1, D)),
                  _const_spec((D, 2 * LANES)), _const_spec((1, LANES))],
        out_specs=[pl.BlockSpec((None, t, D), fwd),
                   pl.BlockSpec((t, D), flat),
                   pl.BlockSpec((t, LANES), flat)],
        out_shape=[jax.ShapeDtypeStruct((b, s, D), f32),
                   jax.ShapeDtypeStruct((b * s, D), bf16),
                   jax.ShapeDtypeStruct((b * s, LANES), f32)],
        scratch_shapes=[pltpu.VMEM((CHUNK, D), f32)],
        compiler_params=pltpu.CompilerParams(dimension_semantics=("arbitrary", "arbitrary"),
                                             vmem_limit_bytes=VMEM_LIMIT),
        name="mixer_b",
    )(yp, qr, kvf, p1, cr, x, mod, xif, cdf, wout, ln1w, ln1b, wr, br)


def _route_body(lg_ref, idx_ref, w_ref, rank_ref, cnt_ref):
    tt = SORT_TOKENS
    for j in range(lg_ref.shape[0] // tt):
        cols = slice(j * tt, (j + 1) * tt)
        idx, w, rank, cnt = _route_tile(lg_ref[cols, :])
        idx_ref[:, cols] = idx
        w_ref[:, cols] = w
        rank_ref[:, cols] = rank
        cnt_ref[j] = cnt


def _route_tile(logits):
    tt = logits.shape[0]
    l = logits.T[:E, :]
    iota_e = lax.broadcasted_iota(i32, (E, tt), 0)
    vals, idxs, hots = [], [], []
    for _ in range(TOPK):
        m = jnp.max(l, axis=0, keepdims=True)
        idx = jnp.min(jnp.where(l == m, iota_e, E), axis=0, keepdims=True)
        hot = iota_e == idx
        l = jnp.where(hot, -jnp.inf, l)
        vals.append(m)
        idxs.append(idx)
        hots.append(hot)
    exps = [jnp.exp(v_ - vals[0]) for v_ in vals]
    tot = exps[0] + exps[1] + exps[2] + exps[3]
    w = jnp.concatenate([e_ / tot for e_ in exps], axis=0)

    sel = jnp.zeros((E, tt), f32)
    for hot in hots:
        sel = sel + hot.astype(f32)
    upper = (lax.broadcasted_iota(i32, (tt, tt), 0) < lax.broadcasted_iota(i32, (tt, tt), 1)).astype(bf16)
    rank_e = jnp.dot(sel.astype(bf16), upper, preferred_element_type=f32)
    ranks = [jnp.sum(jnp.where(hot, rank_e, 0.0), axis=0, keepdims=True) for hot in hots]
    cnt = jnp.broadcast_to(jnp.sum(sel, axis=1, keepdims=True), (E, LANES))
    return jnp.concatenate(idxs, axis=0), w, jnp.concatenate(ranks, axis=0).astype(i32), cnt


def _route(logits):
    n = logits.shape[0]
    tt = SORT_TOKENS * ROUTE_TILES
    col = lambda i: (0, i)
    return pl.pallas_call(
        _route_body,
        grid=(n // tt,),
        in_specs=[pl.BlockSpec((tt, LANES), lambda i: (i, 0))],
        out_specs=[pl.BlockSpec((TOPK, tt), col), pl.BlockSpec((TOPK, tt), col), pl.BlockSpec((TOPK, tt), col),
                   pl.BlockSpec((ROUTE_TILES, E, LANES), lambda i: (i, 0, 0))],
        out_shape=[jax.ShapeDtypeStruct((TOPK, n), i32), jax.ShapeDtypeStruct((TOPK, n), f32),
                   jax.ShapeDtypeStruct((TOPK, n), i32), jax.ShapeDtypeStruct((n // SORT_TOKENS, E, LANES), f32)],
        compiler_params=pltpu.CompilerParams(dimension_semantics=("arbitrary",), vmem_limit_bytes=VMEM_LIMIT),
        name="route",
    )(logits)


def _unit_copy(src, dst, sem):
    return pltpu.make_async_copy(src, dst, sem)


def _start_units(n_units, start):
    def even(t, carry):
        start(2 * t, 0)
        return carry

    def odd(t, carry):
        start(2 * t + 1, 1)
        return carry

    lax.fori_loop(0, (n_units + 1) // 2, even, 0)
    lax.fori_loop(0, n_units // 2, odd, 0)


def _wait_units(n_units, wait_rows):
    for bit in reversed(range(MAX_UNITS.bit_length())):
        @pl.when((n_units >> bit) & 1 == 1)
        def _():
            wait_rows((1 << bit) * UNIT)


def _dispatch_body(srow_ref, tunits_ref, zstart_ref, zunits_ref,
                   lpos_ref, h2_ref, xbuf_ref, xs_ref, zero_ref, sem, zsem):
    tt = h2_ref.shape[0]
    i = pl.program_id(0)
    last = pl.num_programs(0) - 1
    slot = i % 2

    def start(j, priority):
        lrow = pl.multiple_of(j * UNIT, UNIT)
        srow = pl.multiple_of(srow_ref[i * MAX_UNITS + j], UNIT)
        _unit_copy(xs_ref.at[slot, pl.ds(lrow, UNIT)], xbuf_ref.at[pl.ds(srow, UNIT)],
                   sem.at[slot]).start(priority=priority)

    def drain(tile, slot_):
        _wait_units(tunits_ref[tile], lambda r: _unit_copy(
            xs_ref.at[slot_, pl.ds(0, r)], xbuf_ref.at[pl.ds(0, r)], sem.at[slot_]).wait())

    @pl.when(i >= 2)
    def _():
        drain(i - 2, slot)

    lpos = lpos_ref[...]
    rows = lax.broadcasted_iota(i32, (SORT_ROWS, tt), 0)
    perm = jnp.zeros((SORT_ROWS, tt), f32)
    for k_ in range(TOPK):
        perm = jnp.where(rows == lpos[k_:k_ + 1, :], 1.0, perm)
    xs_ref[slot] = jnp.dot(perm.astype(bf16), h2_ref[...], preferred_element_type=f32)
    _start_units(tunits_ref[i], start)

    @pl.when(i == last)
    def _():
        zero_ref[...] = jnp.zeros_like(zero_ref)

        def zero_copy(srow):
            return _unit_copy(zero_ref, xbuf_ref.at[pl.ds(srow, UNIT)], zsem)

        def fill(e_, carry):
            def start_one(u, c):
                zero_copy(pl.multiple_of(zstart_ref[e_] + u * UNIT, UNIT)).start()
                return c

            lax.fori_loop(0, zunits_ref[e_], start_one, 0)
            return carry

        def fill_wait(e_, carry):
            def wait_one(u, c):
                zero_copy(0).wait()
                return c

            lax.fori_loop(0, zunits_ref[e_], wait_one, 0)
            return carry

        lax.fori_loop(0, E + 1, fill, 0)
        lax.fori_loop(0, E + 1, fill_wait, 0)
        drain(i, slot)

        @pl.when(i >= 1)
        def _():
            drain(i - 1, 1 - slot)


def _dispatch(h2, lpos, srow, tunits, zstart, zunits, n_rows):
    n = h2.shape[0]
    tt = SORT_TOKENS
    return pl.pallas_call(
        _dispatch_body,
        grid_spec=pltpu.PrefetchScalarGridSpec(
            num_scalar_prefetch=4,
            grid=(n // tt,),
            in_specs=[pl.BlockSpec((TOPK, tt), lambda i, *_: (0, i)),
                      pl.BlockSpec((tt, D), lambda i, *_: (i, 0))],
            out_specs=pl.BlockSpec(memory_space=pl.ANY),
            scratch_shapes=[pltpu.VMEM((2, SORT_ROWS, D), f32), pltpu.VMEM((UNIT, D), f32),
                            pltpu.SemaphoreType.DMA((2,)), pltpu.SemaphoreType.DMA(())]),
        out_shape=jax.ShapeDtypeStruct((n_rows, D), f32),
        compiler_params=pltpu.CompilerParams(dimension_semantics=("arbitrary",), vmem_limit_bytes=VMEM_LIMIT),
        name="dispatch",
    )(srow, tunits, zstart, zunits, lpos, h2)


def _expert_body(be_ref, nused_ref, x_ref, wgu_ref, bgu_ref, wd_ref, bd_ref, y_ref):
    @pl.when(pl.program_id(0) >= nused_ref[0])
    def _():
        y_ref[...] = jnp.zeros_like(y_ref)

    @pl.when(pl.program_id(0) < nused_ref[0])
    def _():
        gu = jnp.dot(x_ref[...].astype(bf16), wgu_ref[...], preferred_element_type=f32) + bgu_ref[...]
        gate = jnp.minimum(gu[:, :FF], SWIGLU_LIMIT)
        up = jnp.clip(gu[:, FF:], -SWIGLU_LIMIT, SWIGLU_LIMIT)
        act = (up + 1.0) * gate * jax.nn.sigmoid(SWIGLU_ALPHA * gate)
        y_ref[...] = jnp.dot(act.astype(bf16), wd_ref[...], preferred_element_type=f32) + bd_ref[...]


def _experts(block_e, n_used, xbuf, wgu, bgu, wd, bd):
    n_rows = xbuf.shape[0]
    r = EXPERT_ROWS
    blk = lambda b_, nu: jnp.minimum(b_, nu[0] - 1)
    wspec = lambda shape: pl.BlockSpec((None,) + shape, lambda b_, be, nu: (be[blk(b_, nu)], 0, 0))
    return pl.pallas_call(
        _expert_body,
        grid_spec=pltpu.PrefetchScalarGridSpec(
            num_scalar_prefetch=2,
            grid=(n_rows // r,),
            in_specs=[pl.BlockSpec((r, D), lambda b_, be, nu: (blk(b_, nu), 0)),
                      wspec((D, 2 * FF)), wspec((1, 2 * FF)), wspec((FF, D)), wspec((1, D))],
            out_specs=pl.BlockSpec((r, D), lambda b_, be, nu: (b_, 0))),
        out_shape=jax.ShapeDtypeStruct((n_rows, D), f32),
        compiler_params=pltpu.CompilerParams(dimension_semantics=("arbitrary",), vmem_limit_bytes=VMEM_LIMIT),
        name="experts",
    )(block_e, n_used, xbuf, wgu, bgu, wd, bd)


def _combine_body(srow_ref, tunits_ref,
                  lpos_ref, w_ref, x1_ref, mod_ref, ln2w_ref, ln2b_ref, ybuf_ref, o_ref, ys_ref, sem):
    tt = x1_ref.shape[0]
    i = pl.program_id(0)
    slot = i % 2

    def fetch(tile, slot_):
        def start(j, priority):
            lrow = pl.multiple_of(j * UNIT, UNIT)
            srow = pl.multiple_of(srow_ref[tile * MAX_UNITS + j], UNIT)
            _unit_copy(ybuf_ref.at[pl.ds(srow, UNIT)], ys_ref.at[slot_, pl.ds(lrow, UNIT)],
                       sem.at[slot_]).start(priority=priority)

        _start_units(tunits_ref[tile], start)

    @pl.when(i == 0)
    def _():
        ys_ref[...] = jnp.zeros_like(ys_ref)
        fetch(0, 0)

    @pl.when(i + 1 < pl.num_programs(0))
    def _():
        fetch(i + 1, 1 - slot)

    _wait_units(tunits_ref[i], lambda r: _unit_copy(
        ybuf_ref.at[pl.ds(0, r)], ys_ref.at[slot, pl.ds(0, r)], sem.at[slot]).wait())

    lpos, w = lpos_ref[...], w_ref[...]
    cols = lax.broadcasted_iota(i32, (tt, SORT_ROWS), 1)
    wm = jnp.zeros((tt, SORT_ROWS), f32)
    for k_ in range(TOPK):
        wm = jnp.where(cols == lpos[:, k_:k_ + 1], w[:, k_:k_ + 1], wm)
    ff = jnp.dot(wm.astype(bf16), ys_ref[slot].astype(bf16), preferred_element_type=f32)
    z = DN_ALPHA * x1_ref[...] + mod_ref[5:6, :] * ff
    o_ref[...] = _normalize(z, LN_EPS) * ln2w_ref[...] + ln2b_ref[...]


def _combine(srow, tunits, lpos_t, w_t, x1, mod, ln2w, ln2b, ybuf, seq):
    n = x1.shape[0]
    tt = SORT_TOKENS
    tok = lambda width: pl.BlockSpec((tt, width), lambda i, *_: (i, 0))
    return pl.pallas_call(
        _combine_body,
        grid_spec=pltpu.PrefetchScalarGridSpec(
            num_scalar_prefetch=2,
            grid=(n // tt,),
            in_specs=[tok(TOPK), tok(TOPK), tok(D),
                      pl.BlockSpec((None, 6, D), lambda i, *_: ((i * tt) // seq, 0, 0)),
                      _const_spec((1, D)), _const_spec((1, D)),
                      pl.BlockSpec(memory_space=pl.ANY)],
            out_specs=tok(D),
            scratch_shapes=[pltpu.VMEM((2, SORT_ROWS, D), f32), pltpu.SemaphoreType.DMA((2,))]),
        out_shape=jax.ShapeDtypeStruct((n, D), f32),
        compiler_params=pltpu.CompilerParams(dimension_semantics=("arbitrary",), vmem_limit_bytes=VMEM_LIMIT),
        name="combine",
    )(srow, tunits, lpos_t, w_t, x1, mod, ln2w, ln2b, ybuf)


def _rope_tables(s):
    half = HD // 2
    nf32 = np.float32
    inv_freq = np.power(nf32(ROPE_BASE), (nf32(-2.0) * np.arange(half, dtype=nf32) / nf32(HD)).astype(nf32)).astype(nf32)
    ang = (np.arange(s, dtype=nf32)[:, None] * inv_freq[None, :]).astype(nf32).astype(np.float64)
    cos, sin = np.cos(ang).astype(nf32), np.sin(ang).astype(nf32)
    return jnp.asarray(np.concatenate([cos, cos], axis=1)), jnp.asarray(np.concatenate([-sin, sin], axis=1))


def _retention_tables(theta_f, theta_b):
    lg_f = jax.nn.log_sigmoid(theta_f.astype(f32))
    lg_b = jax.nn.log_sigmoid(theta_b.astype(f32))
    idx = jnp.arange(CHUNK, dtype=f32)
    diff = idx[:, None] - idx[None, :]
    dec_f = jnp.where(diff >= 0, jnp.exp(jnp.maximum(diff, 0.0)[None] * lg_f[:, None, None]), 0.0)
    dec_b = jnp.where(diff < 0, jnp.exp(jnp.maximum(-diff, 0.0)[None] * lg_b[:, None, None]), 0.0)
    kscale = HD ** -0.5
    mtab = ((dec_f + dec_b) * kscale).transpose(1, 0, 2).reshape(CHUNK, D)
    lanes = lambda t: jnp.repeat(t.T, HD, axis=1)
    zf = lanes(jnp.exp((CHUNK - 1.0 - idx)[None, :] * lg_f[:, None]) * kscale)
    zb = lanes(jnp.exp(idx[None, :] * lg_b[:, None]) * kscale)
    xif = lanes(jnp.exp((idx + 1.0)[None, :] * lg_f[:, None]))
    xib = lanes(jnp.exp((CHUNK - idx)[None, :] * lg_b[:, None]))
    cdf = jnp.repeat(jnp.exp(CHUNK * lg_f), HD)[None, :]
    cdb = jnp.repeat(jnp.exp(CHUNK * lg_b), HD)[None, :]
    return (mtab, zf, zb, xib, cdb), (xif, cdf)


def _hi_lo(a):
    hi = a.astype(bf16)
    return jnp.concatenate([hi, (a - hi.astype(f32)).astype(bf16)], axis=1)


def _trunk(x, mod, p):
    b, s, _ = x.shape
    n = b * s
    p1, cr, yp, qr, kvf = _mixer_a(x, mod, p["win"], p["glnw"], p["glnb"], p["wsp"], p["bsp"],
                                   _rope_tables(s), p["tabs_a"], p["gnw"])
    xif, cdf = p["tabs_b"]
    x1, h2, logits = _mixer_b(yp, qr, kvf, p1, cr, x, mod, xif, cdf, p["wout"], p["ln1w"], p["ln1b"],
                              p["wr"], p["br"])
    idx, w, lrank, cnt = _route(logits)

    tt = SORT_TOKENS
    nt = n // tt
    counts = cnt[:, :, 0].astype(i32)
    run = (counts + UNIT - 1) // UNIT * UNIT
    loff = jnp.cumsum(run, axis=1) - run
    total = jnp.sum(run, axis=0)
    padded = (total + EXPERT_ROWS - 1) // EXPERT_ROWS * EXPERT_ROWS
    pends = jnp.cumsum(padded)
    pstarts = pends - padded
    seg = pstarts[None, :] + jnp.cumsum(run, axis=0) - run
    n_blocks = (n * TOPK + nt * E * (UNIT - 1)) // EXPERT_ROWS + E
    n_used = (pends[E - 1:] // EXPERT_ROWS).astype(i32)
    block_start = jnp.arange(n_blocks, dtype=i32) * EXPERT_ROWS
    block_e = jnp.minimum(jnp.sum((pends[None, :] <= block_start[:, None]).astype(i32), axis=1), E - 1)
    lpos = lrank.reshape(TOPK, nt, tt)
    idx3 = idx.reshape(TOPK, nt, tt)
    for e_ in range(E):
        lpos = lpos + jnp.where(idx3 == e_, loff[None, :, e_, None], 0)
    lpos = lpos.reshape(TOPK, n)
    ustart = jnp.arange(MAX_UNITS, dtype=i32) * UNIT
    run_of = jnp.sum(((loff + run)[:, None, :] <= ustart[None, :, None]).astype(i32), axis=2)
    srow = jnp.broadcast_to(ustart[None, :], (nt, MAX_UNITS))
    for e_ in range(E):
        srow = srow + jnp.where(run_of == e_, (seg - loff)[:, e_, None], 0)
    flat = lambda a: a.reshape(-1).astype(i32)
    tunits = flat(jnp.sum(run, axis=1) // UNIT)

    n_rows = n_blocks * EXPERT_ROWS
    zstart = jnp.concatenate([pstarts + total, pends[E - 1:]])
    zunits = jnp.concatenate([padded - total, n_rows - pends[E - 1:]]) // UNIT
    xbuf = _dispatch(h2, lpos, flat(srow), tunits, flat(zstart), flat(zunits), n_rows)
    ybuf = _experts(block_e.astype(i32), n_used, xbuf, p["wgu"], p["bgu"], p["wd"], p["bd"])
    out = _combine(flat(srow), tunits, lpos.T, w.T, x1.reshape(n, D), mod, p["ln2w"], p["ln2b"], ybuf, s)
    return out.reshape(b, s, D)


def kernel(x_prompt, x_sample, c_prompt, c_sample, w_ada, b_ada, w_in, gmlp_ln_w, gmlp_ln_b, w_spatial, b_spatial, ret_theta_fwd, ret_theta_bwd, ret_gn_w, w_out, ln1_w, ln1_b, w_router, b_router, w_gate_up, b_gate_up, w_down, b_down, ln2_w, ln2_b):
    l = 0
    bp, bs = x_prompt.shape[0], x_sample.shape[0]
    c_all = jnp.concatenate([c_prompt, c_sample], axis=0)
    c_rows = -(-c_all.shape[0] // SUBLANES) * SUBLANES
    c_all = jnp.pad(c_all, ((0, c_rows - c_all.shape[0]), (0, 0)))
    mod = _adaln(c_all, w_ada[l], b_ada[l]).reshape(c_rows, 6, D)

    tabs_a, tabs_b = _retention_tables(ret_theta_fwd[l], ret_theta_bwd[l])
    row = lambda a: a.reshape(1, -1)
    p = dict(
        win=w_in[l].astype(bf16), glnw=row(gmlp_ln_w[l]), glnb=row(gmlp_ln_b[l]),
        wsp=w_spatial[l].astype(bf16), bsp=jnp.repeat(b_spatial[l].T, HD, axis=1),
        tabs_a=tabs_a, tabs_b=tabs_b, gnw=row(ret_gn_w[l]),
        wout=w_out[l].astype(bf16), ln1w=row(ln1_w[l]), ln1b=row(ln1_b[l]),
        wr=_hi_lo(jnp.pad(w_router[l], ((0, 0), (0, LANES - E)))), br=jnp.pad(row(b_router[l]), ((0, 0), (0, LANES - E))),
        wgu=w_gate_up[l].astype(bf16), bgu=b_gate_up[l].reshape(E, 1, 2 * FF),
        wd=w_down[l].astype(bf16), bd=b_down[l].reshape(E, 1, D),
        ln2w=row(ln2_w[l]), ln2b=row(ln2_b[l]),
    )
    y_prompt = _trunk(x_prompt, mod[:bp], p)
    y_sample = _trunk(x_sample, mod[bp:bp + bs], p)
    return (y_prompt, y_sample)
```

```python
import functools

import numpy as np
import jax
import jax.numpy as jnp
from jax import lax
from jax.experimental import pallas as pl
from jax.experimental.pallas import tpu as pltpu

f32 = jnp.float32
bf16 = jnp.bfloat16
i32 = jnp.int32

D = 1024
CHUNK = 128
HEADS = 8
HD = D // HEADS
N_SEG = 8
E = 32
TOPK = 4
FF = 1024
ROPE_BASE = 10000.0
SWIGLU_LIMIT = 7.0
SWIGLU_ALPHA = 1.702
LN_EPS = 1e-5
GN_EPS = 1e-6
DEPTH = 1
DN_ALPHA = (2 * DEPTH) ** 0.25

LANES = 128
SUBLANES = 8

MIX_TOKENS = 512
SORT_TOKENS = 512
UNIT = SUBLANES
SORT_ROWS = -(-(SORT_TOKENS * TOPK + E * (UNIT - 1)) // LANES) * LANES
MAX_UNITS = SORT_ROWS // UNIT
ROUTE_TILES = 4
EXPERT_ROWS = 512
VMEM_LIMIT = 56 * 1024 * 1024

_HI = lax.Precision.HIGHEST


def _const_spec(shape):
    nd = len(shape)
    return pl.BlockSpec(shape, lambda *_: (0,) * nd, pipeline_mode=pl.Buffered(1))


def _gelu(x):
    return 0.5 * x * (1.0 + lax.erf(x * (2.0 ** -0.5)))


def _normalize(x, eps):
    mu = jnp.mean(x, axis=-1, keepdims=True)
    xc = x - mu
    var = jnp.mean(xc * xc, axis=-1, keepdims=True)
    return xc * lax.rsqrt(var + eps)


def _silu(x):
    return x * jax.nn.sigmoid(x)


def _head(h):
    return slice(h * HD, (h + 1) * HD)


def _adaln_body(c_ref, w_ref, b_ref, o_ref):
    s = _silu(c_ref[...])
    o_ref[...] = jnp.dot(s, w_ref[...], preferred_element_type=f32, precision=_HI) + b_ref[...]


def _adaln(c, w_ada, b_ada):
    rows = c.shape[0]
    return pl.pallas_call(
        _adaln_body,
        grid=(6,),
        in_specs=[pl.BlockSpec((rows, D), lambda j: (0, 0)),
                  pl.BlockSpec((D, D), lambda j: (0, j)),
                  pl.BlockSpec((1, D), lambda j: (0, j))],
        out_specs=pl.BlockSpec((rows, D), lambda j: (0, j)),
        out_shape=jax.ShapeDtypeStruct((rows, 6 * D), f32),
        compiler_params=pltpu.CompilerParams(vmem_limit_bytes=VMEM_LIMIT),
        name="adaln",
    )(c, w_ada, b_ada.reshape(1, 6 * D))


def _mixer_a_body(nck, x_ref, mod_ref, win_ref, glnw_ref, glnb_ref, wsp_ref, bsp_ref,
                  cos_ref, sin_ref, mtab_ref, zf_ref, zb_ref, xib_ref, cdb_ref, gnw_ref,
                  p1_ref, cr_ref, yp_ref, qr_ref, kvf_ref, sb_ref):
    @pl.when(pl.program_id(1) == 0)
    def _():
        sb_ref[...] = jnp.zeros_like(sb_ref)

    h = (x_ref[...] * (1.0 + mod_ref[1:2, :]) + mod_ref[0:1, :]).astype(bf16)

    def seg(j):
        return jnp.dot(h, win_ref[:, j * D:(j + 1) * D], preferred_element_type=f32)

    vn = _normalize(_gelu(seg(1)), LN_EPS) * glnw_ref[...] + glnb_ref[...]
    vnb = vn.astype(bf16)
    ug = _gelu(seg(0))
    ga = jax.nn.sigmoid(seg(6))
    for c in range(nck):
        rows = slice(c * CHUNK, (c + 1) * CHUNK)
        mixed = jnp.concatenate(
            [jnp.dot(wsp_ref[g], vnb[rows, _head(g)], preferred_element_type=f32) for g in range(HEADS)], axis=1)
        p1_ref[rows, :] = (ga[rows, :] * (ug[rows, :] * (mixed + bsp_ref[...]))).astype(bf16)

    q = seg(2)
    k = seg(3)
    v = seg(4).astype(bf16)
    cos, sin = cos_ref[...], sin_ref[...]

    def rotary(t):
        return jnp.concatenate([t[:, _head(h_)] * cos + pltpu.roll(t[:, _head(h_)], HD // 2, 1) * sin
                                for h_ in range(HEADS)], axis=1)

    qf = rotary(q)
    kr = rotary(k)
    qr = qf.astype(bf16)
    qr_ref[...] = qr
    for c in reversed(range(nck)):
        rows = slice(c * CHUNK, (c + 1) * CHUNK)
        kc = kr[rows, :]
        kcb = kc.astype(bf16)
        kzf = (kc * zf_ref[...]).astype(bf16)
        kzb = (kc * zb_ref[...]).astype(bf16)
        qx = (qf[rows, :] * xib_ref[...]).astype(bf16)
        sb = sb_ref[...]
        sbb = sb.astype(bf16)
        yps, kvfs, kvbs = [], [], []
        for h_ in range(HEADS):
            hs = _head(h_)
            vh = v[rows, hs]
            sc = lax.dot_general(qr[rows, hs], kcb[:, hs], (((1,), (1,)), ((), ())), preferred_element_type=f32)
            sc = (sc * mtab_ref[:, hs]).astype(bf16)
            yps.append(jnp.dot(sc, vh, preferred_element_type=f32)
                       + jnp.dot(qx[:, hs], sbb[:, hs], preferred_element_type=f32))
            kvfs.append(lax.dot_general(kzf[:, hs], vh, (((0,), (0,)), ((), ())), preferred_element_type=f32))
            kvbs.append(lax.dot_general(kzb[:, hs], vh, (((0,), (0,)), ((), ())), preferred_element_type=f32))
        yp_ref[rows, :] = jnp.concatenate(yps, axis=1).astype(bf16)
        kvf_ref[c] = jnp.concatenate(kvfs, axis=1)
        sb_ref[...] = sb * cdb_ref[...] + jnp.concatenate(kvbs, axis=1)

    cr_ref[...] = (jax.nn.sigmoid(seg(7)) * _silu(seg(5)) * gnw_ref[...]).astype(bf16)


def _mixer_a(x, mod, win, glnw, glnb, wsp, bsp, rope, tabs, gnw):
    b, s, _ = x.shape
    t = MIX_TOKENS
    nt, nck = s // t, t // CHUNK
    cos, sin = rope
    mtab, zf, zb, xib, cdb = tabs
    rev = lambda bi, i: (bi, nt - 1 - i, 0)
    rope_spec = pl.BlockSpec((t, HD), lambda bi, i: (nt - 1 - i, 0))
    tok = lambda dt: jax.ShapeDtypeStruct((b, s, D), dt)
    return pl.pallas_call(
        functools.partial(_mixer_a_body, nck),
        grid=(b, nt),
        in_specs=[pl.BlockSpec((None, t, D), rev),
                  pl.BlockSpec((None, 6, D), lambda bi, i: (bi, 0, 0)),
                  _const_spec((D, N_SEG * D)),
                  _const_spec((1, D)), _const_spec((1, D)),
                  _const_spec((HEADS, CHUNK, CHUNK)), _const_spec((CHUNK, D)),
                  rope_spec, rope_spec,
                  _const_spec((CHUNK, D)), _const_spec((CHUNK, D)), _const_spec((CHUNK, D)),
                  _const_spec((CHUNK, D)), _const_spec((1, D)), _const_spec((1, D))],
        out_specs=[pl.BlockSpec((None, t, D), rev)] * 4
                  + [pl.BlockSpec((None, nck, CHUNK, D), lambda bi, i: (bi, nt - 1 - i, 0, 0))],
        out_shape=[tok(bf16), tok(bf16), tok(bf16), tok(bf16),
                   jax.ShapeDtypeStruct((b, s // CHUNK, CHUNK, D), f32)],
        scratch_shapes=[pltpu.VMEM((CHUNK, D), f32)],
        compiler_params=pltpu.CompilerParams(dimension_semantics=("arbitrary", "arbitrary"),
                                             vmem_limit_bytes=VMEM_LIMIT),
        name="mixer_a",
    )(x, mod, win, glnw, glnb, wsp, bsp, cos, sin, mtab, zf, zb, xib, cdb, gnw)


def _mixer_b_body(nck, yp_ref, qr_ref, kvf_ref, p1_ref, cr_ref, x_ref, mod_ref, xif_ref, cdf_ref,
                  wout_ref, ln1w_ref, ln1b_ref, wr_ref, br_ref,
                  x1_ref, h2_ref, lg_ref, sf_ref):
    @pl.when(pl.program_id(1) == 0)
    def _():
        sf_ref[...] = jnp.zeros_like(sf_ref)

    ys = []
    for c in range(nck):
        rows = slice(c * CHUNK, (c + 1) * CHUNK)
        sf = sf_ref[...]
        sfb = sf.astype(bf16)
        cross = jnp.concatenate(
            [jnp.dot(qr_ref[rows, _head(h_)], sfb[:, _head(h_)], preferred_element_type=f32)
             for h_ in range(HEADS)], axis=1)
        ys.append(yp_ref[rows, :].astype(f32) + cross * xif_ref[...])
        sf_ref[...] = sf * cdf_ref[...] + kvf_ref[c]
    y = jnp.concatenate(ys, axis=0)
    yn = jnp.concatenate([_normalize(y[:, _head(h_)], GN_EPS) for h_ in range(HEADS)], axis=1)
    merged = p1_ref[...].astype(f32) + cr_ref[...].astype(f32) * yn
    mix = jnp.dot(merged.astype(bf16), wout_ref[...], preferred_element_type=f32)
    x1 = _normalize(DN_ALPHA * x_ref[...] + mod_ref[2:3, :] * mix, LN_EPS) * ln1w_ref[...] + ln1b_ref[...]
    x1_ref[...] = x1
    h2 = x1 * (1.0 + mod_ref[4:5, :]) + mod_ref[3:4, :]
    h2_hi = h2.astype(bf16)
    h2_ref[...] = h2_hi
    h2_lo = (h2 - h2_hi.astype(f32)).astype(bf16)
    hi_terms = jnp.dot(h2_hi, wr_ref[...], preferred_element_type=f32)
    lo_term = jnp.dot(h2_lo, wr_ref[:, :LANES], preferred_element_type=f32)
    lg_ref[...] = hi_terms[:, :LANES] + hi_terms[:, LANES:] + lo_term + br_ref[...]


def _mixer_b(yp, qr, kvf, p1, cr, x, mod, xif, cdf, wout, ln1w, ln1b, wr, br):
    b, s, _ = x.shape
    t = MIX_TOKENS
    nt, nck = s // t, t // CHUNK
    fwd = lambda bi, i: (bi, i, 0)
    flat = lambda bi, i: (bi * nt + i, 0)
    return pl.pallas_call(
        functools.partial(_mixer_b_body, nck),
        grid=(b, nt),
        in_specs=[pl.BlockSpec((None, t, D), fwd), pl.BlockSpec((None, t, D), fwd),
                  pl.BlockSpec((None, nck, CHUNK, D), lambda bi, i: (bi, i, 0, 0)),
                  pl.BlockSpec((None, t, D), fwd), pl.BlockSpec((None, t, D), fwd),
                  pl.BlockSpec((None, t, D), fwd),
                  pl.BlockSpec((None, 6, D), lambda bi, i: (bi, 0, 0)),
                  _const_spec((CHUNK, D)), _const_spec((1, D)),
                  _const_spec((D, D)), _const_spec((1, D)), _const_spec((1, D)),
                  _const_spec((D, 2 * LANES)), _const_spec((1, LANES))],
        out_specs=[pl.BlockSpec((None, t, D), fwd),
                   pl.BlockSpec((t, D), flat),
                   pl.BlockSpec((t, LANES), flat)],
        out_shape=[jax.ShapeDtypeStruct((b, s, D), f32),
                   jax.ShapeDtypeStruct((b * s, D), bf16),
                   jax.ShapeDtypeStruct((b * s, LANES), f32)],
        scratch_shapes=[pltpu.VMEM((CHUNK, D), f32)],
        compiler_params=pltpu.CompilerParams(dimension_semantics=("arbitrary", "arbitrary"),
                                             vmem_limit_bytes=VMEM_LIMIT),
        name="mixer_b",
    )(yp, qr, kvf, p1, cr, x, mod, xif, cdf, wout, ln1w, ln1b, wr, br)


def _route_body(lg_ref, idx_ref, w_ref, rank_ref, cnt_ref):
    tt = SORT_TOKENS
    for j in range(lg_ref.shape[0] // tt):
        cols = slice(j * tt, (j + 1) * tt)
        idx, w, rank, cnt = _route_tile(lg_ref[cols, :])
        idx_ref[:, cols] = idx
        w_ref[:, cols] = w
        rank_ref[:, cols] = rank
        cnt_ref[j] = cnt


def _route_tile(logits):
    tt = logits.shape[0]
    l = logits.T[:E, :]
    iota_e = lax.broadcasted_iota(i32, (E, tt), 0)
    vals, idxs, hots = [], [], []
    for _ in range(TOPK):
        m = jnp.max(l, axis=0, keepdims=True)
        idx = jnp.min(jnp.where(l == m, iota_e, E), axis=0, keepdims=True)
        hot = iota_e == idx
        l = jnp.where(hot, -jnp.inf, l)
        vals.append(m)
        idxs.append(idx)
        hots.append(hot)
    exps = [jnp.exp(v_ - vals[0]) for v_ in vals]
    tot = exps[0] + exps[1] + exps[2] + exps[3]
    w = jnp.concatenate([e_ / tot for e_ in exps], axis=0)

    sel = jnp.zeros((E, tt), f32)
    for hot in hots:
        sel = sel + hot.astype(f32)
    upper = (lax.broadcasted_iota(i32, (tt, tt), 0) < lax.broadcasted_iota(i32, (tt, tt), 1)).astype(bf16)
    rank_e = jnp.dot(sel.astype(bf16), upper, preferred_element_type=f32)
    ranks = [jnp.sum(jnp.where(hot, rank_e, 0.0), axis=0, keepdims=True) for hot in hots]
    cnt = jnp.broadcast_to(jnp.sum(sel, axis=1, keepdims=True), (E, LANES))
    return jnp.concatenate(idxs, axis=0), w, jnp.concatenate(ranks, axis=0).astype(i32), cnt


def _route(logits):
    n = logits.shape[0]
    tt = SORT_TOKENS * ROUTE_TILES
    col = lambda i: (0, i)
    return pl.pallas_call(
        _route_body,
        grid=(n // tt,),
        in_specs=[pl.BlockSpec((tt, LANES), lambda i: (i, 0))],
        out_specs=[pl.BlockSpec((TOPK, tt), col), pl.BlockSpec((TOPK, tt), col), pl.BlockSpec((TOPK, tt), col),
                   pl.BlockSpec((ROUTE_TILES, E, LANES), lambda i: (i, 0, 0))],
        out_shape=[jax.ShapeDtypeStruct((TOPK, n), i32), jax.ShapeDtypeStruct((TOPK, n), f32),
                   jax.ShapeDtypeStruct((TOPK, n), i32), jax.ShapeDtypeStruct((n // SORT_TOKENS, E, LANES), f32)],
        compiler_params=pltpu.CompilerParams(dimension_semantics=("arbitrary",), vmem_limit_bytes=VMEM_LIMIT),
        name="route",
    )(logits)


def _unit_copy(src, dst, sem):
    return pltpu.make_async_copy(src, dst, sem)


def _start_runs(tile, runs_ref, start):
    def per_run(e_, carry):
        k_ = (tile * E + e_) * 3
        units = runs_ref[k_]
        lrow = pl.multiple_of(runs_ref[k_ + 1], UNIT)
        srow = pl.multiple_of(runs_ref[k_ + 2], UNIT)

        def large(c, cc):
            off = pl.multiple_of(c * (4 * UNIT), 4 * UNIT)
            start(lrow + off, srow + off, 4 * UNIT, 0)
            return cc

        lax.fori_loop(0, units // 4, large, 0)
        off2 = pl.multiple_of((units // 4) * (4 * UNIT), 4 * UNIT)

        @pl.when(units & 2 != 0)
        def _():
            start(lrow + off2, srow + off2, 2 * UNIT, 1)

        off1 = pl.multiple_of(off2 + (units & 2) * UNIT, UNIT)

        @pl.when(units & 1 != 0)
        def _():
            start(lrow + off1, srow + off1, UNIT, 1)

        return carry

    lax.fori_loop(0, E, per_run, 0)


def _wait_units(n_units, wait_rows):
    for bit in reversed(range(MAX_UNITS.bit_length())):
        @pl.when((n_units >> bit) & 1 == 1)
        def _():
            wait_rows((1 << bit) * UNIT)


def _in_group(i, bound):
    return (i >= bound[0]) & (i < bound[1])


def _group_block(i, bound):
    return jnp.clip(i - bound[0], 0, bound[1] - bound[0] - 1)


def _dispatch_body(bounds, runs_ref, tunits_ref, zstart_ref, zunits_ref, lpos_ref, *refs):
    h2_refs = refs[:len(bounds)]
    xbuf_ref, xs_ref, zero_ref, sem, zsem = refs[len(bounds):]
    tt = SORT_TOKENS
    i = pl.program_id(0)
    last = pl.num_programs(0) - 1
    slot = i % 2

    def start(lrow, srow, rows, priority):
        _unit_copy(xs_ref.at[slot, pl.ds(pl.multiple_of(lrow, UNIT), rows)],
                   xbuf_ref.at[pl.ds(pl.multiple_of(srow, UNIT), rows)], sem.at[slot]).start(priority=priority)

    def drain(tile, slot_):
        _wait_units(tunits_ref[tile], lambda r: _unit_copy(
            xs_ref.at[slot_, pl.ds(0, r)], xbuf_ref.at[pl.ds(0, r)], sem.at[slot_]).wait())

    @pl.when(i >= 2)
    def _():
        drain(i - 2, slot)

    lpos = lpos_ref[...]
    rows = lax.broadcasted_iota(i32, (SORT_ROWS, tt), 0)
    perm = jnp.zeros((SORT_ROWS, tt), f32)
    for k_ in range(TOPK):
        perm = jnp.where(rows == lpos[k_:k_ + 1, :], 1.0, perm)
    perm = perm.astype(bf16)
    for h2_ref, bound in zip(h2_refs, bounds):
        @pl.when(_in_group(i, bound))
        def _():
            xs_ref[slot] = jnp.dot(perm, h2_ref[...], preferred_element_type=f32)

    _start_runs(i, runs_ref, start)

    @pl.when(i == last)
    def _():
        zero_ref[...] = jnp.zeros_like(zero_ref)

        def zero_copy(srow):
            return _unit_copy(zero_ref, xbuf_ref.at[pl.ds(srow, UNIT)], zsem)

        def fill(e_, carry):
            def start_one(u, c):
                zero_copy(pl.multiple_of(zstart_ref[e_] + u * UNIT, UNIT)).start()
                return c

            lax.fori_loop(0, zunits_ref[e_], start_one, 0)
            return carry

        def fill_wait(e_, carry):
            def wait_one(u, c):
                zero_copy(0).wait()
                return c

            lax.fori_loop(0, zunits_ref[e_], wait_one, 0)
            return carry

        lax.fori_loop(0, E + 1, fill, 0)
        lax.fori_loop(0, E + 1, fill_wait, 0)
        drain(i, slot)

        @pl.when(i >= 1)
        def _():
            drain(i - 1, 1 - slot)


def _tile_bounds(sizes):
    ends = np.cumsum([n // SORT_TOKENS for n in sizes])
    return tuple((int(e - n // SORT_TOKENS), int(e)) for e, n in zip(ends, sizes))


def _dispatch(h2s, lpos, runs, tunits, zstart, zunits, n_rows):
    tt = SORT_TOKENS
    bounds = _tile_bounds([h.shape[0] for h in h2s])
    return pl.pallas_call(
        functools.partial(_dispatch_body, bounds),
        grid_spec=pltpu.PrefetchScalarGridSpec(
            num_scalar_prefetch=4,
            grid=(bounds[-1][1],),
            in_specs=[pl.BlockSpec((TOPK, tt), lambda i, *_: (0, i))]
                     + [pl.BlockSpec((tt, D), lambda i, *_, bound=bound: (_group_block(i, bound), 0))
                        for bound in bounds],
            out_specs=pl.BlockSpec(memory_space=pl.ANY),
            scratch_shapes=[pltpu.VMEM((2, SORT_ROWS, D), f32), pltpu.VMEM((UNIT, D), f32),
                            pltpu.SemaphoreType.DMA((2,)), pltpu.SemaphoreType.DMA(())]),
        out_shape=jax.ShapeDtypeStruct((n_rows, D), f32),
        compiler_params=pltpu.CompilerParams(dimension_semantics=("arbitrary",), vmem_limit_bytes=VMEM_LIMIT),
        name="dispatch",
    )(runs, tunits, zstart, zunits, lpos, *h2s)


def _expert_body(be_ref, nused_ref, x_ref, wgu_ref, bgu_ref, wd_ref, bd_ref, y_ref):
    @pl.when(pl.program_id(0) >= nused_ref[0])
    def _():
        y_ref[...] = jnp.zeros_like(y_ref)

    @pl.when(pl.program_id(0) < nused_ref[0])
    def _():
        gu = jnp.dot(x_ref[...].astype(bf16), wgu_ref[...], preferred_element_type=f32) + bgu_ref[...]
        gate = jnp.minimum(gu[:, :FF], SWIGLU_LIMIT)
        up = jnp.clip(gu[:, FF:], -SWIGLU_LIMIT, SWIGLU_LIMIT)
        act = (up + 1.0) * gate * jax.nn.sigmoid(SWIGLU_ALPHA * gate)
        y_ref[...] = jnp.dot(act.astype(bf16), wd_ref[...], preferred_element_type=f32) + bd_ref[...]


def _experts(block_e, n_used, xbuf, wgu, bgu, wd, bd):
    n_rows = xbuf.shape[0]
    r = EXPERT_ROWS
    blk = lambda b_, nu: jnp.minimum(b_, nu[0] - 1)
    wspec = lambda shape: pl.BlockSpec((None,) + shape, lambda b_, be, nu: (be[blk(b_, nu)], 0, 0))
    return pl.pallas_call(
        _expert_body,
        grid_spec=pltpu.PrefetchScalarGridSpec(
            num_scalar_prefetch=2,
            grid=(n_rows // r,),
            in_specs=[pl.BlockSpec((r, D), lambda b_, be, nu: (blk(b_, nu), 0)),
                      wspec((D, 2 * FF)), wspec((1, 2 * FF)), wspec((FF, D)), wspec((1, D))],
            out_specs=pl.BlockSpec((r, D), lambda b_, be, nu: (b_, 0))),
        out_shape=jax.ShapeDtypeStruct((n_rows, D), f32),
        compiler_params=pltpu.CompilerParams(dimension_semantics=("arbitrary",), vmem_limit_bytes=VMEM_LIMIT),
        name="experts",
    )(block_e, n_used, xbuf, wgu, bgu, wd, bd)


def _combine_body(bounds, runs_ref, tunits_ref,
                  lpos_ref, w_ref, mod_ref, ln2w_ref, ln2b_ref, ybuf_ref, *refs):
    n_groups = len(bounds)
    x1_refs, o_refs = refs[:n_groups], refs[n_groups:2 * n_groups]
    ys_ref, sem = refs[2 * n_groups:]
    tt = SORT_TOKENS
    i = pl.program_id(0)
    slot = i % 2

    def fetch(tile, slot_):
        def start(lrow, srow, rows, priority):
            _unit_copy(ybuf_ref.at[pl.ds(pl.multiple_of(srow, UNIT), rows)],
                       ys_ref.at[slot_, pl.ds(pl.multiple_of(lrow, UNIT), rows)],
                       sem.at[slot_]).start(priority=priority)

        _start_runs(tile, runs_ref, start)

    @pl.when(i == 0)
    def _():
        ys_ref[...] = jnp.zeros_like(ys_ref)
        fetch(0, 0)

    @pl.when(i + 1 < pl.num_programs(0))
    def _():
        fetch(i + 1, 1 - slot)

    _wait_units(tunits_ref[i], lambda r: _unit_copy(
        ybuf_ref.at[pl.ds(0, r)], ys_ref.at[slot, pl.ds(0, r)], sem.at[slot]).wait())

    lpos, w = lpos_ref[...], w_ref[...]
    cols = lax.broadcasted_iota(i32, (tt, SORT_ROWS), 1)
    wm = jnp.zeros((tt, SORT_ROWS), f32)
    for k_ in range(TOPK):
        wm = jnp.where(cols == lpos[:, k_:k_ + 1], w[:, k_:k_ + 1], wm)
    gated = mod_ref[5:6, :] * jnp.dot(wm.astype(bf16), ys_ref[slot].astype(bf16), preferred_element_type=f32)
    for x1_ref, o_ref, bound in zip(x1_refs, o_refs, bounds):
        @pl.when(_in_group(i, bound))
        def _():
            o_ref[...] = _normalize(DN_ALPHA * x1_ref[...] + gated, LN_EPS) * ln2w_ref[...] + ln2b_ref[...]


def _combine(runs, tunits, lpos_t, w_t, mod, ln2w, ln2b, ybuf, x1s, seqs):
    tt = SORT_TOKENS
    bounds = _tile_bounds([x.shape[0] for x in x1s])
    tok = lambda width: pl.BlockSpec((tt, width), lambda i, *_: (i, 0))
    grp = [pl.BlockSpec((tt, D), lambda i, *_, bound=bound: (_group_block(i, bound), 0)) for bound in bounds]

    def mod_row(i, *_):
        row, base = 0, 0
        for bound, x, seq in zip(bounds, x1s, seqs):
            row = jnp.where(_in_group(i, bound), base + (i - bound[0]) * tt // seq, row)
            base += x.shape[0] // seq
        return (row, 0, 0)

    return pl.pallas_call(
        functools.partial(_combine_body, bounds),
        grid_spec=pltpu.PrefetchScalarGridSpec(
            num_scalar_prefetch=2,
            grid=(bounds[-1][1],),
            in_specs=[tok(TOPK), tok(TOPK), pl.BlockSpec((None, 6, D), mod_row),
                      _const_spec((1, D)), _const_spec((1, D)), pl.BlockSpec(memory_space=pl.ANY)] + grp,
            out_specs=grp,
            scratch_shapes=[pltpu.VMEM((2, SORT_ROWS, D), f32), pltpu.SemaphoreType.DMA((2,))]),
        out_shape=[jax.ShapeDtypeStruct(x.shape, f32) for x in x1s],
        compiler_params=pltpu.CompilerParams(dimension_semantics=("arbitrary",), vmem_limit_bytes=VMEM_LIMIT),
        name="combine",
    )(runs, tunits, lpos_t, w_t, mod, ln2w, ln2b, ybuf, *x1s)


def _rope_tables(s):
    half = HD // 2
    nf32 = np.float32
    inv_freq = np.power(nf32(ROPE_BASE), (nf32(-2.0) * np.arange(half, dtype=nf32) / nf32(HD)).astype(nf32)).astype(nf32)
    ang = (np.arange(s, dtype=nf32)[:, None] * inv_freq[None, :]).astype(nf32).astype(np.float64)
    cos, sin = np.cos(ang).astype(nf32), np.sin(ang).astype(nf32)
    return jnp.asarray(np.concatenate([cos, cos], axis=1)), jnp.asarray(np.concatenate([-sin, sin], axis=1))


def _retention_tables(theta_f, theta_b):
    lg_f = jax.nn.log_sigmoid(theta_f.astype(f32))
    lg_b = jax.nn.log_sigmoid(theta_b.astype(f32))
    idx = jnp.arange(CHUNK, dtype=f32)
    diff = idx[:, None] - idx[None, :]
    dec_f = jnp.where(diff >= 0, jnp.exp(jnp.maximum(diff, 0.0)[None] * lg_f[:, None, None]), 0.0)
    dec_b = jnp.where(diff < 0, jnp.exp(jnp.maximum(-diff, 0.0)[None] * lg_b[:, None, None]), 0.0)
    kscale = HD ** -0.5
    mtab = ((dec_f + dec_b) * kscale).transpose(1, 0, 2).reshape(CHUNK, D)
    lanes = lambda t: jnp.repeat(t.T, HD, axis=1)
    zf = lanes(jnp.exp((CHUNK - 1.0 - idx)[None, :] * lg_f[:, None]) * kscale)
    zb = lanes(jnp.exp(idx[None, :] * lg_b[:, None]) * kscale)
    xif = lanes(jnp.exp((idx + 1.0)[None, :] * lg_f[:, None]))
    xib = lanes(jnp.exp((CHUNK - idx)[None, :] * lg_b[:, None]))
    cdf = jnp.repeat(jnp.exp(CHUNK * lg_f), HD)[None, :]
    cdb = jnp.repeat(jnp.exp(CHUNK * lg_b), HD)[None, :]
    return (mtab, zf, zb, xib, cdb), (xif, cdf)


def _hi_lo(a):
    hi = a.astype(bf16)
    return jnp.concatenate([hi, (a - hi.astype(f32)).astype(bf16)], axis=1)


def _mixer(x, mod, p):
    p1, cr, yp, qr, kvf = _mixer_a(x, mod, p["win"], p["glnw"], p["glnb"], p["wsp"], p["bsp"],
                                   _rope_tables(x.shape[1]), p["tabs_a"], p["gnw"])
    xif, cdf = p["tabs_b"]
    return _mixer_b(yp, qr, kvf, p1, cr, x, mod, xif, cdf, p["wout"], p["ln1w"], p["ln1b"], p["wr"], p["br"])


def _moe_ln2(x1s, h2s, logits, mod, p):
    routed = [_route(lg) for lg in logits]
    idx, w, lrank = (jnp.concatenate([r[j] for r in routed], axis=1) for j in range(3))
    cnt = jnp.concatenate([r[3] for r in routed], axis=0)
    n = idx.shape[1]

    tt = SORT_TOKENS
    nt = n // tt
    counts = cnt[:, :, 0].astype(i32)
    run = (counts + UNIT - 1) // UNIT * UNIT
    loff = jnp.cumsum(run, axis=1) - run
    total = jnp.sum(run, axis=0)
    padded = (total + EXPERT_ROWS - 1) // EXPERT_ROWS * EXPERT_ROWS
    pends = jnp.cumsum(padded)
    pstarts = pends - padded
    seg = pstarts[None, :] + jnp.cumsum(run, axis=0) - run
    n_blocks = (n * TOPK + nt * E * (UNIT - 1)) // EXPERT_ROWS + E
    n_used = (pends[E - 1:] // EXPERT_ROWS).astype(i32)
    block_start = jnp.arange(n_blocks, dtype=i32) * EXPERT_ROWS
    block_e = jnp.minimum(jnp.sum((pends[None, :] <= block_start[:, None]).astype(i32), axis=1), E - 1)
    lpos = lrank.reshape(TOPK, nt, tt)
    idx3 = idx.reshape(TOPK, nt, tt)
    for e_ in range(E):
        lpos = lpos + jnp.where(idx3 == e_, loff[None, :, e_, None], 0)
    lpos = lpos.reshape(TOPK, n)
    flat = lambda a: a.reshape(-1).astype(i32)
    runs = flat(jnp.stack([run // UNIT, loff, seg], axis=-1))
    tunits = flat(jnp.sum(run, axis=1) // UNIT)

    n_rows = n_blocks * EXPERT_ROWS
    zstart = jnp.concatenate([pstarts + total, pends[E - 1:]])
    zunits = jnp.concatenate([padded - total, n_rows - pends[E - 1:]]) // UNIT
    xbuf = _dispatch(h2s, lpos, runs, tunits, flat(zstart), flat(zunits), n_rows)
    ybuf = _experts(block_e.astype(i32), n_used, xbuf, p["wgu"], p["bgu"], p["wd"], p["bd"])
    outs = _combine(runs, tunits, lpos.T, w.T, mod, p["ln2w"], p["ln2b"], ybuf,
                    [x.reshape(-1, D) for x in x1s], [x.shape[1] for x in x1s])
    return [o.reshape(x.shape) for o, x in zip(outs, x1s)]


def kernel(x_prompt, x_sample, c_prompt, c_sample, w_ada, b_ada, w_in, gmlp_ln_w, gmlp_ln_b, w_spatial, b_spatial, ret_theta_fwd, ret_theta_bwd, ret_gn_w, w_out, ln1_w, ln1_b, w_router, b_router, w_gate_up, b_gate_up, w_down, b_down, ln2_w, ln2_b):
    l = 0
    bp, bs = x_prompt.shape[0], x_sample.shape[0]
    c_all = jnp.concatenate([c_prompt, c_sample], axis=0)
    c_rows = -(-c_all.shape[0] // SUBLANES) * SUBLANES
    c_all = jnp.pad(c_all, ((0, c_rows - c_all.shape[0]), (0, 0)))
    mod = _adaln(c_all, w_ada[l], b_ada[l]).reshape(c_rows, 6, D)

    tabs_a, tabs_b = _retention_tables(ret_theta_fwd[l], ret_theta_bwd[l])
    row = lambda a: a.reshape(1, -1)
    p = dict(
        win=w_in[l].astype(bf16), glnw=row(gmlp_ln_w[l]), glnb=row(gmlp_ln_b[l]),
        wsp=w_spatial[l].astype(bf16), bsp=jnp.repeat(b_spatial[l].T, HD, axis=1),
        tabs_a=tabs_a, tabs_b=tabs_b, gnw=row(ret_gn_w[l]),
        wout=w_out[l].astype(bf16), ln1w=row(ln1_w[l]), ln1b=row(ln1_b[l]),
        wr=_hi_lo(jnp.pad(w_router[l], ((0, 0), (0, LANES - E)))), br=jnp.pad(row(b_router[l]), ((0, 0), (0, LANES - E))),
        wgu=w_gate_up[l].astype(bf16), bgu=b_gate_up[l].reshape(E, 1, 2 * FF),
        wd=w_down[l].astype(bf16), bd=b_down[l].reshape(E, 1, D),
        ln2w=row(ln2_w[l]), ln2b=row(ln2_b[l]),
    )
    mixed = [_mixer(x_prompt, mod[:bp], p), _mixer(x_sample, mod[bp:bp + bs], p)]
    x1s, h2s, logits = zip(*mixed)
    y_prompt, y_sample = _moe_ln2(x1s, h2s, logits, mod, p)
    return (y_prompt, y_sample)
```

```python
import functools

import numpy as np
import jax
import jax.numpy as jnp
from jax import lax
from jax.experimental import pallas as pl
from jax.experimental.pallas import tpu as pltpu

f32 = jnp.float32
bf16 = jnp.bfloat16
i32 = jnp.int32

D = 1024
CHUNK = 128
HEADS = 8
HD = D // HEADS
N_SEG = 8
E = 32
TOPK = 4
FF = 1024
ROPE_BASE = 10000.0
SWIGLU_LIMIT = 7.0
SWIGLU_ALPHA = 1.702
LN_EPS = 1e-5
GN_EPS = 1e-6
DEPTH = 1
DN_ALPHA = (2 * DEPTH) ** 0.25

LANES = 128
SUBLANES = 8

MIX_TOKENS = 512
SORT_TOKENS = 512
UNIT = SUBLANES
SORT_ROWS = -(-(SORT_TOKENS * TOPK + E * (UNIT - 1)) // LANES) * LANES
MAX_UNITS = SORT_ROWS // UNIT
ROUTE_TILES = 4
EXPERT_ROWS = 512
VMEM_LIMIT = 56 * 1024 * 1024

_HI = lax.Precision.HIGHEST


def _const_spec(shape):
    nd = len(shape)
    return pl.BlockSpec(shape, lambda *_: (0,) * nd, pipeline_mode=pl.Buffered(1))


def _gelu(x):
    return 0.5 * x * (1.0 + lax.erf(x * (2.0 ** -0.5)))


def _normalize(x, eps):
    mu = jnp.mean(x, axis=-1, keepdims=True)
    xc = x - mu
    var = jnp.mean(xc * xc, axis=-1, keepdims=True)
    return xc * lax.rsqrt(var + eps)


def _silu(x):
    return x * jax.nn.sigmoid(x)


def _head(h):
    return slice(h * HD, (h + 1) * HD)


def _adaln_body(c_ref, w_ref, b_ref, o_ref):
    s = _silu(c_ref[...])
    o_ref[...] = jnp.dot(s, w_ref[...], preferred_element_type=f32, precision=_HI) + b_ref[...]


def _adaln(c, w_ada, b_ada):
    rows = c.shape[0]
    return pl.pallas_call(
        _adaln_body,
        grid=(6,),
        in_specs=[pl.BlockSpec((rows, D), lambda j: (0, 0)),
                  pl.BlockSpec((D, D), lambda j: (0, j)),
                  pl.BlockSpec((1, D), lambda j: (0, j))],
        out_specs=pl.BlockSpec((rows, D), lambda j: (0, j)),
        out_shape=jax.ShapeDtypeStruct((rows, 6 * D), f32),
        compiler_params=pltpu.CompilerParams(vmem_limit_bytes=VMEM_LIMIT),
        name="adaln",
    )(c, w_ada, b_ada.reshape(1, 6 * D))


def _mixer_a_body(nck, x_ref, mod_ref, win_ref, glnw_ref, glnb_ref, wsp_ref, bsp_ref,
                  cos_ref, sin_ref, mtab_ref, zf_ref, zb_ref, xib_ref, cdb_ref, gnw_ref,
                  p1_ref, cr_ref, yp_ref, qr_ref, kvf_ref, sb_ref):
    @pl.when(pl.program_id(1) == 0)
    def _():
        sb_ref[...] = jnp.zeros_like(sb_ref)

    h = (x_ref[...] * (1.0 + mod_ref[1:2, :]) + mod_ref[0:1, :]).astype(bf16)

    def seg(j):
        return jnp.dot(h, win_ref[:, j * D:(j + 1) * D], preferred_element_type=f32)

    vn = _normalize(_gelu(seg(1)), LN_EPS) * glnw_ref[...] + glnb_ref[...]
    vnb = vn.astype(bf16)
    ug = _gelu(seg(0))
    ga = jax.nn.sigmoid(seg(6))
    for c in range(nck):
        rows = slice(c * CHUNK, (c + 1) * CHUNK)
        mixed = jnp.concatenate(
            [jnp.dot(wsp_ref[g], vnb[rows, _head(g)], preferred_element_type=f32) for g in range(HEADS)], axis=1)
        p1_ref[rows, :] = (ga[rows, :] * (ug[rows, :] * (mixed + bsp_ref[...]))).astype(bf16)

    q = seg(2)
    k = seg(3)
    v = seg(4).astype(bf16)
    cos, sin = cos_ref[...], sin_ref[...]

    def rotary(t):
        return jnp.concatenate([t[:, _head(h_)] * cos + pltpu.roll(t[:, _head(h_)], HD // 2, 1) * sin
                                for h_ in range(HEADS)], axis=1)

    qf = rotary(q)
    kr = rotary(k)
    qr = qf.astype(bf16)
    qr_ref[...] = qr
    for c in reversed(range(nck)):
        rows = slice(c * CHUNK, (c + 1) * CHUNK)
        kc = kr[rows, :]
        kcb = kc.astype(bf16)
        kzf = (kc * zf_ref[...]).astype(bf16)
        kzb = (kc * zb_ref[...]).astype(bf16)
        qx = (qf[rows, :] * xib_ref[...]).astype(bf16)
        sb = sb_ref[...]
        sbb = sb.astype(bf16)
        yps, kvfs, kvbs = [], [], []
        for h_ in range(HEADS):
            hs = _head(h_)
            vh = v[rows, hs]
            sc = lax.dot_general(qr[rows, hs], kcb[:, hs], (((1,), (1,)), ((), ())), preferred_element_type=f32)
            sc = (sc * mtab_ref[:, hs]).astype(bf16)
            yps.append(jnp.dot(sc, vh, preferred_element_type=f32)
                       + jnp.dot(qx[:, hs], sbb[:, hs], preferred_element_type=f32))
            kvfs.append(lax.dot_general(kzf[:, hs], vh, (((0,), (0,)), ((), ())), preferred_element_type=f32))
            kvbs.append(lax.dot_general(kzb[:, hs], vh, (((0,), (0,)), ((), ())), preferred_element_type=f32))
        yp_ref[rows, :] = jnp.concatenate(yps, axis=1).astype(bf16)
        kvf_ref[c] = jnp.concatenate(kvfs, axis=1)
        sb_ref[...] = sb * cdb_ref[...] + jnp.concatenate(kvbs, axis=1)

    cr_ref[...] = (jax.nn.sigmoid(seg(7)) * _silu(seg(5)) * gnw_ref[...]).astype(bf16)


def _mixer_a(x, mod, win, glnw, glnb, wsp, bsp, rope, tabs, gnw):
    b, s, _ = x.shape
    t = MIX_TOKENS
    nt, nck = s // t, t // CHUNK
    cos, sin = rope
    mtab, zf, zb, xib, cdb = tabs
    rev = lambda bi, i: (bi, nt - 1 - i, 0)
    rope_spec = pl.BlockSpec((t, HD), lambda bi, i: (nt - 1 - i, 0))
    tok = lambda dt: jax.ShapeDtypeStruct((b, s, D), dt)
    return pl.pallas_call(
        functools.partial(_mixer_a_body, nck),
        grid=(b, nt),
        in_specs=[pl.BlockSpec((None, t, D), rev),
                  pl.BlockSpec((None, 6, D), lambda bi, i: (bi, 0, 0)),
                  _const_spec((D, N_SEG * D)),
                  _const_spec((1, D)), _const_spec((1, D)),
                  _const_spec((HEADS, CHUNK, CHUNK)), _const_spec((CHUNK, D)),
                  rope_spec, rope_spec,
                  _const_spec((CHUNK, D)), _const_spec((CHUNK, D)), _const_spec((CHUNK, D)),
                  _const_spec((CHUNK, D)), _const_spec((1, D)), _const_spec((1, D))],
        out_specs=[pl.BlockSpec((None, t, D), rev)] * 4
                  + [pl.BlockSpec((None, nck, CHUNK, D), lambda bi, i: (bi, nt - 1 - i, 0, 0))],
        out_shape=[tok(bf16), tok(bf16), tok(bf16), tok(bf16),
                   jax.ShapeDtypeStruct((b, s // CHUNK, CHUNK, D), f32)],
        scratch_shapes=[pltpu.VMEM((CHUNK, D), f32)],
        compiler_params=pltpu.CompilerParams(dimension_semantics=("arbitrary", "arbitrary"),
                                             vmem_limit_bytes=VMEM_LIMIT),
        name="mixer_a",
    )(x, mod, win, glnw, glnb, wsp, bsp, cos, sin, mtab, zf, zb, xib, cdb, gnw)


def _mixer_b_body(nck, yp_ref, qr_ref, kvf_ref, p1_ref, cr_ref, x_ref, mod_ref, xif_ref, cdf_ref,
                  wout_ref, ln1w_ref, ln1b_ref, wr_ref, br_ref,
                  x1_ref, h2_ref, lg_ref, sf_ref):
    @pl.when(pl.program_id(1) == 0)
    def _():
        sf_ref[...] = jnp.zeros_like(sf_ref)

    ys = []
    for c in range(nck):
        rows = slice(c * CHUNK, (c + 1) * CHUNK)
        sf = sf_ref[...]
        sfb = sf.astype(bf16)
        cross = jnp.concatenate(
            [jnp.dot(qr_ref[rows, _head(h_)], sfb[:, _head(h_)], preferred_element_type=f32)
             for h_ in range(HEADS)], axis=1)
        ys.append(yp_ref[rows, :].astype(f32) + cross * xif_ref[...])
        sf_ref[...] = sf * cdf_ref[...] + kvf_ref[c]
    y = jnp.concatenate(ys, axis=0)
    yn = jnp.concatenate([_normalize(y[:, _head(h_)], GN_EPS) for h_ in range(HEADS)], axis=1)
    merged = p1_ref[...].astype(f32) + cr_ref[...].astype(f32) * yn
    mix = jnp.dot(merged.astype(bf16), wout_ref[...], preferred_element_type=f32)
    x1 = _normalize(DN_ALPHA * x_ref[...] + mod_ref[2:3, :] * mix, LN_EPS) * ln1w_ref[...] + ln1b_ref[...]
    x1_ref[...] = x1
    h2 = x1 * (1.0 + mod_ref[4:5, :]) + mod_ref[3:4, :]
    h2_hi = h2.astype(bf16)
    h2_ref[...] = h2_hi
    h2_lo = (h2 - h2_hi.astype(f32)).astype(bf16)
    hi_terms = jnp.dot(h2_hi, wr_ref[...], preferred_element_type=f32)
    lo_term = jnp.dot(h2_lo, wr_ref[:, :LANES], preferred_element_type=f32)
    lg_ref[...] = hi_terms[:, :LANES] + hi_terms[:, LANES:] + lo_term + br_ref[...]


def _mixer_b(yp, qr, kvf, p1, cr, x, mod, xif, cdf, wout, ln1w, ln1b, wr, br):
    b, s, _ = x.shape
    t = MIX_TOKENS
    nt, nck = s // t, t // CHUNK
    fwd = lambda bi, i: (bi, i, 0)
    flat = lambda bi, i: (bi * nt + i, 0)
    return pl.pallas_call(
        functools.partial(_mixer_b_body, nck),
        grid=(b, nt),
        in_specs=[pl.BlockSpec((None, t, D), fwd), pl.BlockSpec((None, t, D), fwd),
                  pl.BlockSpec((None, nck, CHUNK, D), lambda bi, i: (bi, i, 0, 0)),
                  pl.BlockSpec((None, t, D), fwd), pl.BlockSpec((None, t, D), fwd),
                  pl.BlockSpec((None, t, D), fwd),
                  pl.BlockSpec((None, 6, D), lambda bi, i: (bi, 0, 0)),
                  _const_spec((CHUNK, D)), _const_spec((1, D)),
                  _const_spec((D, D)), _const_spec((1, D)), _const_spec((1, D)),
                  _const_spec((D, 2 * LANES)), _const_spec((1, LANES))],
        out_specs=[pl.BlockSpec((None, t, D), fwd),
                   pl.BlockSpec((t, D), flat),
                   pl.BlockSpec((t, LANES), flat)],
        out_shape=[jax.ShapeDtypeStruct((b, s, D), f32),
                   jax.ShapeDtypeStruct((b * s, D), bf16),
                   jax.ShapeDtypeStruct((b * s, LANES), f32)],
        scratch_shapes=[pltpu.VMEM((CHUNK, D), f32)],
        compiler_params=pltpu.CompilerParams(dimension_semantics=("arbitrary", "arbitrary"),
                                             vmem_limit_bytes=VMEM_LIMIT),
        name="mixer_b",
    )(yp, qr, kvf, p1, cr, x, mod, xif, cdf, wout, ln1w, ln1b, wr, br)


def _route_body(lg_ref, idx_ref, w_ref, rank_ref, cnt_ref):
    tt = SORT_TOKENS
    for j in range(lg_ref.shape[0] // tt):
        cols = slice(j * tt, (j + 1) * tt)
        idx, w, rank, cnt = _route_tile(lg_ref[cols, :])
        idx_ref[:, cols] = idx
        w_ref[:, cols] = w
        rank_ref[:, cols] = rank
        cnt_ref[j] = cnt


def _route_tile(logits):
    tt = logits.shape[0]
    l = logits.T[:E, :]
    iota_e = lax.broadcasted_iota(i32, (E, tt), 0)
    vals, idxs, hots = [], [], []
    for _ in range(TOPK):
        m = jnp.max(l, axis=0, keepdims=True)
        idx = jnp.min(jnp.where(l == m, iota_e, E), axis=0, keepdims=True)
        hot = iota_e == idx
        l = jnp.where(hot, -jnp.inf, l)
        vals.append(m)
        idxs.append(idx)
        hots.append(hot)
    exps = [jnp.exp(v_ - vals[0]) for v_ in vals]
    tot = exps[0] + exps[1] + exps[2] + exps[3]
    w = jnp.concatenate([e_ / tot for e_ in exps], axis=0)

    sel = jnp.zeros((E, tt), f32)
    for hot in hots:
        sel = sel + hot.astype(f32)
    upper = (lax.broadcasted_iota(i32, (tt, tt), 0) < lax.broadcasted_iota(i32, (tt, tt), 1)).astype(bf16)
    rank_e = jnp.dot(sel.astype(bf16), upper, preferred_element_type=f32)
    ranks = [jnp.sum(jnp.where(hot, rank_e, 0.0), axis=0, keepdims=True) for hot in hots]
    cnt = jnp.broadcast_to(jnp.sum(sel, axis=1, keepdims=True), (E, LANES))
    return jnp.concatenate(idxs, axis=0), w, jnp.concatenate(ranks, axis=0).astype(i32), cnt


def _route(logits):
    n = logits.shape[0]
    tt = SORT_TOKENS * ROUTE_TILES
    col = lambda i: (0, i)
    return pl.pallas_call(
        _route_body,
        grid=(n // tt,),
        in_specs=[pl.BlockSpec((tt, LANES), lambda i: (i, 0))],
        out_specs=[pl.BlockSpec((TOPK, tt), col), pl.BlockSpec((TOPK, tt), col), pl.BlockSpec((TOPK, tt), col),
                   pl.BlockSpec((ROUTE_TILES, E, LANES), lambda i: (i, 0, 0))],
        out_shape=[jax.ShapeDtypeStruct((TOPK, n), i32), jax.ShapeDtypeStruct((TOPK, n), f32),
                   jax.ShapeDtypeStruct((TOPK, n), i32), jax.ShapeDtypeStruct((n // SORT_TOKENS, E, LANES), f32)],
        compiler_params=pltpu.CompilerParams(dimension_semantics=("arbitrary",), vmem_limit_bytes=VMEM_LIMIT),
        name="route",
    )(logits)


def _unit_copy(src, dst, sem):
    return pltpu.make_async_copy(src, dst, sem)


def _start_runs(tile, runs_ref, start):
    def per_run(e_, carry):
        k_ = (tile * E + e_) * 3
        units = runs_ref[k_]
        lrow = pl.multiple_of(runs_ref[k_ + 1], UNIT)
        srow = pl.multiple_of(runs_ref[k_ + 2], UNIT)

        def large(c, cc):
            off = pl.multiple_of(c * (4 * UNIT), 4 * UNIT)
            start(lrow + off, srow + off, 4 * UNIT, 0)
            return cc

        lax.fori_loop(0, units // 4, large, 0)
        off2 = pl.multiple_of((units // 4) * (4 * UNIT), 4 * UNIT)

        @pl.when(units & 2 != 0)
        def _():
            start(lrow + off2, srow + off2, 2 * UNIT, 1)

        off1 = pl.multiple_of(off2 + (units & 2) * UNIT, UNIT)

        @pl.when(units & 1 != 0)
        def _():
            start(lrow + off1, srow + off1, UNIT, 1)

        return carry

    lax.fori_loop(0, E, per_run, 0)


def _wait_units(n_units, wait_rows):
    for bit in reversed(range(MAX_UNITS.bit_length())):
        @pl.when((n_units >> bit) & 1 == 1)
        def _():
            wait_rows((1 << bit) * UNIT)


def _in_group(i, bound):
    return (i >= bound[0]) & (i < bound[1])


def _group_block(i, bound):
    return jnp.clip(i - bound[0], 0, bound[1] - bound[0] - 1)


def _group_value(i, bounds, refs):
    value = refs[-1][...]
    for ref, bound in zip(reversed(refs[:-1]), reversed(bounds[:-1])):
        value = jnp.where(_in_group(i, bound), ref[...], value)
    return value


def _dispatch_body(bounds, runs_ref, tunits_ref, zstart_ref, zunits_ref, lpos_ref, *refs):
    h2_refs = refs[:len(bounds)]
    xbuf_ref, xs_ref, zero_ref, sem, zsem = refs[len(bounds):]
    tt = SORT_TOKENS
    i = pl.program_id(0)
    last = pl.num_programs(0) - 1
    slot = i % 2

    def start(lrow, srow, rows, priority):
        _unit_copy(xs_ref.at[slot, pl.ds(pl.multiple_of(lrow, UNIT), rows)],
                   xbuf_ref.at[pl.ds(pl.multiple_of(srow, UNIT), rows)], sem.at[slot]).start(priority=priority)

    def drain(tile, slot_):
        _wait_units(tunits_ref[tile], lambda r: _unit_copy(
            xs_ref.at[slot_, pl.ds(0, r)], xbuf_ref.at[pl.ds(0, r)], sem.at[slot_]).wait())

    @pl.when(i >= 2)
    def _():
        drain(i - 2, slot)

    lpos = lpos_ref[...]
    rows = lax.broadcasted_iota(i32, (SORT_ROWS, tt), 0)
    perm = jnp.zeros((SORT_ROWS, tt), f32)
    for k_ in range(TOPK):
        perm = jnp.where(rows == lpos[k_:k_ + 1, :], 1.0, perm)
    xs_ref[slot] = jnp.dot(perm.astype(bf16), _group_value(i, bounds, h2_refs), preferred_element_type=f32)
    _start_runs(i, runs_ref, start)

    @pl.when(i == last)
    def _():
        zero_ref[...] = jnp.zeros_like(zero_ref)

        def zero_copy(srow):
            return _unit_copy(zero_ref, xbuf_ref.at[pl.ds(srow, UNIT)], zsem)

        def fill(e_, carry):
            def start_one(u, c):
                zero_copy(pl.multiple_of(zstart_ref[e_] + u * UNIT, UNIT)).start()
                return c

            lax.fori_loop(0, zunits_ref[e_], start_one, 0)
            return carry

        def fill_wait(e_, carry):
            def wait_one(u, c):
                zero_copy(0).wait()
                return c

            lax.fori_loop(0, zunits_ref[e_], wait_one, 0)
            return carry

        lax.fori_loop(0, E + 1, fill, 0)
        lax.fori_loop(0, E + 1, fill_wait, 0)
        drain(i, slot)

        @pl.when(i >= 1)
        def _():
            drain(i - 1, 1 - slot)


def _tile_bounds(sizes):
    ends = np.cumsum([n // SORT_TOKENS for n in sizes])
    return tuple((int(e - n // SORT_TOKENS), int(e)) for e, n in zip(ends, sizes))


def _dispatch(h2s, lpos, runs, tunits, zstart, zunits, n_rows):
    tt = SORT_TOKENS
    bounds = _tile_bounds([h.shape[0] for h in h2s])
    return pl.pallas_call(
        functools.partial(_dispatch_body, bounds),
        grid_spec=pltpu.PrefetchScalarGridSpec(
            num_scalar_prefetch=4,
            grid=(bounds[-1][1],),
            in_specs=[pl.BlockSpec((TOPK, tt), lambda i, *_: (0, i))]
                     + [pl.BlockSpec((tt, D), lambda i, *_, bound=bound: (_group_block(i, bound), 0))
                        for bound in bounds],
            out_specs=pl.BlockSpec(memory_space=pl.ANY),
            scratch_shapes=[pltpu.VMEM((2, SORT_ROWS, D), f32), pltpu.VMEM((UNIT, D), f32),
                            pltpu.SemaphoreType.DMA((2,)), pltpu.SemaphoreType.DMA(())]),
        out_shape=jax.ShapeDtypeStruct((n_rows, D), f32),
        compiler_params=pltpu.CompilerParams(dimension_semantics=("arbitrary",), vmem_limit_bytes=VMEM_LIMIT),
        name="dispatch",
    )(runs, tunits, zstart, zunits, lpos, *h2s)


def _expert_body(be_ref, nused_ref, x_ref, wgu_ref, bgu_ref, wd_ref, bd_ref, y_ref):
    @pl.when(pl.program_id(0) >= nused_ref[0])
    def _():
        y_ref[...] = jnp.zeros_like(y_ref)

    @pl.when(pl.program_id(0) < nused_ref[0])
    def _():
        gu = jnp.dot(x_ref[...].astype(bf16), wgu_ref[...], preferred_element_type=f32) + bgu_ref[...]
        gate = jnp.minimum(gu[:, :FF], SWIGLU_LIMIT)
        up = jnp.clip(gu[:, FF:], -SWIGLU_LIMIT, SWIGLU_LIMIT)
        act = (up + 1.0) * gate * jax.nn.sigmoid(SWIGLU_ALPHA * gate)
        y_ref[...] = jnp.dot(act.astype(bf16), wd_ref[...], preferred_element_type=f32) + bd_ref[...]


def _experts(block_e, n_used, xbuf, wgu, bgu, wd, bd):
    n_rows = xbuf.shape[0]
    r = EXPERT_ROWS
    blk = lambda b_, nu: jnp.minimum(b_, nu[0] - 1)
    wspec = lambda shape: pl.BlockSpec((None,) + shape, lambda b_, be, nu: (be[blk(b_, nu)], 0, 0))
    return pl.pallas_call(
        _expert_body,
        grid_spec=pltpu.PrefetchScalarGridSpec(
            num_scalar_prefetch=2,
            grid=(n_rows // r,),
            in_specs=[pl.BlockSpec((r, D), lambda b_, be, nu: (blk(b_, nu), 0)),
                      wspec((D, 2 * FF)), wspec((1, 2 * FF)), wspec((FF, D)), wspec((1, D))],
            out_specs=pl.BlockSpec((r, D), lambda b_, be, nu: (b_, 0))),
        out_shape=jax.ShapeDtypeStruct((n_rows, D), f32),
        compiler_params=pltpu.CompilerParams(dimension_semantics=("arbitrary",), vmem_limit_bytes=VMEM_LIMIT),
        name="experts",
    )(block_e, n_used, xbuf, wgu, bgu, wd, bd)


def _combine_body(bounds, runs_ref, tunits_ref,
                  lpos_ref, w_ref, mod_ref, ln2w_ref, ln2b_ref, ybuf_ref, *refs):
    n_groups = len(bounds)
    x1_refs, o_refs = refs[:n_groups], refs[n_groups:2 * n_groups]
    ys_ref, sem = refs[2 * n_groups:]
    tt = SORT_TOKENS
    i = pl.program_id(0)
    slot = i % 2

    def fetch(tile, slot_):
        def start(lrow, srow, rows, priority):
            _unit_copy(ybuf_ref.at[pl.ds(pl.multiple_of(srow, UNIT), rows)],
                       ys_ref.at[slot_, pl.ds(pl.multiple_of(lrow, UNIT), rows)],
                       sem.at[slot_]).start(priority=priority)

        _start_runs(tile, runs_ref, start)

    @pl.when(i == 0)
    def _():
        ys_ref[...] = jnp.zeros_like(ys_ref)
        fetch(0, 0)

    @pl.when(i + 1 < pl.num_programs(0))
    def _():
        fetch(i + 1, 1 - slot)

    _wait_units(tunits_ref[i], lambda r: _unit_copy(
        ybuf_ref.at[pl.ds(0, r)], ys_ref.at[slot, pl.ds(0, r)], sem.at[slot]).wait())

    lpos, w = lpos_ref[...], w_ref[...]
    cols = lax.broadcasted_iota(i32, (tt, SORT_ROWS), 1)
    wm = jnp.zeros((tt, SORT_ROWS), f32)
    for k_ in range(TOPK):
        wm = jnp.where(cols == lpos[:, k_:k_ + 1], w[:, k_:k_ + 1], wm)
    ff = jnp.dot(wm.astype(bf16), ys_ref[slot].astype(bf16), preferred_element_type=f32)
    z = DN_ALPHA * _group_value(i, bounds, x1_refs) + mod_ref[5:6, :] * ff
    out = _normalize(z, LN_EPS) * ln2w_ref[...] + ln2b_ref[...]
    for o_ref, bound in zip(o_refs, bounds):
        @pl.when(_in_group(i, bound))
        def _():
            o_ref[...] = out


def _combine(runs, tunits, lpos_t, w_t, mod, ln2w, ln2b, ybuf, x1s, seqs):
    tt = SORT_TOKENS
    bounds = _tile_bounds([x.shape[0] for x in x1s])
    tok = lambda width: pl.BlockSpec((tt, width), lambda i, *_: (i, 0))
    grp = [pl.BlockSpec((tt, D), lambda i, *_, bound=bound: (_group_block(i, bound), 0)) for bound in bounds]

    def mod_row(i, *_):
        row, base = 0, 0
        for bound, x, seq in zip(bounds, x1s, seqs):
            row = jnp.where(_in_group(i, bound), base + (i - bound[0]) * tt // seq, row)
            base += x.shape[0] // seq
        return (row, 0, 0)

    return pl.pallas_call(
        functools.partial(_combine_body, bounds),
        grid_spec=pltpu.PrefetchScalarGridSpec(
            num_scalar_prefetch=2,
            grid=(bounds[-1][1],),
            in_specs=[tok(TOPK), tok(TOPK), pl.BlockSpec((None, 6, D), mod_row),
                      _const_spec((1, D)), _const_spec((1, D)), pl.BlockSpec(memory_space=pl.ANY)] + grp,
            out_specs=grp,
            scratch_shapes=[pltpu.VMEM((2, SORT_ROWS, D), f32), pltpu.SemaphoreType.DMA((2,))]),
        out_shape=[jax.ShapeDtypeStruct(x.shape, f32) for x in x1s],
        compiler_params=pltpu.CompilerParams(dimension_semantics=("arbitrary",), vmem_limit_bytes=VMEM_LIMIT),
        name="combine",
    )(runs, tunits, lpos_t, w_t, mod, ln2w, ln2b, ybuf, *x1s)


def _rope_tables(s):
    half = HD // 2
    nf32 = np.float32
    inv_freq = np.power(nf32(ROPE_BASE), (nf32(-2.0) * np.arange(half, dtype=nf32) / nf32(HD)).astype(nf32)).astype(nf32)
    ang = (np.arange(s, dtype=nf32)[:, None] * inv_freq[None, :]).astype(nf32).astype(np.float64)
    cos, sin = np.cos(ang).astype(nf32), np.sin(ang).astype(nf32)
    return jnp.asarray(np.concatenate([cos, cos], axis=1)), jnp.asarray(np.concatenate([-sin, sin], axis=1))


def _retention_tables(theta_f, theta_b):
    lg_f = jax.nn.log_sigmoid(theta_f.astype(f32))
    lg_b = jax.nn.log_sigmoid(theta_b.astype(f32))
    idx = jnp.arange(CHUNK, dtype=f32)
    diff = idx[:, None] - idx[None, :]
    dec_f = jnp.where(diff >= 0, jnp.exp(jnp.maximum(diff, 0.0)[None] * lg_f[:, None, None]), 0.0)
    dec_b = jnp.where(diff < 0, jnp.exp(jnp.maximum(-diff, 0.0)[None] * lg_b[:, None, None]), 0.0)
    kscale = HD ** -0.5
    mtab = ((dec_f + dec_b) * kscale).transpose(1, 0, 2).reshape(CHUNK, D)
    lanes = lambda t: jnp.repeat(t.T, HD, axis=1)
    zf = lanes(jnp.exp((CHUNK - 1.0 - idx)[None, :] * lg_f[:, None]) * kscale)
    zb = lanes(jnp.exp(idx[None, :] * lg_b[:, None]) * kscale)
    xif = lanes(jnp.exp((idx + 1.0)[None, :] * lg_f[:, None]))
    xib = lanes(jnp.exp((CHUNK - idx)[None, :] * lg_b[:, None]))
    cdf = jnp.repeat(jnp.exp(CHUNK * lg_f), HD)[None, :]
    cdb = jnp.repeat(jnp.exp(CHUNK * lg_b), HD)[None, :]
    return (mtab, zf, zb, xib, cdb), (xif, cdf)


def _hi_lo(a):
    hi = a.astype(bf16)
    return jnp.concatenate([hi, (a - hi.astype(f32)).astype(bf16)], axis=1)


def _mixer(x, mod, p):
    p1, cr, yp, qr, kvf = _mixer_a(x, mod, p["win"], p["glnw"], p["glnb"], p["wsp"], p["bsp"],
                                   _rope_tables(x.shape[1]), p["tabs_a"], p["gnw"])
    xif, cdf = p["tabs_b"]
    return _mixer_b(yp, qr, kvf, p1, cr, x, mod, xif, cdf, p["wout"], p["ln1w"], p["ln1b"], p["wr"], p["br"])


def _moe_ln2(x1s, h2s, logits, mod, p):
    routed = [_route(lg) for lg in logits]
    idx, w, lrank = (jnp.concatenate([r[j] for r in routed], axis=1) for j in range(3))
    cnt = jnp.concatenate([r[3] for r in routed], axis=0)
    n = idx.shape[1]

    tt = SORT_TOKENS
    nt = n // tt
    counts = cnt[:, :, 0].astype(i32)
    run = (counts + UNIT - 1) // UNIT * UNIT
    loff = jnp.cumsum(run, axis=1) - run
    total = jnp.sum(run, axis=0)
    padded = (total + EXPERT_ROWS - 1) // EXPERT_ROWS * EXPERT_ROWS
    pends = jnp.cumsum(padded)
    pstarts = pends - padded
    seg = pstarts[None, :] + jnp.cumsum(run, axis=0) - run
    n_blocks = (n * TOPK + nt * E * (UNIT - 1)) // EXPERT_ROWS + E
    n_used = (pends[E - 1:] // EXPERT_ROWS).astype(i32)
    block_start = jnp.arange(n_blocks, dtype=i32) * EXPERT_ROWS
    block_e = jnp.minimum(jnp.sum((pends[None, :] <= block_start[:, None]).astype(i32), axis=1), E - 1)
    lpos = lrank.reshape(TOPK, nt, tt)
    idx3 = idx.reshape(TOPK, nt, tt)
    for e_ in range(E):
        lpos = lpos + jnp.where(idx3 == e_, loff[None, :, e_, None], 0)
    lpos = lpos.reshape(TOPK, n)
    flat = lambda a: a.reshape(-1).astype(i32)
    runs = flat(jnp.stack([run // UNIT, loff, seg], axis=-1))
    tunits = flat(jnp.sum(run, axis=1) // UNIT)

    n_rows = n_blocks * EXPERT_ROWS
    zstart = jnp.concatenate([pstarts + total, pends[E - 1:]])
    zunits = jnp.concatenate([padded - total, n_rows - pends[E - 1:]]) // UNIT
    xbuf = _dispatch(h2s, lpos, runs, tunits, flat(zstart), flat(zunits), n_rows)
    ybuf = _experts(block_e.astype(i32), n_used, xbuf, p["wgu"], p["bgu"], p["wd"], p["bd"])
    outs = _combine(runs, tunits, lpos.T, w.T, mod, p["ln2w"], p["ln2b"], ybuf,
                    [x.reshape(-1, D) for x in x1s], [x.shape[1] for x in x1s])
    return [o.reshape(x.shape) for o, x in zip(outs, x1s)]


def kernel(x_prompt, x_sample, c_prompt, c_sample, w_ada, b_ada, w_in, gmlp_ln_w, gmlp_ln_b, w_spatial, b_spatial, ret_theta_fwd, ret_theta_bwd, ret_gn_w, w_out, ln1_w, ln1_b, w_router, b_router, w_gate_up, b_gate_up, w_down, b_down, ln2_w, ln2_b):
    l = 0
    bp, bs = x_prompt.shape[0], x_sample.shape[0]
    c_all = jnp.concatenate([c_prompt, c_sample], axis=0)
    c_rows = -(-c_all.shape[0] // SUBLANES) * SUBLANES
    c_all = jnp.pad(c_all, ((0, c_rows - c_all.shape[0]), (0, 0)))
    mod = _adaln(c_all, w_ada[l], b_ada[l]).reshape(c_rows, 6, D)

    tabs_a, tabs_b = _retention_tables(ret_theta_fwd[l], ret_theta_bwd[l])
    row = lambda a: a.reshape(1, -1)
    p = dict(
        win=w_in[l].astype(bf16), glnw=row(gmlp_ln_w[l]), glnb=row(gmlp_ln_b[l]),
        wsp=w_spatial[l].astype(bf16), bsp=jnp.repeat(b_spatial[l].T, HD, axis=1),
        tabs_a=tabs_a, tabs_b=tabs_b, gnw=row(ret_gn_w[l]),
        wout=w_out[l].astype(bf16), ln1w=row(ln1_w[l]), ln1b=row(ln1_b[l]),
        wr=_hi_lo(jnp.pad(w_router[l], ((0, 0), (0, LANES - E)))), br=jnp.pad(row(b_router[l]), ((0, 0), (0, LANES - E))),
        wgu=w_gate_up[l].astype(bf16), bgu=b_gate_up[l].reshape(E, 1, 2 * FF),
        wd=w_down[l].astype(bf16), bd=b_down[l].reshape(E, 1, D),
        ln2w=row(ln2_w[l]), ln2b=row(ln2_b[l]),
    )
    mixed = [_mixer(x_prompt, mod[:bp], p), _mixer(x_sample, mod[bp:bp + bs], p)]
    x1s, h2s, logits = zip(*mixed)
    y_prompt, y_sample = _moe_ln2(x1s, h2s, logits, mod, p)
    return (y_prompt, y_sample)
```

```python
import functools

import numpy as np
import jax
import jax.numpy as jnp
from jax import lax
from jax.experimental import pallas as pl
from jax.experimental.pallas import tpu as pltpu

f32 = jnp.float32
bf16 = jnp.bfloat16
i32 = jnp.int32

D = 1024
CHUNK = 128
HEADS = 8
HD = D // HEADS
N_SEG = 8
E = 32
TOPK = 4
FF = 1024
ROPE_BASE = 10000.0
SWIGLU_LIMIT = 7.0
SWIGLU_ALPHA = 1.702
LN_EPS = 1e-5
GN_EPS = 1e-6
DEPTH = 1
DN_ALPHA = (2 * DEPTH) ** 0.25

LANES = 128
SUBLANES = 8

MIX_TOKENS = 512
SORT_TOKENS = 512
UNIT = SUBLANES
SORT_ROWS = -(-(SORT_TOKENS * TOPK + E * (UNIT - 1)) // LANES) * LANES
MAX_UNITS = SORT_ROWS // UNIT
ROUTE_TILES = 4
EXPERT_ROWS = 512
VMEM_LIMIT = 56 * 1024 * 1024

_HI = lax.Precision.HIGHEST


def _const_spec(shape):
    nd = len(shape)
    return pl.BlockSpec(shape, lambda *_: (0,) * nd, pipeline_mode=pl.Buffered(1))


def _gelu(x):
    return 0.5 * x * (1.0 + lax.erf(x * (2.0 ** -0.5)))


def _normalize(x, eps):
    mu = jnp.mean(x, axis=-1, keepdims=True)
    xc = x - mu
    var = jnp.mean(xc * xc, axis=-1, keepdims=True)
    return xc * lax.rsqrt(var + eps)


def _silu(x):
    return x * jax.nn.sigmoid(x)


def _head(h):
    return slice(h * HD, (h + 1) * HD)


def _adaln_body(c_ref, w_ref, b_ref, o_ref):
    s = _silu(c_ref[...])
    o_ref[...] = jnp.dot(s, w_ref[...], preferred_element_type=f32, precision=_HI) + b_ref[...]


def _adaln(c, w_ada, b_ada):
    rows = c.shape[0]
    return pl.pallas_call(
        _adaln_body,
        grid=(6,),
        in_specs=[pl.BlockSpec((rows, D), lambda j: (0, 0)),
                  pl.BlockSpec((D, D), lambda j: (0, j)),
                  pl.BlockSpec((1, D), lambda j: (0, j))],
        out_specs=pl.BlockSpec((rows, D), lambda j: (0, j)),
        out_shape=jax.ShapeDtypeStruct((rows, 6 * D), f32),
        compiler_params=pltpu.CompilerParams(vmem_limit_bytes=VMEM_LIMIT),
        name="adaln",
    )(c, w_ada, b_ada.reshape(1, 6 * D))


def _mixer_a_body(nck, x_ref, mod_ref, win_ref, glnw_ref, glnb_ref, wsp_ref, bsp_ref,
                  cos_ref, sin_ref, mtab_ref, zf_ref, zb_ref, xib_ref, cdb_ref, gnw_ref,
                  p1_ref, cr_ref, yp_ref, qr_ref, kvf_ref, sb_ref):
    @pl.when(pl.program_id(1) == 0)
    def _():
        sb_ref[...] = jnp.zeros_like(sb_ref)

    h = (x_ref[...] * (1.0 + mod_ref[1:2, :]) + mod_ref[0:1, :]).astype(bf16)

    def seg(j):
        return jnp.dot(h, win_ref[:, j * D:(j + 1) * D], preferred_element_type=f32)

    vn = _normalize(_gelu(seg(1)), LN_EPS) * glnw_ref[...] + glnb_ref[...]
    vnb = vn.astype(bf16)
    ug = _gelu(seg(0))
    ga = jax.nn.sigmoid(seg(6))
    for c in range(nck):
        rows = slice(c * CHUNK, (c + 1) * CHUNK)
        mixed = jnp.concatenate(
            [jnp.dot(wsp_ref[g], vnb[rows, _head(g)], preferred_element_type=f32) for g in range(HEADS)], axis=1)
        p1_ref[rows, :] = (ga[rows, :] * (ug[rows, :] * (mixed + bsp_ref[...]))).astype(bf16)

    q = seg(2)
    k = seg(3)
    v = seg(4).astype(bf16)
    cos, sin = cos_ref[...], sin_ref[...]

    def rotary(t):
        return jnp.concatenate([t[:, _head(h_)] * cos + pltpu.roll(t[:, _head(h_)], HD // 2, 1) * sin
                                for h_ in range(HEADS)], axis=1)

    qf = rotary(q)
    kr = rotary(k)
    qr = qf.astype(bf16)
    qr_ref[...] = qr
    for c in reversed(range(nck)):
        rows = slice(c * CHUNK, (c + 1) * CHUNK)
        kc = kr[rows, :]
        kcb = kc.astype(bf16)
        kzf = (kc * zf_ref[...]).astype(bf16)
        kzb = (kc * zb_ref[...]).astype(bf16)
        qx = (qf[rows, :] * xib_ref[...]).astype(bf16)
        sb = sb_ref[...]
        sbb = sb.astype(bf16)
        yps, kvfs, kvbs = [], [], []
        for h_ in range(HEADS):
            hs = _head(h_)
            vh = v[rows, hs]
            sc = lax.dot_general(qr[rows, hs], kcb[:, hs], (((1,), (1,)), ((), ())), preferred_element_type=f32)
            sc = (sc * mtab_ref[:, hs]).astype(bf16)
            yps.append(jnp.dot(sc, vh, preferred_element_type=f32)
                       + jnp.dot(qx[:, hs], sbb[:, hs], preferred_element_type=f32))
            kvfs.append(lax.dot_general(kzf[:, hs], vh, (((0,), (0,)), ((), ())), preferred_element_type=f32))
            kvbs.append(lax.dot_general(kzb[:, hs], vh, (((0,), (0,)), ((), ())), preferred_element_type=f32))
        yp_ref[rows, :] = jnp.concatenate(yps, axis=1).astype(bf16)
        kvf_ref[c] = jnp.concatenate(kvfs, axis=1)
        sb_ref[...] = sb * cdb_ref[...] + jnp.concatenate(kvbs, axis=1)

    cr_ref[...] = (jax.nn.sigmoid(seg(7)) * _silu(seg(5)) * gnw_ref[...]).astype(bf16)


def _mixer_a(x, mod, win, glnw, glnb, wsp, bsp, rope, tabs, gnw):
    b, s, _ = x.shape
    t = MIX_TOKENS
    nt, nck = s // t, t // CHUNK
    cos, sin = rope
    mtab, zf, zb, xib, cdb = tabs
    rev = lambda bi, i: (bi, nt - 1 - i, 0)
    rope_spec = pl.BlockSpec((t, HD), lambda bi, i: (nt - 1 - i, 0))
    tok = lambda dt: jax.ShapeDtypeStruct((b, s, D), dt)
    return pl.pallas_call(
        functools.partial(_mixer_a_body, nck),
        grid=(b, nt),
        in_specs=[pl.BlockSpec((None, t, D), rev),
                  pl.BlockSpec((None, 6, D), lambda bi, i: (bi, 0, 0)),
                  _const_spec((D, N_SEG * D)),
                  _const_spec((1, D)), _const_spec((1, D)),
                  _const_spec((HEADS, CHUNK, CHUNK)), _const_spec((CHUNK, D)),
                  rope_spec, rope_spec,
                  _const_spec((CHUNK, D)), _const_spec((CHUNK, D)), _const_spec((CHUNK, D)),
                  _const_spec((CHUNK, D)), _const_spec((1, D)), _const_spec((1, D))],
        out_specs=[pl.BlockSpec((None, t, D), rev)] * 4
                  + [pl.BlockSpec((None, nck, CHUNK, D), lambda bi, i: (bi, nt - 1 - i, 0, 0))],
        out_shape=[tok(bf16), tok(bf16), tok(bf16), tok(bf16),
                   jax.ShapeDtypeStruct((b, s // CHUNK, CHUNK, D), f32)],
        scratch_shapes=[pltpu.VMEM((CHUNK, D), f32)],
        compiler_params=pltpu.CompilerParams(dimension_semantics=("arbitrary", "arbitrary"),
                                             vmem_limit_bytes=VMEM_LIMIT),
        name="mixer_a",
    )(x, mod, win, glnw, glnb, wsp, bsp, cos, sin, mtab, zf, zb, xib, cdb, gnw)


def _mixer_b_body(nck, yp_ref, qr_ref, kvf_ref, p1_ref, cr_ref, x_ref, mod_ref, xif_ref, cdf_ref,
                  wout_ref, ln1w_ref, ln1b_ref, wr_ref, br_ref,
                  x1_ref, h2_ref, lg_ref, sf_ref):
    @pl.when(pl.program_id(1) == 0)
    def _():
        sf_ref[...] = jnp.zeros_like(sf_ref)

    ys = []
    for c in range(nck):
        rows = slice(c * CHUNK, (c + 1) * CHUNK)
        sf = sf_ref[...]
        sfb = sf.astype(bf16)
        cross = jnp.concatenate(
            [jnp.dot(qr_ref[rows, _head(h_)], sfb[:, _head(h_)], preferred_element_type=f32)
             for h_ in range(HEADS)], axis=1)
        ys.append(yp_ref[rows, :].astype(f32) + cross * xif_ref[...])
        sf_ref[...] = sf * cdf_ref[...] + kvf_ref[c]
    y = jnp.concatenate(ys, axis=0)
    yn = jnp.concatenate([_normalize(y[:, _head(h_)], GN_EPS) for h_ in range(HEADS)], axis=1)
    merged = p1_ref[...].astype(f32) + cr_ref[...].astype(f32) * yn
    mix = jnp.dot(merged.astype(bf16), wout_ref[...], preferred_element_type=f32)
    x1 = _normalize(DN_ALPHA * x_ref[...] + mod_ref[2:3, :] * mix, LN_EPS) * ln1w_ref[...] + ln1b_ref[...]
    x1_ref[...] = x1
    h2 = x1 * (1.0 + mod_ref[4:5, :]) + mod_ref[3:4, :]
    h2_hi = h2.astype(bf16)
    h2_ref[...] = h2_hi
    h2_lo = (h2 - h2_hi.astype(f32)).astype(bf16)
    hi_terms = jnp.dot(h2_hi, wr_ref[...], preferred_element_type=f32)
    lo_term = jnp.dot(h2_lo, wr_ref[:, :LANES], preferred_element_type=f32)
    lg_ref[...] = hi_terms[:, :LANES] + hi_terms[:, LANES:] + lo_term + br_ref[...]


def _mixer_b(yp, qr, kvf, p1, cr, x, mod, xif, cdf, wout, ln1w, ln1b, wr, br):
    b, s, _ = x.shape
    t = MIX_TOKENS
    nt, nck = s // t, t // CHUNK
    fwd = lambda bi, i: (bi, i, 0)
    flat = lambda bi, i: (bi * nt + i, 0)
    return pl.pallas_call(
        functools.partial(_mixer_b_body, nck),
        grid=(b, nt),
        in_specs=[pl.BlockSpec((None, t, D), fwd), pl.BlockSpec((None, t, D), fwd),
                  pl.BlockSpec((None, nck, CHUNK, D), lambda bi, i: (bi, i, 0, 0)),
                  pl.BlockSpec((None, t, D), fwd), pl.BlockSpec((None, t, D), fwd),
                  pl.BlockSpec((None, t, D), fwd),
                  pl.BlockSpec((None, 6, D), lambda bi, i: (bi, 0, 0)),
                  _const_spec((CHUNK, D)), _const_spec((1, D)),
                  _const_spec((D, D)), _const_spec((1, D)), _const_spec((1, D)),
                  _const_spec((D, 2 * LANES)), _const_spec((1, LANES))],
        out_specs=[pl.BlockSpec((None, t, D), fwd),
                   pl.BlockSpec((t, D), flat),
                   pl.BlockSpec((t, LANES), flat)],
        out_shape=[jax.ShapeDtypeStruct((b, s, D), f32),
                   jax.ShapeDtypeStruct((b * s, D), bf16),
                   jax.ShapeDtypeStruct((b * s, LANES), f32)],
        scratch_shapes=[pltpu.VMEM((CHUNK, D), f32)],
        compiler_params=pltpu.CompilerParams(dimension_semantics=("arbitrary", "arbitrary"),
                                             vmem_limit_bytes=VMEM_LIMIT),
        name="mixer_b",
    )(yp, qr, kvf, p1, cr, x, mod, xif, cdf, wout, ln1w, ln1b, wr, br)


def _route_body(lg_ref, idx_ref, w_ref, rank_ref, cnt_ref):
    tt = SORT_TOKENS
    for j in range(lg_ref.shape[0] // tt):
        cols = slice(j * tt, (j + 1) * tt)
        idx, w, rank, cnt = _route_tile(lg_ref[cols, :])
        idx_ref[:, cols] = idx
        w_ref[:, cols] = w
        rank_ref[:, cols] = rank
        cnt_ref[j] = cnt


def _route_tile(logits):
    tt = logits.shape[0]
    l = logits.T[:E, :]
    iota_e = lax.broadcasted_iota(i32, (E, tt), 0)
    vals, idxs, hots = [], [], []
    for _ in range(TOPK):
        m = jnp.max(l, axis=0, keepdims=True)
        idx = jnp.min(jnp.where(l == m, iota_e, E), axis=0, keepdims=True)
        hot = iota_e == idx
        l = jnp.where(hot, -jnp.inf, l)
        vals.append(m)
        idxs.append(idx)
        hots.append(hot)
    exps = [jnp.exp(v_ - vals[0]) for v_ in vals]
    tot = exps[0] + exps[1] + exps[2] + exps[3]
    w = jnp.concatenate([e_ / tot for e_ in exps], axis=0)

    sel = jnp.zeros((E, tt), f32)
    for hot in hots:
        sel = sel + hot.astype(f32)
    upper = (lax.broadcasted_iota(i32, (tt, tt), 0) < lax.broadcasted_iota(i32, (tt, tt), 1)).astype(bf16)
    rank_e = jnp.dot(sel.astype(bf16), upper, preferred_element_type=f32)
    ranks = [jnp.sum(jnp.where(hot, rank_e, 0.0), axis=0, keepdims=True) for hot in hots]
    cnt = jnp.broadcast_to(jnp.sum(sel, axis=1, keepdims=True), (E, LANES))
    return jnp.concatenate(idxs, axis=0), w, jnp.concatenate(ranks, axis=0).astype(i32), cnt


def _route(logits):
    n = logits.shape[0]
    tt = SORT_TOKENS * ROUTE_TILES
    col = lambda i: (0, i)
    return pl.pallas_call(
        _route_body,
        grid=(n // tt,),
        in_specs=[pl.BlockSpec((tt, LANES), lambda i: (i, 0))],
        out_specs=[pl.BlockSpec((TOPK, tt), col), pl.BlockSpec((TOPK, tt), col), pl.BlockSpec((TOPK, tt), col),
                   pl.BlockSpec((ROUTE_TILES, E, LANES), lambda i: (i, 0, 0))],
        out_shape=[jax.ShapeDtypeStruct((TOPK, n), i32), jax.ShapeDtypeStruct((TOPK, n), f32),
                   jax.ShapeDtypeStruct((TOPK, n), i32), jax.ShapeDtypeStruct((n // SORT_TOKENS, E, LANES), f32)],
        compiler_params=pltpu.CompilerParams(dimension_semantics=("arbitrary",), vmem_limit_bytes=VMEM_LIMIT),
        name="route",
    )(logits)


def _unit_copy(src, dst, sem):
    return pltpu.make_async_copy(src, dst, sem)


def _start_runs(tile, runs_ref, start):
    def per_run(e_, carry):
        k_ = (tile * E + e_) * 3
        units = runs_ref[k_]
        lrow = pl.multiple_of(runs_ref[k_ + 1], UNIT)
        srow = pl.multiple_of(runs_ref[k_ + 2], UNIT)

        def large(c, cc):
            off = pl.multiple_of(c * (4 * UNIT), 4 * UNIT)
            start(lrow + off, srow + off, 4 * UNIT, 0)
            return cc

        lax.fori_loop(0, units // 4, large, 0)
        off2 = pl.multiple_of((units // 4) * (4 * UNIT), 4 * UNIT)

        @pl.when(units & 2 != 0)
        def _():
            start(lrow + off2, srow + off2, 2 * UNIT, 1)

        off1 = pl.multiple_of(off2 + (units & 2) * UNIT, UNIT)

        @pl.when(units & 1 != 0)
        def _():
            start(lrow + off1, srow + off1, UNIT, 1)

        return carry

    lax.fori_loop(0, E, per_run, 0)


def _wait_units(n_units, wait_rows):
    for bit in reversed(range(MAX_UNITS.bit_length())):
        @pl.when((n_units >> bit) & 1 == 1)
        def _():
            wait_rows((1 << bit) * UNIT)


def _in_group(i, bound):
    return (i >= bound[0]) & (i < bound[1])


def _group_block(i, bound):
    return jnp.clip(i - bound[0], 0, bound[1] - bound[0] - 1)


def _group_value(i, bounds, refs):
    value = refs[-1][...]
    for ref, bound in zip(reversed(refs[:-1]), reversed(bounds[:-1])):
        value = jnp.where(_in_group(i, bound), ref[...], value)
    return value


def _dispatch_body(bounds, runs_ref, tunits_ref, zstart_ref, zunits_ref, lpos_ref, *refs):
    h2_refs = refs[:len(bounds)]
    xbuf_ref, xs_ref, zero_ref, sem, zsem = refs[len(bounds):]
    tt = SORT_TOKENS
    i = pl.program_id(0)
    last = pl.num_programs(0) - 1
    slot = i % 2

    def start(lrow, srow, rows, priority):
        _unit_copy(xs_ref.at[slot, pl.ds(pl.multiple_of(lrow, UNIT), rows)],
                   xbuf_ref.at[pl.ds(pl.multiple_of(srow, UNIT), rows)], sem.at[slot]).start(priority=priority)

    def drain(tile, slot_):
        _wait_units(tunits_ref[tile], lambda r: _unit_copy(
            xs_ref.at[slot_, pl.ds(0, r)], xbuf_ref.at[pl.ds(0, r)], sem.at[slot_]).wait())

    @pl.when(i >= 2)
    def _():
        drain(i - 2, slot)

    lpos = lpos_ref[...]
    rows = lax.broadcasted_iota(i32, (SORT_ROWS, tt), 0)
    perm = jnp.zeros((SORT_ROWS, tt), f32)
    for k_ in range(TOPK):
        perm = jnp.where(rows == lpos[k_:k_ + 1, :], 1.0, perm)
    xs_ref[slot] = jnp.dot(perm.astype(bf16), _group_value(i, bounds, h2_refs), preferred_element_type=f32)
    _start_runs(i, runs_ref, start)

    @pl.when(i == last)
    def _():
        zero_ref[...] = jnp.zeros_like(zero_ref)

        def zero_copy(srow):
            return _unit_copy(zero_ref, xbuf_ref.at[pl.ds(srow, UNIT)], zsem)

        def fill(e_, carry):
            def start_one(u, c):
                zero_copy(pl.multiple_of(zstart_ref[e_] + u * UNIT, UNIT)).start()
                return c

            lax.fori_loop(0, zunits_ref[e_], start_one, 0)
            return carry

        def fill_wait(e_, carry):
            def wait_one(u, c):
                zero_copy(0).wait()
                return c

            lax.fori_loop(0, zunits_ref[e_], wait_one, 0)
            return carry

        lax.fori_loop(0, E + 1, fill, 0)
        lax.fori_loop(0, E + 1, fill_wait, 0)
        drain(i, slot)

        @pl.when(i >= 1)
        def _():
            drain(i - 1, 1 - slot)


def _tile_bounds(sizes):
    ends = np.cumsum([n // SORT_TOKENS for n in sizes])
    return tuple((int(e - n // SORT_TOKENS), int(e)) for e, n in zip(ends, sizes))


def _dispatch(h2s, lpos, runs, tunits, zstart, zunits, n_rows):
    tt = SORT_TOKENS
    bounds = _tile_bounds([h.shape[0] for h in h2s])
    return pl.pallas_call(
        functools.partial(_dispatch_body, bounds),
        grid_spec=pltpu.PrefetchScalarGridSpec(
            num_scalar_prefetch=4,
            grid=(bounds[-1][1],),
            in_specs=[pl.BlockSpec((TOPK, tt), lambda i, *_: (0, i))]
                     + [pl.BlockSpec((tt, D), lambda i, *_, bound=bound: (_group_block(i, bound), 0))
                        for bound in bounds],
            out_specs=pl.BlockSpec(memory_space=pl.ANY),
            scratch_shapes=[pltpu.VMEM((2, SORT_ROWS, D), f32), pltpu.VMEM((UNIT, D), f32),
                            pltpu.SemaphoreType.DMA((2,)), pltpu.SemaphoreType.DMA(())]),
        out_shape=jax.ShapeDtypeStruct((n_rows, D), f32),
        compiler_params=pltpu.CompilerParams(dimension_semantics=("arbitrary",), vmem_limit_bytes=VMEM_LIMIT),
        name="dispatch",
    )(runs, tunits, zstart, zunits, lpos, *h2s)


def _expert_body(be_ref, nused_ref, x_ref, wgu_ref, bgu_ref, wd_ref, bd_ref, y_ref, wgu_bf_ref, wd_bf_ref):
    b_ = pl.program_id(0)

    @pl.when(b_ >= nused_ref[0])
    def _():
        y_ref[...] = jnp.zeros_like(y_ref)

    @pl.when((b_ == 0) | ((b_ < nused_ref[0]) & (be_ref[b_] != be_ref[jnp.maximum(b_ - 1, 0)])))
    def _():
        wgu_bf_ref[...] = wgu_ref[...].astype(bf16)
        wd_bf_ref[...] = wd_ref[...].astype(bf16)

    @pl.when(b_ < nused_ref[0])
    def _():
        gu = jnp.dot(x_ref[...].astype(bf16), wgu_bf_ref[...], preferred_element_type=f32) + bgu_ref[...]
        gate = jnp.minimum(gu[:, :FF], SWIGLU_LIMIT)
        up = jnp.clip(gu[:, FF:], -SWIGLU_LIMIT, SWIGLU_LIMIT)
        act = (up + 1.0) * gate * jax.nn.sigmoid(SWIGLU_ALPHA * gate)
        y_ref[...] = jnp.dot(act.astype(bf16), wd_bf_ref[...], preferred_element_type=f32) + bd_ref[...]


def _experts(block_e, n_used, xbuf, wgu, bgu, wd, bd):
    n_rows = xbuf.shape[0]
    r = EXPERT_ROWS
    blk = lambda b_, nu: jnp.minimum(b_, nu[0] - 1)
    wspec = lambda shape: pl.BlockSpec((None,) + shape, lambda b_, be, nu: (be[blk(b_, nu)], 0, 0))
    return pl.pallas_call(
        _expert_body,
        grid_spec=pltpu.PrefetchScalarGridSpec(
            num_scalar_prefetch=2,
            grid=(n_rows // r,),
            in_specs=[pl.BlockSpec((r, D), lambda b_, be, nu: (blk(b_, nu), 0)),
                      wspec((D, 2 * FF)), wspec((1, 2 * FF)), wspec((FF, D)), wspec((1, D))],
            out_specs=pl.BlockSpec((r, D), lambda b_, be, nu: (b_, 0)),
            scratch_shapes=[pltpu.VMEM((D, 2 * FF), bf16), pltpu.VMEM((FF, D), bf16)]),
        out_shape=jax.ShapeDtypeStruct((n_rows, D), f32),
        compiler_params=pltpu.CompilerParams(dimension_semantics=("arbitrary",), vmem_limit_bytes=VMEM_LIMIT),
        name="experts",
    )(block_e, n_used, xbuf, wgu, bgu, wd, bd)


def _combine_body(bounds, runs_ref, tunits_ref,
                  lpos_ref, w_ref, mod_ref, ln2w_ref, ln2b_ref, ybuf_ref, *refs):
    n_groups = len(bounds)
    x1_refs, o_refs = refs[:n_groups], refs[n_groups:2 * n_groups]
    ys_ref, sem = refs[2 * n_groups:]
    tt = SORT_TOKENS
    i = pl.program_id(0)
    slot = i % 2

    def fetch(tile, slot_):
        def start(lrow, srow, rows, priority):
            _unit_copy(ybuf_ref.at[pl.ds(pl.multiple_of(srow, UNIT), rows)],
                       ys_ref.at[slot_, pl.ds(pl.multiple_of(lrow, UNIT), rows)],
                       sem.at[slot_]).start(priority=priority)

        _start_runs(tile, runs_ref, start)

    @pl.when(i == 0)
    def _():
        ys_ref[...] = jnp.zeros_like(ys_ref)
        fetch(0, 0)

    @pl.when(i + 1 < pl.num_programs(0))
    def _():
        fetch(i + 1, 1 - slot)

    _wait_units(tunits_ref[i], lambda r: _unit_copy(
        ybuf_ref.at[pl.ds(0, r)], ys_ref.at[slot, pl.ds(0, r)], sem.at[slot]).wait())

    lpos, w = lpos_ref[...], w_ref[...]
    cols = lax.broadcasted_iota(i32, (tt, SORT_ROWS), 1)
    wm = jnp.zeros((tt, SORT_ROWS), f32)
    for k_ in range(TOPK):
        wm = jnp.where(cols == lpos[:, k_:k_ + 1], w[:, k_:k_ + 1], wm)
    ff = jnp.dot(wm.astype(bf16), ys_ref[slot].astype(bf16), preferred_element_type=f32)
    z = DN_ALPHA * _group_value(i, bounds, x1_refs) + mod_ref[5:6, :] * ff
    out = _normalize(z, LN_EPS) * ln2w_ref[...] + ln2b_ref[...]
    for o_ref, bound in zip(o_refs, bounds):
        @pl.when(_in_group(i, bound))
        def _():
            o_ref[...] = out


def _combine(runs, tunits, lpos_t, w_t, mod, ln2w, ln2b, ybuf, x1s, seqs):
    tt = SORT_TOKENS
    bounds = _tile_bounds([x.shape[0] for x in x1s])
    tok = lambda width: pl.BlockSpec((tt, width), lambda i, *_: (i, 0))
    grp = [pl.BlockSpec((tt, D), lambda i, *_, bound=bound: (_group_block(i, bound), 0)) for bound in bounds]

    def mod_row(i, *_):
        row, base = 0, 0
        for bound, x, seq in zip(bounds, x1s, seqs):
            row = jnp.where(_in_group(i, bound), base + (i - bound[0]) * tt // seq, row)
            base += x.shape[0] // seq
        return (row, 0, 0)

    return pl.pallas_call(
        functools.partial(_combine_body, bounds),
        grid_spec=pltpu.PrefetchScalarGridSpec(
            num_scalar_prefetch=2,
            grid=(bounds[-1][1],),
            in_specs=[tok(TOPK), tok(TOPK), pl.BlockSpec((None, 6, D), mod_row),
                      _const_spec((1, D)), _const_spec((1, D)), pl.BlockSpec(memory_space=pl.ANY)] + grp,
            out_specs=grp,
            scratch_shapes=[pltpu.VMEM((2, SORT_ROWS, D), f32), pltpu.SemaphoreType.DMA((2,))]),
        out_shape=[jax.ShapeDtypeStruct(x.shape, f32) for x in x1s],
        compiler_params=pltpu.CompilerParams(dimension_semantics=("arbitrary",), vmem_limit_bytes=VMEM_LIMIT),
        name="combine",
    )(runs, tunits, lpos_t, w_t, mod, ln2w, ln2b, ybuf, *x1s)


def _rope_tables(s):
    half = HD // 2
    nf32 = np.float32
    inv_freq = np.power(nf32(ROPE_BASE), (nf32(-2.0) * np.arange(half, dtype=nf32) / nf32(HD)).astype(nf32)).astype(nf32)
    ang = (np.arange(s, dtype=nf32)[:, None] * inv_freq[None, :]).astype(nf32).astype(np.float64)
    cos, sin = np.cos(ang).astype(nf32), np.sin(ang).astype(nf32)
    return jnp.asarray(np.concatenate([cos, cos], axis=1)), jnp.asarray(np.concatenate([-sin, sin], axis=1))


def _retention_tables(theta_f, theta_b):
    lg_f = jax.nn.log_sigmoid(theta_f.astype(f32))
    lg_b = jax.nn.log_sigmoid(theta_b.astype(f32))
    idx = jnp.arange(CHUNK, dtype=f32)
    diff = idx[:, None] - idx[None, :]
    dec_f = jnp.where(diff >= 0, jnp.exp(jnp.maximum(diff, 0.0)[None] * lg_f[:, None, None]), 0.0)
    dec_b = jnp.where(diff < 0, jnp.exp(jnp.maximum(-diff, 0.0)[None] * lg_b[:, None, None]), 0.0)
    kscale = HD ** -0.5
    mtab = ((dec_f + dec_b) * kscale).transpose(1, 0, 2).reshape(CHUNK, D)
    lanes = lambda t: jnp.repeat(t.T, HD, axis=1)
    zf = lanes(jnp.exp((CHUNK - 1.0 - idx)[None, :] * lg_f[:, None]) * kscale)
    zb = lanes(jnp.exp(idx[None, :] * lg_b[:, None]) * kscale)
    xif = lanes(jnp.exp((idx + 1.0)[None, :] * lg_f[:, None]))
    xib = lanes(jnp.exp((CHUNK - idx)[None, :] * lg_b[:, None]))
    cdf = jnp.repeat(jnp.exp(CHUNK * lg_f), HD)[None, :]
    cdb = jnp.repeat(jnp.exp(CHUNK * lg_b), HD)[None, :]
    return (mtab, zf, zb, xib, cdb), (xif, cdf)


def _hi_lo(a):
    hi = a.astype(bf16)
    return jnp.concatenate([hi, (a - hi.astype(f32)).astype(bf16)], axis=1)


def _mixer(x, mod, p):
    p1, cr, yp, qr, kvf = _mixer_a(x, mod, p["win"], p["glnw"], p["glnb"], p["wsp"], p["bsp"],
                                   _rope_tables(x.shape[1]), p["tabs_a"], p["gnw"])
    xif, cdf = p["tabs_b"]
    return _mixer_b(yp, qr, kvf, p1, cr, x, mod, xif, cdf, p["wout"], p["ln1w"], p["ln1b"], p["wr"], p["br"])


def _moe_ln2(x1s, h2s, logits, mod, p):
    routed = [_route(lg) for lg in logits]
    idx, w, lrank = (jnp.concatenate([r[j] for r in routed], axis=1) for j in range(3))
    cnt = jnp.concatenate([r[3] for r in routed], axis=0)
    n = idx.shape[1]

    tt = SORT_TOKENS
    nt = n // tt
    counts = cnt[:, :, 0].astype(i32)
    run = (counts + UNIT - 1) // UNIT * UNIT
    loff = jnp.cumsum(run, axis=1) - run
    total = jnp.sum(run, axis=0)
    padded = (total + EXPERT_ROWS - 1) // EXPERT_ROWS * EXPERT_ROWS
    pends = jnp.cumsum(padded)
    pstarts = pends - padded
    seg = pstarts[None, :] + jnp.cumsum(run, axis=0) - run
    n_blocks = (n * TOPK + nt * E * (UNIT - 1)) // EXPERT_ROWS + E
    n_used = (pends[E - 1:] // EXPERT_ROWS).astype(i32)
    block_start = jnp.arange(n_blocks, dtype=i32) * EXPERT_ROWS
    block_e = jnp.minimum(jnp.sum((pends[None, :] <= block_start[:, None]).astype(i32), axis=1), E - 1)
    lpos = lrank.reshape(TOPK, nt, tt)
    idx3 = idx.reshape(TOPK, nt, tt)
    for e_ in range(E):
        lpos = lpos + jnp.where(idx3 == e_, loff[None, :, e_, None], 0)
    lpos = lpos.reshape(TOPK, n)
    flat = lambda a: a.reshape(-1).astype(i32)
    runs = flat(jnp.stack([run // UNIT, loff, seg], axis=-1))
    tunits = flat(jnp.sum(run, axis=1) // UNIT)

    n_rows = n_blocks * EXPERT_ROWS
    zstart = jnp.concatenate([pstarts + total, pends[E - 1:]])
    zunits = jnp.concatenate([padded - total, n_rows - pends[E - 1:]]) // UNIT
    xbuf = _dispatch(h2s, lpos, runs, tunits, flat(zstart), flat(zunits), n_rows)
    ybuf = _experts(block_e.astype(i32), n_used, xbuf, p["wgu"], p["bgu"], p["wd"], p["bd"])
    outs = _combine(runs, tunits, lpos.T, w.T, mod, p["ln2w"], p["ln2b"], ybuf,
                    [x.reshape(-1, D) for x in x1s], [x.shape[1] for x in x1s])
    return [o.reshape(x.shape) for o, x in zip(outs, x1s)]


def kernel(x_prompt, x_sample, c_prompt, c_sample, w_ada, b_ada, w_in, gmlp_ln_w, gmlp_ln_b, w_spatial, b_spatial, ret_theta_fwd, ret_theta_bwd, ret_gn_w, w_out, ln1_w, ln1_b, w_router, b_router, w_gate_up, b_gate_up, w_down, b_down, ln2_w, ln2_b):
    l = 0
    bp, bs = x_prompt.shape[0], x_sample.shape[0]
    c_all = jnp.concatenate([c_prompt, c_sample], axis=0)
    c_rows = -(-c_all.shape[0] // SUBLANES) * SUBLANES
    c_all = jnp.pad(c_all, ((0, c_rows - c_all.shape[0]), (0, 0)))
    mod = _adaln(c_all, w_ada[l], b_ada[l]).reshape(c_rows, 6, D)

    tabs_a, tabs_b = _retention_tables(ret_theta_fwd[l], ret_theta_bwd[l])
    row = lambda a: a.reshape(1, -1)
    p = dict(
        win=w_in[l].astype(bf16), glnw=row(gmlp_ln_w[l]), glnb=row(gmlp_ln_b[l]),
        wsp=w_spatial[l].astype(bf16), bsp=jnp.repeat(b_spatial[l].T, HD, axis=1),
        tabs_a=tabs_a, tabs_b=tabs_b, gnw=row(ret_gn_w[l]),
        wout=w_out[l].astype(bf16), ln1w=row(ln1_w[l]), ln1b=row(ln1_b[l]),
        wr=_hi_lo(jnp.pad(w_router[l], ((0, 0), (0, LANES - E)))), br=jnp.pad(row(b_router[l]), ((0, 0), (0, LANES - E))),
        wgu=w_gate_up[l], bgu=b_gate_up[l].reshape(E, 1, 2 * FF),
        wd=w_down[l], bd=b_down[l].reshape(E, 1, D),
        ln2w=row(ln2_w[l]), ln2b=row(ln2_b[l]),
    )
    mixed = [_mixer(x_prompt, mod[:bp], p), _mixer(x_sample, mod[bp:bp + bs], p)]
    x1s, h2s, logits = zip(*mixed)
    y_prompt, y_sample = _moe_ln2(x1s, h2s, logits, mod, p)
    return (y_prompt, y_sample)
```

```python
import functools

import numpy as np
import jax
import jax.numpy as jnp
from jax import lax
from jax.experimental import pallas as pl
from jax.experimental.pallas import tpu as pltpu

f32 = jnp.float32
bf16 = jnp.bfloat16
i32 = jnp.int32

D = 1024
CHUNK = 128
HEADS = 8
HD = D // HEADS
N_SEG = 8
E = 32
TOPK = 4
FF = 1024
ROPE_BASE = 10000.0
SWIGLU_LIMIT = 7.0
SWIGLU_ALPHA = 1.702
LN_EPS = 1e-5
GN_EPS = 1e-6
DEPTH = 1
DN_ALPHA = (2 * DEPTH) ** 0.25

LANES = 128
SUBLANES = 8

MIX_TOKENS = 512
SORT_TOKENS = 256
UNIT = SUBLANES
SORT_ROWS = -(-(SORT_TOKENS * TOPK + E * (UNIT - 1)) // LANES) * LANES
MAX_UNITS = SORT_ROWS // UNIT
ROUTE_TILES = 4
EXPERT_ROWS = 512
VMEM_LIMIT = 56 * 1024 * 1024

_HI = lax.Precision.HIGHEST


def _const_spec(shape):
    nd = len(shape)
    return pl.BlockSpec(shape, lambda *_: (0,) * nd, pipeline_mode=pl.Buffered(1))


def _gelu(x):
    return 0.5 * x * (1.0 + lax.erf(x * (2.0 ** -0.5)))


def _normalize(x, eps):
    mu = jnp.mean(x, axis=-1, keepdims=True)
    xc = x - mu
    var = jnp.mean(xc * xc, axis=-1, keepdims=True)
    return xc * lax.rsqrt(var + eps)


def _silu(x):
    return x * jax.nn.sigmoid(x)


def _head(h):
    return slice(h * HD, (h + 1) * HD)


def _adaln_body(c_ref, w_ref, b_ref, o_ref):
    s = _silu(c_ref[...])
    o_ref[...] = jnp.dot(s, w_ref[...], preferred_element_type=f32, precision=_HI) + b_ref[...]


def _adaln(c, w_ada, b_ada):
    rows = c.shape[0]
    return pl.pallas_call(
        _adaln_body,
        grid=(6,),
        in_specs=[pl.BlockSpec((rows, D), lambda j: (0, 0)),
                  pl.BlockSpec((D, D), lambda j: (0, j)),
                  pl.BlockSpec((1, D), lambda j: (0, j))],
        out_specs=pl.BlockSpec((rows, D), lambda j: (0, j)),
        out_shape=jax.ShapeDtypeStruct((rows, 6 * D), f32),
        compiler_params=pltpu.CompilerParams(vmem_limit_bytes=VMEM_LIMIT),
        name="adaln",
    )(c, w_ada, b_ada.reshape(1, 6 * D))


def _mixer_a_body(nck, x_ref, mod_ref, win_ref, glnw_ref, glnb_ref, wsp_ref, bsp_ref,
                  cos_ref, sin_ref, mtab_ref, zf_ref, zb_ref, xib_ref, cdb_ref, gnw_ref,
                  p1_ref, cr_ref, yp_ref, qr_ref, kvf_ref, sb_ref):
    @pl.when(pl.program_id(1) == 0)
    def _():
        sb_ref[...] = jnp.zeros_like(sb_ref)

    h = (x_ref[...] * (1.0 + mod_ref[1:2, :]) + mod_ref[0:1, :]).astype(bf16)

    def seg(j):
        return jnp.dot(h, win_ref[:, j * D:(j + 1) * D], preferred_element_type=f32)

    vn = _normalize(_gelu(seg(1)), LN_EPS) * glnw_ref[...] + glnb_ref[...]
    vnb = vn.astype(bf16)
    ug = _gelu(seg(0))
    ga = jax.nn.sigmoid(seg(6))
    for c in range(nck):
        rows = slice(c * CHUNK, (c + 1) * CHUNK)
        mixed = jnp.concatenate(
            [jnp.dot(wsp_ref[g], vnb[rows, _head(g)], preferred_element_type=f32) for g in range(HEADS)], axis=1)
        p1_ref[rows, :] = (ga[rows, :] * (ug[rows, :] * (mixed + bsp_ref[...]))).astype(bf16)

    q = seg(2)
    k = seg(3)
    v = seg(4).astype(bf16)
    cos, sin = cos_ref[...], sin_ref[...]

    def rotary(t):
        return jnp.concatenate([t[:, _head(h_)] * cos + pltpu.roll(t[:, _head(h_)], HD // 2, 1) * sin
                                for h_ in range(HEADS)], axis=1)

    qf = rotary(q)
    kr = rotary(k)
    qr = qf.astype(bf16)
    qr_ref[...] = qr
    for c in reversed(range(nck)):
        rows = slice(c * CHUNK, (c + 1) * CHUNK)
        kc = kr[rows, :]
        kcb = kc.astype(bf16)
        kzf = (kc * zf_ref[...]).astype(bf16)
        kzb = (kc * zb_ref[...]).astype(bf16)
        qx = (qf[rows, :] * xib_ref[...]).astype(bf16)
        sb = sb_ref[...]
        sbb = sb.astype(bf16)
        yps, kvfs, kvbs = [], [], []
        for h_ in range(HEADS):
            hs = _head(h_)
            vh = v[rows, hs]
            sc = lax.dot_general(qr[rows, hs], kcb[:, hs], (((1,), (1,)), ((), ())), preferred_element_type=f32)
            sc = (sc * mtab_ref[:, hs]).astype(bf16)
            yps.append(jnp.dot(sc, vh, preferred_element_type=f32)
                       + jnp.dot(qx[:, hs], sbb[:, hs], preferred_element_type=f32))
            kvfs.append(lax.dot_general(kzf[:, hs], vh, (((0,), (0,)), ((), ())), preferred_element_type=f32))
            kvbs.append(lax.dot_general(kzb[:, hs], vh, (((0,), (0,)), ((), ())), preferred_element_type=f32))
        yp_ref[rows, :] = jnp.concatenate(yps, axis=1).astype(bf16)
        kvf_ref[c] = jnp.concatenate(kvfs, axis=1)
        sb_ref[...] = sb * cdb_ref[...] + jnp.concatenate(kvbs, axis=1)

    cr_ref[...] = (jax.nn.sigmoid(seg(7)) * _silu(seg(5)) * gnw_ref[...]).astype(bf16)


def _mixer_a(x, mod, win, glnw, glnb, wsp, bsp, rope, tabs, gnw):
    b, s, _ = x.shape
    t = MIX_TOKENS
    nt, nck = s // t, t // CHUNK
    cos, sin = rope
    mtab, zf, zb, xib, cdb = tabs
    rev = lambda bi, i: (bi, nt - 1 - i, 0)
    rope_spec = pl.BlockSpec((t, HD), lambda bi, i: (nt - 1 - i, 0))
    tok = lambda dt: jax.ShapeDtypeStruct((b, s, D), dt)
    return pl.pallas_call(
        functools.partial(_mixer_a_body, nck),
        grid=(b, nt),
        in_specs=[pl.BlockSpec((None, t, D), rev),
                  pl.BlockSpec((None, 6, D), lambda bi, i: (bi, 0, 0)),
                  _const_spec((D, N_SEG * D)),
                  _const_spec((1, D)), _const_spec((1, D)),
                  _const_spec((HEADS, CHUNK, CHUNK)), _const_spec((CHUNK, D)),
                  rope_spec, rope_spec,
                  _const_spec((CHUNK, D)), _const_spec((CHUNK, D)), _const_spec((CHUNK, D)),
                  _const_spec((CHUNK, D)), _const_spec((1, D)), _const_spec((1, D))],
        out_specs=[pl.BlockSpec((None, t, D), rev)] * 4
                  + [pl.BlockSpec((None, nck, CHUNK, D), lambda bi, i: (bi, nt - 1 - i, 0, 0))],
        out_shape=[tok(bf16), tok(bf16), tok(bf16), tok(bf16),
                   jax.ShapeDtypeStruct((b, s // CHUNK, CHUNK, D), f32)],
        scratch_shapes=[pltpu.VMEM((CHUNK, D), f32)],
        compiler_params=pltpu.CompilerParams(dimension_semantics=("arbitrary", "arbitrary"),
                                             vmem_limit_bytes=VMEM_LIMIT),
        name="mixer_a",
    )(x, mod, win, glnw, glnb, wsp, bsp, cos, sin, mtab, zf, zb, xib, cdb, gnw)


def _mixer_b_body(nck, yp_ref, qr_ref, kvf_ref, p1_ref, cr_ref, x_ref, mod_ref, xif_ref, cdf_ref,
                  wout_ref, ln1w_ref, ln1b_ref, wr_ref, br_ref,
                  x1_ref, h2_ref, lg_ref, sf_ref):
    @pl.when(pl.program_id(1) == 0)
    def _():
        sf_ref[...] = jnp.zeros_like(sf_ref)

    ys = []
    for c in range(nck):
        rows = slice(c * CHUNK, (c + 1) * CHUNK)
        sf = sf_ref[...]
        sfb = sf.astype(bf16)
        cross = jnp.concatenate(
            [jnp.dot(qr_ref[rows, _head(h_)], sfb[:, _head(h_)], preferred_element_type=f32)
             for h_ in range(HEADS)], axis=1)
        ys.append(yp_ref[rows, :].astype(f32) + cross * xif_ref[...])
        sf_ref[...] = sf * cdf_ref[...] + kvf_ref[c]
    y = jnp.concatenate(ys, axis=0)
    yn = jnp.concatenate([_normalize(y[:, _head(h_)], GN_EPS) for h_ in range(HEADS)], axis=1)
    merged = p1_ref[...].astype(f32) + cr_ref[...].astype(f32) * yn
    mix = jnp.dot(merged.astype(bf16), wout_ref[...], preferred_element_type=f32)
    x1 = _normalize(DN_ALPHA * x_ref[...] + mod_ref[2:3, :] * mix, LN_EPS) * ln1w_ref[...] + ln1b_ref[...]
    x1_ref[...] = x1
    h2 = x1 * (1.0 + mod_ref[4:5, :]) + mod_ref[3:4, :]
    h2_hi = h2.astype(bf16)
    h2_ref[...] = h2_hi
    h2_lo = (h2 - h2_hi.astype(f32)).astype(bf16)
    hi_terms = jnp.dot(h2_hi, wr_ref[...], preferred_element_type=f32)
    lo_term = jnp.dot(h2_lo, wr_ref[:, :LANES], preferred_element_type=f32)
    lg_ref[...] = hi_terms[:, :LANES] + hi_terms[:, LANES:] + lo_term + br_ref[...]


def _mixer_b(yp, qr, kvf, p1, cr, x, mod, xif, cdf, wout, ln1w, ln1b, wr, br):
    b, s, _ = x.shape
    t = MIX_TOKENS
    nt, nck = s // t, t // CHUNK
    fwd = lambda bi, i: (bi, i, 0)
    flat = lambda bi, i: (bi * nt + i, 0)
    return pl.pallas_call(
        functools.partial(_mixer_b_body, nck),
        grid=(b, nt),
        in_specs=[pl.BlockSpec((None, t, D), fwd), pl.BlockSpec((None, t, D), fwd),
                  pl.BlockSpec((None, nck, CHUNK, D), lambda bi, i: (bi, i, 0, 0)),
                  pl.BlockSpec((None, t, D), fwd), pl.BlockSpec((None, t, D), fwd),
                  pl.BlockSpec((None, t, D), fwd),
                  pl.BlockSpec((None, 6, D), lambda bi, i: (bi, 0, 0)),
                  _const_spec((CHUNK, D)), _const_spec((1, D)),
                  _const_spec((D, D)), _const_spec((1, D)), _const_spec((1, D)),
                  _const_spec((D, 2 * LANES)), _const_spec((1, LANES))],
        out_specs=[pl.BlockSpec((None, t, D), fwd),
                   pl.BlockSpec((t, D), flat),
                   pl.BlockSpec((t, LANES), flat)],
        out_shape=[jax.ShapeDtypeStruct((b, s, D), f32),
                   jax.ShapeDtypeStruct((b * s, D), bf16),
                   jax.ShapeDtypeStruct((b * s, LANES), f32)],
        scratch_shapes=[pltpu.VMEM((CHUNK, D), f32)],
        compiler_params=pltpu.CompilerParams(dimension_semantics=("arbitrary", "arbitrary"),
                                             vmem_limit_bytes=VMEM_LIMIT),
        name="mixer_b",
    )(yp, qr, kvf, p1, cr, x, mod, xif, cdf, wout, ln1w, ln1b, wr, br)


def _route_body(lg_ref, idx_ref, w_ref, rank_ref, cnt_ref):
    tt = SORT_TOKENS
    for j in range(lg_ref.shape[0] // tt):
        cols = slice(j * tt, (j + 1) * tt)
        idx, w, rank, cnt = _route_tile(lg_ref[cols, :])
        idx_ref[:, cols] = idx
        w_ref[:, cols] = w
        rank_ref[:, cols] = rank
        cnt_ref[j] = cnt


def _route_tile(logits):
    tt = logits.shape[0]
    l = logits.T[:E, :]
    iota_e = lax.broadcasted_iota(i32, (E, tt), 0)
    vals, idxs, hots = [], [], []
    for _ in range(TOPK):
        m = jnp.max(l, axis=0, keepdims=True)
        idx = jnp.min(jnp.where(l == m, iota_e, E), axis=0, keepdims=True)
        hot = iota_e == idx
        l = jnp.where(hot, -jnp.inf, l)
        vals.append(m)
        idxs.append(idx)
        hots.append(hot)
    exps = [jnp.exp(v_ - vals[0]) for v_ in vals]
    tot = exps[0] + exps[1] + exps[2] + exps[3]
    w = jnp.concatenate([e_ / tot for e_ in exps], axis=0)

    sel = jnp.zeros((E, tt), f32)
    for hot in hots:
        sel = sel + hot.astype(f32)
    upper = (lax.broadcasted_iota(i32, (tt, tt), 0) < lax.broadcasted_iota(i32, (tt, tt), 1)).astype(bf16)
    rank_e = jnp.dot(sel.astype(bf16), upper, preferred_element_type=f32)
    ranks = [jnp.sum(jnp.where(hot, rank_e, 0.0), axis=0, keepdims=True) for hot in hots]
    cnt = jnp.broadcast_to(jnp.sum(sel, axis=1, keepdims=True), (E, LANES))
    return jnp.concatenate(idxs, axis=0), w, jnp.concatenate(ranks, axis=0).astype(i32), cnt


def _route(logits):
    n = logits.shape[0]
    tt = SORT_TOKENS * ROUTE_TILES
    col = lambda i: (0, i)
    return pl.pallas_call(
        _route_body,
        grid=(n // tt,),
        in_specs=[pl.BlockSpec((tt, LANES), lambda i: (i, 0))],
        out_specs=[pl.BlockSpec((TOPK, tt), col), pl.BlockSpec((TOPK, tt), col), pl.BlockSpec((TOPK, tt), col),
                   pl.BlockSpec((ROUTE_TILES, E, LANES), lambda i: (i, 0, 0))],
        out_shape=[jax.ShapeDtypeStruct((TOPK, n), i32), jax.ShapeDtypeStruct((TOPK, n), f32),
                   jax.ShapeDtypeStruct((TOPK, n), i32), jax.ShapeDtypeStruct((n // SORT_TOKENS, E, LANES), f32)],
        compiler_params=pltpu.CompilerParams(dimension_semantics=("arbitrary",), vmem_limit_bytes=VMEM_LIMIT),
        name="route",
    )(logits)


def _unit_copy(src, dst, sem):
    return pltpu.make_async_copy(src, dst, sem)


def _start_runs(tile, runs_ref, start):
    def per_run(e_, carry):
        k_ = (tile * E + e_) * 3
        units = runs_ref[k_]
        lrow = pl.multiple_of(runs_ref[k_ + 1], UNIT)
        srow = pl.multiple_of(runs_ref[k_ + 2], UNIT)

        def large(c, cc):
            off = pl.multiple_of(c * (4 * UNIT), 4 * UNIT)
            start(lrow + off, srow + off, 4 * UNIT, 0)
            return cc

        lax.fori_loop(0, units // 4, large, 0)
        off2 = pl.multiple_of((units // 4) * (4 * UNIT), 4 * UNIT)

        @pl.when(units & 2 != 0)
        def _():
            start(lrow + off2, srow + off2, 2 * UNIT, 1)

        off1 = pl.multiple_of(off2 + (units & 2) * UNIT, UNIT)

        @pl.when(units & 1 != 0)
        def _():
            start(lrow + off1, srow + off1, UNIT, 1)

        return carry

    lax.fori_loop(0, E, per_run, 0)


def _wait_units(n_units, wait_rows):
    for bit in reversed(range(MAX_UNITS.bit_length())):
        @pl.when((n_units >> bit) & 1 == 1)
        def _():
            wait_rows((1 << bit) * UNIT)


def _in_group(i, bound):
    return (i >= bound[0]) & (i < bound[1])


def _group_block(i, bound):
    return jnp.clip(i - bound[0], 0, bound[1] - bound[0] - 1)


def _group_value(i, bounds, refs):
    value = refs[-1][...]
    for ref, bound in zip(reversed(refs[:-1]), reversed(bounds[:-1])):
        value = jnp.where(_in_group(i, bound), ref[...], value)
    return value


def _dispatch_body(bounds, runs_ref, tunits_ref, zstart_ref, zunits_ref, lpos_ref, *refs):
    h2_refs = refs[:len(bounds)]
    xbuf_ref, xs_ref, zero_ref, sem, zsem = refs[len(bounds):]
    tt = SORT_TOKENS
    i = pl.program_id(0)
    last = pl.num_programs(0) - 1
    slot = i % 2

    def start(lrow, srow, rows, priority):
        _unit_copy(xs_ref.at[slot, pl.ds(pl.multiple_of(lrow, UNIT), rows)],
                   xbuf_ref.at[pl.ds(pl.multiple_of(srow, UNIT), rows)], sem.at[slot]).start(priority=priority)

    def drain(tile, slot_):
        _wait_units(tunits_ref[tile], lambda r: _unit_copy(
            xs_ref.at[slot_, pl.ds(0, r)], xbuf_ref.at[pl.ds(0, r)], sem.at[slot_]).wait())

    @pl.when(i >= 2)
    def _():
        drain(i - 2, slot)

    lpos = lpos_ref[...]
    rows = lax.broadcasted_iota(i32, (SORT_ROWS, tt), 0)
    perm = jnp.zeros((SORT_ROWS, tt), f32)
    for k_ in range(TOPK):
        perm = jnp.where(rows == lpos[k_:k_ + 1, :], 1.0, perm)
    xs_ref[slot] = jnp.dot(perm.astype(bf16), _group_value(i, bounds, h2_refs), preferred_element_type=f32)
    _start_runs(i, runs_ref, start)

    @pl.when(i == last)
    def _():
        zero_ref[...] = jnp.zeros_like(zero_ref)

        def zero_copy(srow):
            return _unit_copy(zero_ref, xbuf_ref.at[pl.ds(srow, UNIT)], zsem)

        def fill(e_, carry):
            def start_one(u, c):
                zero_copy(pl.multiple_of(zstart_ref[e_] + u * UNIT, UNIT)).start()
                return c

            lax.fori_loop(0, zunits_ref[e_], start_one, 0)
            return carry

        def fill_wait(e_, carry):
            def wait_one(u, c):
                zero_copy(0).wait()
                return c

            lax.fori_loop(0, zunits_ref[e_], wait_one, 0)
            return carry

        lax.fori_loop(0, E + 1, fill, 0)
        lax.fori_loop(0, E + 1, fill_wait, 0)
        drain(i, slot)

        @pl.when(i >= 1)
        def _():
            drain(i - 1, 1 - slot)


def _tile_bounds(sizes):
    ends = np.cumsum([n // SORT_TOKENS for n in sizes])
    return tuple((int(e - n // SORT_TOKENS), int(e)) for e, n in zip(ends, sizes))


def _dispatch(h2s, lpos, runs, tunits, zstart, zunits, n_rows):
    tt = SORT_TOKENS
    bounds = _tile_bounds([h.shape[0] for h in h2s])
    return pl.pallas_call(
        functools.partial(_dispatch_body, bounds),
        grid_spec=pltpu.PrefetchScalarGridSpec(
            num_scalar_prefetch=4,
            grid=(bounds[-1][1],),
            in_specs=[pl.BlockSpec((TOPK, tt), lambda i, *_: (0, i))]
                     + [pl.BlockSpec((tt, D), lambda i, *_, bound=bound: (_group_block(i, bound), 0))
                        for bound in bounds],
            out_specs=pl.BlockSpec(memory_space=pl.ANY),
            scratch_shapes=[pltpu.VMEM((2, SORT_ROWS, D), f32), pltpu.VMEM((UNIT, D), f32),
                            pltpu.SemaphoreType.DMA((2,)), pltpu.SemaphoreType.DMA(())]),
        out_shape=jax.ShapeDtypeStruct((n_rows, D), f32),
        compiler_params=pltpu.CompilerParams(dimension_semantics=("arbitrary",), vmem_limit_bytes=VMEM_LIMIT),
        name="dispatch",
    )(runs, tunits, zstart, zunits, lpos, *h2s)


def _expert_body(be_ref, nused_ref, x_ref, wgu_ref, bgu_ref, wd_ref, bd_ref, y_ref, wgu_bf_ref, wd_bf_ref):
    b_ = pl.program_id(0)

    @pl.when(b_ >= nused_ref[0])
    def _():
        y_ref[...] = jnp.zeros_like(y_ref)

    @pl.when((b_ == 0) | ((b_ < nused_ref[0]) & (be_ref[b_] != be_ref[jnp.maximum(b_ - 1, 0)])))
    def _():
        wgu_bf_ref[...] = wgu_ref[...].astype(bf16)
        wd_bf_ref[...] = wd_ref[...].astype(bf16)

    @pl.when(b_ < nused_ref[0])
    def _():
        gu = jnp.dot(x_ref[...].astype(bf16), wgu_bf_ref[...], preferred_element_type=f32) + bgu_ref[...]
        gate = jnp.minimum(gu[:, :FF], SWIGLU_LIMIT)
        up = jnp.clip(gu[:, FF:], -SWIGLU_LIMIT, SWIGLU_LIMIT)
        act = (up + 1.0) * gate * jax.nn.sigmoid(SWIGLU_ALPHA * gate)
        y_ref[...] = jnp.dot(act.astype(bf16), wd_bf_ref[...], preferred_element_type=f32) + bd_ref[...]


def _experts(block_e, n_used, xbuf, wgu, bgu, wd, bd):
    n_rows = xbuf.shape[0]
    r = EXPERT_ROWS
    blk = lambda b_, nu: jnp.minimum(b_, nu[0] - 1)
    wspec = lambda shape: pl.BlockSpec((None,) + shape, lambda b_, be, nu: (be[blk(b_, nu)], 0, 0))
    return pl.pallas_call(
        _expert_body,
        grid_spec=pltpu.PrefetchScalarGridSpec(
            num_scalar_prefetch=2,
            grid=(n_rows // r,),
            in_specs=[pl.BlockSpec((r, D), lambda b_, be, nu: (blk(b_, nu), 0)),
                      wspec((D, 2 * FF)), wspec((1, 2 * FF)), wspec((FF, D)), wspec((1, D))],
            out_specs=pl.BlockSpec((r, D), lambda b_, be, nu: (b_, 0)),
            scratch_shapes=[pltpu.VMEM((D, 2 * FF), bf16), pltpu.VMEM((FF, D), bf16)]),
        out_shape=jax.ShapeDtypeStruct((n_rows, D), f32),
        compiler_params=pltpu.CompilerParams(dimension_semantics=("arbitrary",), vmem_limit_bytes=VMEM_LIMIT),
        name="experts",
    )(block_e, n_used, xbuf, wgu, bgu, wd, bd)


def _combine_body(bounds, runs_ref, tunits_ref,
                  lpos_ref, w_ref, mod_ref, ln2w_ref, ln2b_ref, ybuf_ref, *refs):
    n_groups = len(bounds)
    x1_refs, o_refs = refs[:n_groups], refs[n_groups:2 * n_groups]
    ys_ref, sem = refs[2 * n_groups:]
    tt = SORT_TOKENS
    i = pl.program_id(0)
    slot = i % 2

    def fetch(tile, slot_):
        def start(lrow, srow, rows, priority):
            _unit_copy(ybuf_ref.at[pl.ds(pl.multiple_of(srow, UNIT), rows)],
                       ys_ref.at[slot_, pl.ds(pl.multiple_of(lrow, UNIT), rows)],
                       sem.at[slot_]).start(priority=priority)

        _start_runs(tile, runs_ref, start)

    @pl.when(i == 0)
    def _():
        ys_ref[...] = jnp.zeros_like(ys_ref)
        fetch(0, 0)

    @pl.when(i + 1 < pl.num_programs(0))
    def _():
        fetch(i + 1, 1 - slot)

    _wait_units(tunits_ref[i], lambda r: _unit_copy(
        ybuf_ref.at[pl.ds(0, r)], ys_ref.at[slot, pl.ds(0, r)], sem.at[slot]).wait())

    lpos, w = lpos_ref[...], w_ref[...]
    cols = lax.broadcasted_iota(i32, (tt, SORT_ROWS), 1)
    wm = jnp.zeros((tt, SORT_ROWS), f32)
    for k_ in range(TOPK):
        wm = jnp.where(cols == lpos[:, k_:k_ + 1], w[:, k_:k_ + 1], wm)
    ff = jnp.dot(wm.astype(bf16), ys_ref[slot].astype(bf16), preferred_element_type=f32)
    z = DN_ALPHA * _group_value(i, bounds, x1_refs) + mod_ref[5:6, :] * ff
    out = _normalize(z, LN_EPS) * ln2w_ref[...] + ln2b_ref[...]
    for o_ref, bound in zip(o_refs, bounds):
        @pl.when(_in_group(i, bound))
        def _():
            o_ref[...] = out


def _combine(runs, tunits, lpos_t, w_t, mod, ln2w, ln2b, ybuf, x1s, seqs):
    tt = SORT_TOKENS
    bounds = _tile_bounds([x.shape[0] for x in x1s])
    tok = lambda width: pl.BlockSpec((tt, width), lambda i, *_: (i, 0))
    grp = [pl.BlockSpec((tt, D), lambda i, *_, bound=bound: (_group_block(i, bound), 0)) for bound in bounds]

    def mod_row(i, *_):
        row, base = 0, 0
        for bound, x, seq in zip(bounds, x1s, seqs):
            row = jnp.where(_in_group(i, bound), base + (i - bound[0]) * tt // seq, row)
            base += x.shape[0] // seq
        return (row, 0, 0)

    return pl.pallas_call(
        functools.partial(_combine_body, bounds),
        grid_spec=pltpu.PrefetchScalarGridSpec(
            num_scalar_prefetch=2,
            grid=(bounds[-1][1],),
            in_specs=[tok(TOPK), tok(TOPK), pl.BlockSpec((None, 6, D), mod_row),
                      _const_spec((1, D)), _const_spec((1, D)), pl.BlockSpec(memory_space=pl.ANY)] + grp,
            out_specs=grp,
            scratch_shapes=[pltpu.VMEM((2, SORT_ROWS, D), f32), pltpu.SemaphoreType.DMA((2,))]),
        out_shape=[jax.ShapeDtypeStruct(x.shape, f32) for x in x1s],
        compiler_params=pltpu.CompilerParams(dimension_semantics=("arbitrary",), vmem_limit_bytes=VMEM_LIMIT),
        name="combine",
    )(runs, tunits, lpos_t, w_t, mod, ln2w, ln2b, ybuf, *x1s)


def _rope_tables(s):
    half = HD // 2
    nf32 = np.float32
    inv_freq = np.power(nf32(ROPE_BASE), (nf32(-2.0) * np.arange(half, dtype=nf32) / nf32(HD)).astype(nf32)).astype(nf32)
    ang = (np.arange(s, dtype=nf32)[:, None] * inv_freq[None, :]).astype(nf32).astype(np.float64)
    cos, sin = np.cos(ang).astype(nf32), np.sin(ang).astype(nf32)
    return jnp.asarray(np.concatenate([cos, cos], axis=1)), jnp.asarray(np.concatenate([-sin, sin], axis=1))


def _retention_tables(theta_f, theta_b):
    lg_f = jax.nn.log_sigmoid(theta_f.astype(f32))
    lg_b = jax.nn.log_sigmoid(theta_b.astype(f32))
    idx = jnp.arange(CHUNK, dtype=f32)
    diff = idx[:, None] - idx[None, :]
    dec_f = jnp.where(diff >= 0, jnp.exp(jnp.maximum(diff, 0.0)[None] * lg_f[:, None, None]), 0.0)
    dec_b = jnp.where(diff < 0, jnp.exp(jnp.maximum(-diff, 0.0)[None] * lg_b[:, None, None]), 0.0)
    kscale = HD ** -0.5
    mtab = ((dec_f + dec_b) * kscale).transpose(1, 0, 2).reshape(CHUNK, D)
    lanes = lambda t: jnp.repeat(t.T, HD, axis=1)
    zf = lanes(jnp.exp((CHUNK - 1.0 - idx)[None, :] * lg_f[:, None]) * kscale)
    zb = lanes(jnp.exp(idx[None, :] * lg_b[:, None]) * kscale)
    xif = lanes(jnp.exp((idx + 1.0)[None, :] * lg_f[:, None]))
    xib = lanes(jnp.exp((CHUNK - idx)[None, :] * lg_b[:, None]))
    cdf = jnp.repeat(jnp.exp(CHUNK * lg_f), HD)[None, :]
    cdb = jnp.repeat(jnp.exp(CHUNK * lg_b), HD)[None, :]
    return (mtab, zf, zb, xib, cdb), (xif, cdf)


def _hi_lo(a):
    hi = a.astype(bf16)
    return jnp.concatenate([hi, (a - hi.astype(f32)).astype(bf16)], axis=1)


def _mixer(x, mod, p):
    p1, cr, yp, qr, kvf = _mixer_a(x, mod, p["win"], p["glnw"], p["glnb"], p["wsp"], p["bsp"],
                                   _rope_tables(x.shape[1]), p["tabs_a"], p["gnw"])
    xif, cdf = p["tabs_b"]
    return _mixer_b(yp, qr, kvf, p1, cr, x, mod, xif, cdf, p["wout"], p["ln1w"], p["ln1b"], p["wr"], p["br"])


def _moe_ln2(x1s, h2s, logits, mod, p):
    routed = [_route(lg) for lg in logits]
    idx, w, lrank = (jnp.concatenate([r[j] for r in routed], axis=1) for j in range(3))
    cnt = jnp.concatenate([r[3] for r in routed], axis=0)
    n = idx.shape[1]

    tt = SORT_TOKENS
    nt = n // tt
    counts = cnt[:, :, 0].astype(i32)
    run = (counts + UNIT - 1) // UNIT * UNIT
    loff = jnp.cumsum(run, axis=1) - run
    total = jnp.sum(run, axis=0)
    padded = (total + EXPERT_ROWS - 1) // EXPERT_ROWS * EXPERT_ROWS
    pends = jnp.cumsum(padded)
    pstarts = pends - padded
    seg = pstarts[None, :] + jnp.cumsum(run, axis=0) - run
    n_blocks = (n * TOPK + nt * E * (UNIT - 1)) // EXPERT_ROWS + E
    n_used = (pends[E - 1:] // EXPERT_ROWS).astype(i32)
    block_start = jnp.arange(n_blocks, dtype=i32) * EXPERT_ROWS
    block_e = jnp.minimum(jnp.sum((pends[None, :] <= block_start[:, None]).astype(i32), axis=1), E - 1)
    lpos = lrank.reshape(TOPK, nt, tt)
    idx3 = idx.reshape(TOPK, nt, tt)
    for e_ in range(E):
        lpos = lpos + jnp.where(idx3 == e_, loff[None, :, e_, None], 0)
    lpos = lpos.reshape(TOPK, n)
    flat = lambda a: a.reshape(-1).astype(i32)
    runs = flat(jnp.stack([run // UNIT, loff, seg], axis=-1))
    tunits = flat(jnp.sum(run, axis=1) // UNIT)

    n_rows = n_blocks * EXPERT_ROWS
    zstart = jnp.concatenate([pstarts + total, pends[E - 1:]])
    zunits = jnp.concatenate([padded - total, n_rows - pends[E - 1:]]) // UNIT
    xbuf = _dispatch(h2s, lpos, runs, tunits, flat(zstart), flat(zunits), n_rows)
    ybuf = _experts(block_e.astype(i32), n_used, xbuf, p["wgu"], p["bgu"], p["wd"], p["bd"])
    outs = _combine(runs, tunits, lpos.T, w.T, mod, p["ln2w"], p["ln2b"], ybuf,
                    [x.reshape(-1, D) for x in x1s], [x.shape[1] for x in x1s])
    return [o.reshape(x.shape) for o, x in zip(outs, x1s)]


def kernel(x_prompt, x_sample, c_prompt, c_sample, w_ada, b_ada, w_in, gmlp_ln_w, gmlp_ln_b, w_spatial, b_spatial, ret_theta_fwd, ret_theta_bwd, ret_gn_w, w_out, ln1_w, ln1_b, w_router, b_router, w_gate_up, b_gate_up, w_down, b_down, ln2_w, ln2_b):
    l = 0
    bp, bs = x_prompt.shape[0], x_sample.shape[0]
    c_all = jnp.concatenate([c_prompt, c_sample], axis=0)
    c_rows = -(-c_all.shape[0] // SUBLANES) * SUBLANES
    c_all = jnp.pad(c_all, ((0, c_rows - c_all.shape[0]), (0, 0)))
    mod = _adaln(c_all, w_ada[l], b_ada[l]).reshape(c_rows, 6, D)

    tabs_a, tabs_b = _retention_tables(ret_theta_fwd[l], ret_theta_bwd[l])
    row = lambda a: a.reshape(1, -1)
    p = dict(
        win=w_in[l].astype(bf16), glnw=row(gmlp_ln_w[l]), glnb=row(gmlp_ln_b[l]),
        wsp=w_spatial[l].astype(bf16), bsp=jnp.repeat(b_spatial[l].T, HD, axis=1),
        tabs_a=tabs_a, tabs_b=tabs_b, gnw=row(ret_gn_w[l]),
        wout=w_out[l].astype(bf16), ln1w=row(ln1_w[l]), ln1b=row(ln1_b[l]),
        wr=_hi_lo(jnp.pad(w_router[l], ((0, 0), (0, LANES - E)))), br=jnp.pad(row(b_router[l]), ((0, 0), (0, LANES - E))),
        wgu=w_gate_up[l], bgu=b_gate_up[l].reshape(E, 1, 2 * FF),
        wd=w_down[l], bd=b_down[l].reshape(E, 1, D),
        ln2w=row(ln2_w[l]), ln2b=row(ln2_b[l]),
    )
    mixed = [_mixer(x_prompt, mod[:bp], p), _mixer(x_sample, mod[bp:bp + bs], p)]
    x1s, h2s, logits = zip(*mixed)
    y_prompt, y_sample = _moe_ln2(x1s, h2s, logits, mod, p)
    return (y_prompt, y_sample)
```

```python
import functools

import numpy as np
import jax
import jax.numpy as jnp
from jax import lax
from jax.experimental import pallas as pl
from jax.experimental.pallas import tpu as pltpu

f32 = jnp.float32
bf16 = jnp.bfloat16
i32 = jnp.int32

D = 1024
CHUNK = 128
HEADS = 8
HD = D // HEADS
N_SEG = 8
E = 32
TOPK = 4
FF = 1024
ROPE_BASE = 10000.0
SWIGLU_LIMIT = 7.0
SWIGLU_ALPHA = 1.702
LN_EPS = 1e-5
GN_EPS = 1e-6
DEPTH = 1
DN_ALPHA = (2 * DEPTH) ** 0.25

LANES = 128
SUBLANES = 8

MIX_TOKENS = 512
SORT_TOKENS = 512
UNIT = 2 * SUBLANES
SORT_ROWS = -(-(SORT_TOKENS * TOPK + E * (UNIT - 1)) // LANES) * LANES
MAX_UNITS = SORT_ROWS // UNIT
ROUTE_TILES = 4
EXPERT_ROWS = 512
VMEM_LIMIT = 56 * 1024 * 1024

_HI = lax.Precision.HIGHEST


def _const_spec(shape):
    nd = len(shape)
    return pl.BlockSpec(shape, lambda *_: (0,) * nd, pipeline_mode=pl.Buffered(1))


def _gelu(x):
    return 0.5 * x * (1.0 + lax.erf(x * (2.0 ** -0.5)))


def _normalize(x, eps):
    mu = jnp.mean(x, axis=-1, keepdims=True)
    xc = x - mu
    var = jnp.mean(xc * xc, axis=-1, keepdims=True)
    return xc * lax.rsqrt(var + eps)


def _silu(x):
    return x * jax.nn.sigmoid(x)


def _head(h):
    return slice(h * HD, (h + 1) * HD)


def _adaln_body(c_ref, w_ref, b_ref, o_ref):
    s = _silu(c_ref[...])
    o_ref[...] = jnp.dot(s, w_ref[...], preferred_element_type=f32, precision=_HI) + b_ref[...]


def _adaln(c, w_ada, b_ada):
    rows = c.shape[0]
    return pl.pallas_call(
        _adaln_body,
        grid=(6,),
        in_specs=[pl.BlockSpec((rows, D), lambda j: (0, 0)),
                  pl.BlockSpec((D, D), lambda j: (0, j)),
                  pl.BlockSpec((1, D), lambda j: (0, j))],
        out_specs=pl.BlockSpec((rows, D), lambda j: (0, j)),
        out_shape=jax.ShapeDtypeStruct((rows, 6 * D), f32),
        compiler_params=pltpu.CompilerParams(vmem_limit_bytes=VMEM_LIMIT),
        name="adaln",
    )(c, w_ada, b_ada.reshape(1, 6 * D))


def _mixer_a_body(nck, x_ref, mod_ref, win_ref, glnw_ref, glnb_ref, wsp_ref, bsp_ref,
                  cos_ref, sin_ref, mtab_ref, zf_ref, zb_ref, xib_ref, cdb_ref, gnw_ref,
                  p1_ref, cr_ref, yp_ref, qr_ref, kvf_ref, sb_ref):
    @pl.when(pl.program_id(1) == 0)
    def _():
        sb_ref[...] = jnp.zeros_like(sb_ref)

    h = (x_ref[...] * (1.0 + mod_ref[1:2, :]) + mod_ref[0:1, :]).astype(bf16)

    def seg(j):
        return jnp.dot(h, win_ref[:, j * D:(j + 1) * D], preferred_element_type=f32)

    vn = _normalize(_gelu(seg(1)), LN_EPS) * glnw_ref[...] + glnb_ref[...]
    vnb = vn.astype(bf16)
    ug = _gelu(seg(0))
    ga = jax.nn.sigmoid(seg(6))
    for c in range(nck):
        rows = slice(c * CHUNK, (c + 1) * CHUNK)
        mixed = jnp.concatenate(
            [jnp.dot(wsp_ref[g], vnb[rows, _head(g)], preferred_element_type=f32) for g in range(HEADS)], axis=1)
        p1_ref[rows, :] = (ga[rows, :] * (ug[rows, :] * (mixed + bsp_ref[...]))).astype(bf16)

    q = seg(2)
    k = seg(3)
    v = seg(4).astype(bf16)
    cos, sin = cos_ref[...], sin_ref[...]

    def rotary(t):
        return jnp.concatenate([t[:, _head(h_)] * cos + pltpu.roll(t[:, _head(h_)], HD // 2, 1) * sin
                                for h_ in range(HEADS)], axis=1)

    qf = rotary(q)
    kr = rotary(k)
    qr = qf.astype(bf16)
    qr_ref[...] = qr
    for c in reversed(range(nck)):
        rows = slice(c * CHUNK, (c + 1) * CHUNK)
        kc = kr[rows, :]
        kcb = kc.astype(bf16)
        kzf = (kc * zf_ref[...]).astype(bf16)
        kzb = (kc * zb_ref[...]).astype(bf16)
        qx = (qf[rows, :] * xib_ref[...]).astype(bf16)
        sb = sb_ref[...]
        sbb = sb.astype(bf16)
        yps, kvfs, kvbs = [], [], []
        for h_ in range(HEADS):
            hs = _head(h_)
            vh = v[rows, hs]
            sc = lax.dot_general(qr[rows, hs], kcb[:, hs], (((1,), (1,)), ((), ())), preferred_element_type=f32)
            sc = (sc * mtab_ref[:, hs]).astype(bf16)
            yps.append(jnp.dot(sc, vh, preferred_element_type=f32)
                       + jnp.dot(qx[:, hs], sbb[:, hs], preferred_element_type=f32))
            kvfs.append(lax.dot_general(kzf[:, hs], vh, (((0,), (0,)), ((), ())), preferred_element_type=f32))
            kvbs.append(lax.dot_general(kzb[:, hs], vh, (((0,), (0,)), ((), ())), preferred_element_type=f32))
        yp_ref[rows, :] = jnp.concatenate(yps, axis=1).astype(bf16)
        kvf_ref[c] = jnp.concatenate(kvfs, axis=1)
        sb_ref[...] = sb * cdb_ref[...] + jnp.concatenate(kvbs, axis=1)

    cr_ref[...] = (jax.nn.sigmoid(seg(7)) * _silu(seg(5)) * gnw_ref[...]).astype(bf16)


def _mixer_a(x, mod, win, glnw, glnb, wsp, bsp, rope, tabs, gnw):
    b, s, _ = x.shape
    t = MIX_TOKENS
    nt, nck = s // t, t // CHUNK
    cos, sin = rope
    mtab, zf, zb, xib, cdb = tabs
    rev = lambda bi, i: (bi, nt - 1 - i, 0)
    rope_spec = pl.BlockSpec((t, HD), lambda bi, i: (nt - 1 - i, 0))
    tok = lambda dt: jax.ShapeDtypeStruct((b, s, D), dt)
    return pl.pallas_call(
        functools.partial(_mixer_a_body, nck),
        grid=(b, nt),
        in_specs=[pl.BlockSpec((None, t, D), rev),
                  pl.BlockSpec((None, 6, D), lambda bi, i: (bi, 0, 0)),
                  _const_spec((D, N_SEG * D)),
                  _const_spec((1, D)), _const_spec((1, D)),
                  _const_spec((HEADS, CHUNK, CHUNK)), _const_spec((CHUNK, D)),
                  rope_spec, rope_spec,
                  _const_spec((CHUNK, D)), _const_spec((CHUNK, D)), _const_spec((CHUNK, D)),
                  _const_spec((CHUNK, D)), _const_spec((1, D)), _const_spec((1, D))],
        out_specs=[pl.BlockSpec((None, t, D), rev)] * 4
                  + [pl.BlockSpec((None, nck, CHUNK, D), lambda bi, i: (bi, nt - 1 - i, 0, 0))],
        out_shape=[tok(bf16), tok(bf16), tok(bf16), tok(bf16),
                   jax.ShapeDtypeStruct((b, s // CHUNK, CHUNK, D), f32)],
        scratch_shapes=[pltpu.VMEM((CHUNK, D), f32)],
        compiler_params=pltpu.CompilerParams(dimension_semantics=("arbitrary", "arbitrary"),
                                             vmem_limit_bytes=VMEM_LIMIT),
        name="mixer_a",
    )(x, mod, win, glnw, glnb, wsp, bsp, cos, sin, mtab, zf, zb, xib, cdb, gnw)


def _mixer_b_body(nck, yp_ref, qr_ref, kvf_ref, p1_ref, cr_ref, x_ref, mod_ref, xif_ref, cdf_ref,
                  wout_ref, ln1w_ref, ln1b_ref, wr_ref, br_ref,
                  x1_ref, h2_ref, lg_ref, sf_ref):
    @pl.when(pl.program_id(1) == 0)
    def _():
        sf_ref[...] = jnp.zeros_like(sf_ref)

    ys = []
    for c in range(nck):
        rows = slice(c * CHUNK, (c + 1) * CHUNK)
        sf = sf_ref[...]
        sfb = sf.astype(bf16)
        cross = jnp.concatenate(
            [jnp.dot(qr_ref[rows, _head(h_)], sfb[:, _head(h_)], preferred_element_type=f32)
             for h_ in range(HEADS)], axis=1)
        ys.append(yp_ref[rows, :].astype(f32) + cross * xif_ref[...])
        sf_ref[...] = sf * cdf_ref[...] + kvf_ref[c]
    y = jnp.concatenate(ys, axis=0)
    yn = jnp.concatenate([_normalize(y[:, _head(h_)], GN_EPS) for h_ in range(HEADS)], axis=1)
    merged = p1_ref[...].astype(f32) + cr_ref[...].astype(f32) * yn
    mix = jnp.dot(merged.astype(bf16), wout_ref[...], preferred_element_type=f32)
    x1 = _normalize(DN_ALPHA * x_ref[...] + mod_ref[2:3, :] * mix, LN_EPS) * ln1w_ref[...] + ln1b_ref[...]
    x1_ref[...] = x1
    h2 = x1 * (1.0 + mod_ref[4:5, :]) + mod_ref[3:4, :]
    h2_hi = h2.astype(bf16)
    h2_ref[...] = h2_hi
    h2_lo = (h2 - h2_hi.astype(f32)).astype(bf16)
    hi_terms = jnp.dot(h2_hi, wr_ref[...], preferred_element_type=f32)
    lo_term = jnp.dot(h2_lo, wr_ref[:, :LANES], preferred_element_type=f32)
    lg_ref[...] = hi_terms[:, :LANES] + hi_terms[:, LANES:] + lo_term + br_ref[...]


def _mixer_b(yp, qr, kvf, p1, cr, x, mod, xif, cdf, wout, ln1w, ln1b, wr, br):
    b, s, _ = x.shape
    t = MIX_TOKENS
    nt, nck = s // t, t // CHUNK
    fwd = lambda bi, i: (bi, i, 0)
    flat = lambda bi, i: (bi * nt + i, 0)
    return pl.pallas_call(
        functools.partial(_mixer_b_body, nck),
        grid=(b, nt),
        in_specs=[pl.BlockSpec((None, t, D), fwd), pl.BlockSpec((None, t, D), fwd),
                  pl.BlockSpec((None, nck, CHUNK, D), lambda bi, i: (bi, i, 0, 0)),
                  pl.BlockSpec((None, t, D), fwd), pl.BlockSpec((None, t, D), fwd),
                  pl.BlockSpec((None, t, D), fwd),
                  pl.BlockSpec((None, 6, D), lambda bi, i: (bi, 0, 0)),
                  _const_spec((CHUNK, D)), _const_spec((1, D)),
                  _const_spec((D, D)), _const_spec((1, D)), _const_spec((1, D)),
                  _const_spec((D, 2 * LANES)), _const_spec((1, LANES))],
        out_specs=[pl.BlockSpec((None, t, D), fwd),
                   pl.BlockSpec((t, D), flat),
                   pl.BlockSpec((t, LANES), flat)],
        out_shape=[jax.ShapeDtypeStruct((b, s, D), f32),
                   jax.ShapeDtypeStruct((b * s, D), bf16),
                   jax.ShapeDtypeStruct((b * s, LANES), f32)],
        scratch_shapes=[pltpu.VMEM((CHUNK, D), f32)],
        compiler_params=pltpu.CompilerParams(dimension_semantics=("arbitrary", "arbitrary"),
                                             vmem_limit_bytes=VMEM_LIMIT),
        name="mixer_b",
    )(yp, qr, kvf, p1, cr, x, mod, xif, cdf, wout, ln1w, ln1b, wr, br)


def _route_body(lg_ref, idx_ref, w_ref, rank_ref, cnt_ref):
    tt = SORT_TOKENS
    for j in range(lg_ref.shape[0] // tt):
        cols = slice(j * tt, (j + 1) * tt)
        idx, w, rank, cnt = _route_tile(lg_ref[cols, :])
        idx_ref[:, cols] = idx
        w_ref[:, cols] = w
        rank_ref[:, cols] = rank
        cnt_ref[j] = cnt


def _route_tile(logits):
    tt = logits.shape[0]
    l = logits.T[:E, :]
    iota_e = lax.broadcasted_iota(i32, (E, tt), 0)
    vals, idxs, hots = [], [], []
    for _ in range(TOPK):
        m = jnp.max(l, axis=0, keepdims=True)
        idx = jnp.min(jnp.where(l == m, iota_e, E), axis=0, keepdims=True)
        hot = iota_e == idx
        l = jnp.where(hot, -jnp.inf, l)
        vals.append(m)
        idxs.append(idx)
        hots.append(hot)
    exps = [jnp.exp(v_ - vals[0]) for v_ in vals]
    tot = exps[0] + exps[1] + exps[2] + exps[3]
    w = jnp.concatenate([e_ / tot for e_ in exps], axis=0)

    sel = jnp.zeros((E, tt), f32)
    for hot in hots:
        sel = sel + hot.astype(f32)
    upper = (lax.broadcasted_iota(i32, (tt, tt), 0) < lax.broadcasted_iota(i32, (tt, tt), 1)).astype(bf16)
    rank_e = jnp.dot(sel.astype(bf16), upper, preferred_element_type=f32)
    ranks = [jnp.sum(jnp.where(hot, rank_e, 0.0), axis=0, keepdims=True) for hot in hots]
    cnt = jnp.broadcast_to(jnp.sum(sel, axis=1, keepdims=True), (E, LANES))
    return jnp.concatenate(idxs, axis=0), w, jnp.concatenate(ranks, axis=0).astype(i32), cnt


def _route(logits):
    n = logits.shape[0]
    tt = SORT_TOKENS * ROUTE_TILES
    col = lambda i: (0, i)
    return pl.pallas_call(
        _route_body,
        grid=(n // tt,),
        in_specs=[pl.BlockSpec((tt, LANES), lambda i: (i, 0))],
        out_specs=[pl.BlockSpec((TOPK, tt), col), pl.BlockSpec((TOPK, tt), col), pl.BlockSpec((TOPK, tt), col),
                   pl.BlockSpec((ROUTE_TILES, E, LANES), lambda i: (i, 0, 0))],
        out_shape=[jax.ShapeDtypeStruct((TOPK, n), i32), jax.ShapeDtypeStruct((TOPK, n), f32),
                   jax.ShapeDtypeStruct((TOPK, n), i32), jax.ShapeDtypeStruct((n // SORT_TOKENS, E, LANES), f32)],
        compiler_params=pltpu.CompilerParams(dimension_semantics=("arbitrary",), vmem_limit_bytes=VMEM_LIMIT),
        name="route",
    )(logits)


def _unit_copy(src, dst, sem):
    return pltpu.make_async_copy(src, dst, sem)


def _start_runs(tile, runs_ref, start):
    def per_run(e_, carry):
        k_ = (tile * E + e_) * 3
        units = runs_ref[k_]
        lrow = pl.multiple_of(runs_ref[k_ + 1], UNIT)
        srow = pl.multiple_of(runs_ref[k_ + 2], UNIT)

        def large(c, cc):
            off = pl.multiple_of(c * (4 * UNIT), 4 * UNIT)
            start(lrow + off, srow + off, 4 * UNIT, 0)
            return cc

        lax.fori_loop(0, units // 4, large, 0)
        off2 = pl.multiple_of((units // 4) * (4 * UNIT), 4 * UNIT)

        @pl.when(units & 2 != 0)
        def _():
            start(lrow + off2, srow + off2, 2 * UNIT, 1)

        off1 = pl.multiple_of(off2 + (units & 2) * UNIT, UNIT)

        @pl.when(units & 1 != 0)
        def _():
            start(lrow + off1, srow + off1, UNIT, 1)

        return carry

    lax.fori_loop(0, E, per_run, 0)


def _wait_units(n_units, wait_rows):
    for bit in reversed(range(MAX_UNITS.bit_length())):
        @pl.when((n_units >> bit) & 1 == 1)
        def _():
            wait_rows((1 << bit) * UNIT)


def _in_group(i, bound):
    return (i >= bound[0]) & (i < bound[1])


def _group_block(i, bound):
    return jnp.clip(i - bound[0], 0, bound[1] - bound[0] - 1)


def _group_value(i, bounds, refs):
    value = refs[-1][...]
    for ref, bound in zip(reversed(refs[:-1]), reversed(bounds[:-1])):
        value = jnp.where(_in_group(i, bound), ref[...], value)
    return value


def _dispatch_body(bounds, runs_ref, tunits_ref, zstart_ref, zunits_ref, lpos_ref, *refs):
    h2_refs = refs[:len(bounds)]
    xbuf_ref, xs_ref, zero_ref, sem, zsem = refs[len(bounds):]
    tt = SORT_TOKENS
    i = pl.program_id(0)
    last = pl.num_programs(0) - 1
    slot = i % 2

    def start(lrow, srow, rows, priority):
        _unit_copy(xs_ref.at[slot, pl.ds(pl.multiple_of(lrow, UNIT), rows)],
                   xbuf_ref.at[pl.ds(pl.multiple_of(srow, UNIT), rows)], sem.at[slot]).start(priority=priority)

    def drain(tile, slot_):
        _wait_units(tunits_ref[tile], lambda r: _unit_copy(
            xs_ref.at[slot_, pl.ds(0, r)], xbuf_ref.at[pl.ds(0, r)], sem.at[slot_]).wait())

    @pl.when(i >= 2)
    def _():
        drain(i - 2, slot)

    lpos = lpos_ref[...]
    rows = lax.broadcasted_iota(i32, (SORT_ROWS, tt), 0)
    perm = jnp.zeros((SORT_ROWS, tt), f32)
    for k_ in range(TOPK):
        perm = jnp.where(rows == lpos[k_:k_ + 1, :], 1.0, perm)
    xs_ref[slot] = jnp.dot(perm.astype(bf16), _group_value(i, bounds, h2_refs),
                           preferred_element_type=f32).astype(bf16)
    _start_runs(i, runs_ref, start)

    @pl.when(i == last)
    def _():
        zero_ref[...] = jnp.zeros_like(zero_ref)

        def zero_copy(srow):
            return _unit_copy(zero_ref, xbuf_ref.at[pl.ds(srow, UNIT)], zsem)

        def fill(e_, carry):
            def start_one(u, c):
                zero_copy(pl.multiple_of(zstart_ref[e_] + u * UNIT, UNIT)).start()
                return c

            lax.fori_loop(0, zunits_ref[e_], start_one, 0)
            return carry

        def fill_wait(e_, carry):
            def wait_one(u, c):
                zero_copy(0).wait()
                return c

            lax.fori_loop(0, zunits_ref[e_], wait_one, 0)
            return carry

        lax.fori_loop(0, E + 1, fill, 0)
        lax.fori_loop(0, E + 1, fill_wait, 0)
        drain(i, slot)

        @pl.when(i >= 1)
        def _():
            drain(i - 1, 1 - slot)


def _tile_bounds(sizes):
    ends = np.cumsum([n // SORT_TOKENS for n in sizes])
    return tuple((int(e - n // SORT_TOKENS), int(e)) for e, n in zip(ends, sizes))


def _dispatch(h2s, lpos, runs, tunits, zstart, zunits, n_rows):
    tt = SORT_TOKENS
    bounds = _tile_bounds([h.shape[0] for h in h2s])
    return pl.pallas_call(
        functools.partial(_dispatch_body, bounds),
        grid_spec=pltpu.PrefetchScalarGridSpec(
            num_scalar_prefetch=4,
            grid=(bounds[-1][1],),
            in_specs=[pl.BlockSpec((TOPK, tt), lambda i, *_: (0, i))]
                     + [pl.BlockSpec((tt, D), lambda i, *_, bound=bound: (_group_block(i, bound), 0))
                        for bound in bounds],
            out_specs=pl.BlockSpec(memory_space=pl.ANY),
            scratch_shapes=[pltpu.VMEM((2, SORT_ROWS, D), bf16), pltpu.VMEM((UNIT, D), bf16),
                            pltpu.SemaphoreType.DMA((2,)), pltpu.SemaphoreType.DMA(())]),
        out_shape=jax.ShapeDtypeStruct((n_rows, D), bf16),
        compiler_params=pltpu.CompilerParams(dimension_semantics=("arbitrary",), vmem_limit_bytes=VMEM_LIMIT),
        name="dispatch",
    )(runs, tunits, zstart, zunits, lpos, *h2s)


def _expert_body(be_ref, nused_ref, x_ref, wgu_ref, bgu_ref, wd_ref, bd_ref, y_ref, wgu_bf_ref, wd_bf_ref):
    b_ = pl.program_id(0)

    @pl.when(b_ >= nused_ref[0])
    def _():
        y_ref[...] = jnp.zeros_like(y_ref)

    @pl.when((b_ == 0) | ((b_ < nused_ref[0]) & (be_ref[b_] != be_ref[jnp.maximum(b_ - 1, 0)])))
    def _():
        wgu_bf_ref[...] = wgu_ref[...].astype(bf16)
        wd_bf_ref[...] = wd_ref[...].astype(bf16)

    @pl.when(b_ < nused_ref[0])
    def _():
        gu = jnp.dot(x_ref[...], wgu_bf_ref[...], preferred_element_type=f32) + bgu_ref[...]
        gate = jnp.minimum(gu[:, :FF], SWIGLU_LIMIT)
        up = jnp.clip(gu[:, FF:], -SWIGLU_LIMIT, SWIGLU_LIMIT)
        act = (up + 1.0) * gate * jax.nn.sigmoid(SWIGLU_ALPHA * gate)
        y_ref[...] = (jnp.dot(act.astype(bf16), wd_bf_ref[...], preferred_element_type=f32)
                      + bd_ref[...]).astype(bf16)


def _experts(block_e, n_used, xbuf, wgu, bgu, wd, bd):
    n_rows = xbuf.shape[0]
    r = EXPERT_ROWS
    blk = lambda b_, nu: jnp.minimum(b_, nu[0] - 1)
    wspec = lambda shape: pl.BlockSpec((None,) + shape, lambda b_, be, nu: (be[blk(b_, nu)], 0, 0))
    return pl.pallas_call(
        _expert_body,
        grid_spec=pltpu.PrefetchScalarGridSpec(
            num_scalar_prefetch=2,
            grid=(n_rows // r,),
            in_specs=[pl.BlockSpec((r, D), lambda b_, be, nu: (blk(b_, nu), 0)),
                      wspec((D, 2 * FF)), wspec((1, 2 * FF)), wspec((FF, D)), wspec((1, D))],
            out_specs=pl.BlockSpec((r, D), lambda b_, be, nu: (b_, 0)),
            scratch_shapes=[pltpu.VMEM((D, 2 * FF), bf16), pltpu.VMEM((FF, D), bf16)]),
        out_shape=jax.ShapeDtypeStruct((n_rows, D), bf16),
        compiler_params=pltpu.CompilerParams(dimension_semantics=("arbitrary",), vmem_limit_bytes=VMEM_LIMIT),
        name="experts",
    )(block_e, n_used, xbuf, wgu, bgu, wd, bd)


def _combine_body(bounds, runs_ref, tunits_ref,
                  lpos_ref, w_ref, mod_ref, ln2w_ref, ln2b_ref, ybuf_ref, *refs):
    n_groups = len(bounds)
    x1_refs, o_refs = refs[:n_groups], refs[n_groups:2 * n_groups]
    ys_ref, sem = refs[2 * n_groups:]
    tt = SORT_TOKENS
    i = pl.program_id(0)
    slot = i % 2

    def fetch(tile, slot_):
        def start(lrow, srow, rows, priority):
            _unit_copy(ybuf_ref.at[pl.ds(pl.multiple_of(srow, UNIT), rows)],
                       ys_ref.at[slot_, pl.ds(pl.multiple_of(lrow, UNIT), rows)],
                       sem.at[slot_]).start(priority=priority)

        _start_runs(tile, runs_ref, start)

    @pl.when(i == 0)
    def _():
        ys_ref[...] = jnp.zeros_like(ys_ref)
        fetch(0, 0)

    @pl.when(i + 1 < pl.num_programs(0))
    def _():
        fetch(i + 1, 1 - slot)

    _wait_units(tunits_ref[i], lambda r: _unit_copy(
        ybuf_ref.at[pl.ds(0, r)], ys_ref.at[slot, pl.ds(0, r)], sem.at[slot]).wait())

    lpos, w = lpos_ref[...], w_ref[...]
    cols = lax.broadcasted_iota(i32, (tt, SORT_ROWS), 1)
    wm = jnp.zeros((tt, SORT_ROWS), f32)
    for k_ in range(TOPK):
        wm = jnp.where(cols == lpos[:, k_:k_ + 1], w[:, k_:k_ + 1], wm)
    ff = jnp.dot(wm.astype(bf16), ys_ref[slot], preferred_element_type=f32)
    z = DN_ALPHA * _group_value(i, bounds, x1_refs) + mod_ref[5:6, :] * ff
    out = _normalize(z, LN_EPS) * ln2w_ref[...] + ln2b_ref[...]
    for o_ref, bound in zip(o_refs, bounds):
        @pl.when(_in_group(i, bound))
        def _():
            o_ref[...] = out


def _combine(runs, tunits, lpos_t, w_t, mod, ln2w, ln2b, ybuf, x1s, seqs):
    tt = SORT_TOKENS
    bounds = _tile_bounds([x.shape[0] for x in x1s])
    tok = lambda width: pl.BlockSpec((tt, width), lambda i, *_: (i, 0))
    grp = [pl.BlockSpec((tt, D), lambda i, *_, bound=bound: (_group_block(i, bound), 0)) for bound in bounds]

    def mod_row(i, *_):
        row, base = 0, 0
        for bound, x, seq in zip(bounds, x1s, seqs):
            row = jnp.where(_in_group(i, bound), base + (i - bound[0]) * tt // seq, row)
            base += x.shape[0] // seq
        return (row, 0, 0)

    return pl.pallas_call(
        functools.partial(_combine_body, bounds),
        grid_spec=pltpu.PrefetchScalarGridSpec(
            num_scalar_prefetch=2,
            grid=(bounds[-1][1],),
            in_specs=[tok(TOPK), tok(TOPK), pl.BlockSpec((None, 6, D), mod_row),
                      _const_spec((1, D)), _const_spec((1, D)), pl.BlockSpec(memory_space=pl.ANY)] + grp,
            out_specs=grp,
            scratch_shapes=[pltpu.VMEM((2, SORT_ROWS, D), bf16), pltpu.SemaphoreType.DMA((2,))]),
        out_shape=[jax.ShapeDtypeStruct(x.shape, f32) for x in x1s],
        compiler_params=pltpu.CompilerParams(dimension_semantics=("arbitrary",), vmem_limit_bytes=VMEM_LIMIT),
        name="combine",
    )(runs, tunits, lpos_t, w_t, mod, ln2w, ln2b, ybuf, *x1s)


def _rope_tables(s):
    half = HD // 2
    nf32 = np.float32
    inv_freq = np.power(nf32(ROPE_BASE), (nf32(-2.0) * np.arange(half, dtype=nf32) / nf32(HD)).astype(nf32)).astype(nf32)
    ang = (np.arange(s, dtype=nf32)[:, None] * inv_freq[None, :]).astype(nf32).astype(np.float64)
    cos, sin = np.cos(ang).astype(nf32), np.sin(ang).astype(nf32)
    return jnp.asarray(np.concatenate([cos, cos], axis=1)), jnp.asarray(np.concatenate([-sin, sin], axis=1))


def _retention_tables(theta_f, theta_b):
    lg_f = jax.nn.log_sigmoid(theta_f.astype(f32))
    lg_b = jax.nn.log_sigmoid(theta_b.astype(f32))
    idx = jnp.arange(CHUNK, dtype=f32)
    diff = idx[:, None] - idx[None, :]
    dec_f = jnp.where(diff >= 0, jnp.exp(jnp.maximum(diff, 0.0)[None] * lg_f[:, None, None]), 0.0)
    dec_b = jnp.where(diff < 0, jnp.exp(jnp.maximum(-diff, 0.0)[None] * lg_b[:, None, None]), 0.0)
    kscale = HD ** -0.5
    mtab = ((dec_f + dec_b) * kscale).transpose(1, 0, 2).reshape(CHUNK, D)
    lanes = lambda t: jnp.repeat(t.T, HD, axis=1)
    zf = lanes(jnp.exp((CHUNK - 1.0 - idx)[None, :] * lg_f[:, None]) * kscale)
    zb = lanes(jnp.exp(idx[None, :] * lg_b[:, None]) * kscale)
    xif = lanes(jnp.exp((idx + 1.0)[None, :] * lg_f[:, None]))
    xib = lanes(jnp.exp((CHUNK - idx)[None, :] * lg_b[:, None]))
    cdf = jnp.repeat(jnp.exp(CHUNK * lg_f), HD)[None, :]
    cdb = jnp.repeat(jnp.exp(CHUNK * lg_b), HD)[None, :]
    return (mtab, zf, zb, xib, cdb), (xif, cdf)


def _hi_lo(a):
    hi = a.astype(bf16)
    return jnp.concatenate([hi, (a - hi.astype(f32)).astype(bf16)], axis=1)


def _mixer(x, mod, p):
    p1, cr, yp, qr, kvf = _mixer_a(x, mod, p["win"], p["glnw"], p["glnb"], p["wsp"], p["bsp"],
                                   _rope_tables(x.shape[1]), p["tabs_a"], p["gnw"])
    xif, cdf = p["tabs_b"]
    return _mixer_b(yp, qr, kvf, p1, cr, x, mod, xif, cdf, p["wout"], p["ln1w"], p["ln1b"], p["wr"], p["br"])


def _moe_ln2(x1s, h2s, logits, mod, p):
    routed = [_route(lg) for lg in logits]
    idx, w, lrank = (jnp.concatenate([r[j] for r in routed], axis=1) for j in range(3))
    cnt = jnp.concatenate([r[3] for r in routed], axis=0)
    n = idx.shape[1]

    tt = SORT_TOKENS
    nt = n // tt
    counts = cnt[:, :, 0].astype(i32)
    run = (counts + UNIT - 1) // UNIT * UNIT
    loff = jnp.cumsum(run, axis=1) - run
    total = jnp.sum(run, axis=0)
    padded = (total + EXPERT_ROWS - 1) // EXPERT_ROWS * EXPERT_ROWS
    pends = jnp.cumsum(padded)
    pstarts = pends - padded
    seg = pstarts[None, :] + jnp.cumsum(run, axis=0) - run
    n_blocks = (n * TOPK + nt * E * (UNIT - 1)) // EXPERT_ROWS + E
    n_used = (pends[E - 1:] // EXPERT_ROWS).astype(i32)
    block_start = jnp.arange(n_blocks, dtype=i32) * EXPERT_ROWS
    block_e = jnp.minimum(jnp.sum((pends[None, :] <= block_start[:, None]).astype(i32), axis=1), E - 1)
    lpos = lrank.reshape(TOPK, nt, tt)
    idx3 = idx.reshape(TOPK, nt, tt)
    for e_ in range(E):
        lpos = lpos + jnp.where(idx3 == e_, loff[None, :, e_, None], 0)
    lpos = lpos.reshape(TOPK, n)
    flat = lambda a: a.reshape(-1).astype(i32)
    runs = flat(jnp.stack([run // UNIT, loff, seg], axis=-1))
    tunits = flat(jnp.sum(run, axis=1) // UNIT)

    n_rows = n_blocks * EXPERT_ROWS
    zstart = jnp.concatenate([pstarts + total, pends[E - 1:]])
    zunits = jnp.concatenate([padded - total, n_rows - pends[E - 1:]]) // UNIT
    xbuf = _dispatch(h2s, lpos, runs, tunits, flat(zstart), flat(zunits), n_rows)
    ybuf = _experts(block_e.astype(i32), n_used, xbuf, p["wgu"], p["bgu"], p["wd"], p["bd"])
    outs = _combine(runs, tunits, lpos.T, w.T, mod, p["ln2w"], p["ln2b"], ybuf,
                    [x.reshape(-1, D) for x in x1s], [x.shape[1] for x in x1s])
    return [o.reshape(x.shape) for o, x in zip(outs, x1s)]


def kernel(x_prompt, x_sample, c_prompt, c_sample, w_ada, b_ada, w_in, gmlp_ln_w, gmlp_ln_b, w_spatial, b_spatial, ret_theta_fwd, ret_theta_bwd, ret_gn_w, w_out, ln1_w, ln1_b, w_router, b_router, w_gate_up, b_gate_up, w_down, b_down, ln2_w, ln2_b):
    l = 0
    bp, bs = x_prompt.shape[0], x_sample.shape[0]
    c_all = jnp.concatenate([c_prompt, c_sample], axis=0)
    c_rows = -(-c_all.shape[0] // SUBLANES) * SUBLANES
    c_all = jnp.pad(c_all, ((0, c_rows - c_all.shape[0]), (0, 0)))
    mod = _adaln(c_all, w_ada[l], b_ada[l]).reshape(c_rows, 6, D)

    tabs_a, tabs_b = _retention_tables(ret_theta_fwd[l], ret_theta_bwd[l])
    row = lambda a: a.reshape(1, -1)
    p = dict(
        win=w_in[l].astype(bf16), glnw=row(gmlp_ln_w[l]), glnb=row(gmlp_ln_b[l]),
        wsp=w_spatial[l].astype(bf16), bsp=jnp.repeat(b_spatial[l].T, HD, axis=1),
        tabs_a=tabs_a, tabs_b=tabs_b, gnw=row(ret_gn_w[l]),
        wout=w_out[l].astype(bf16), ln1w=row(ln1_w[l]), ln1b=row(ln1_b[l]),
        wr=_hi_lo(jnp.pad(w_router[l], ((0, 0), (0, LANES - E)))), br=jnp.pad(row(b_router[l]), ((0, 0), (0, LANES - E))),
        wgu=w_gate_up[l], bgu=b_gate_up[l].reshape(E, 1, 2 * FF),
        wd=w_down[l], bd=b_down[l].reshape(E, 1, D),
        ln2w=row(ln2_w[l]), ln2b=row(ln2_b[l]),
    )
    mixed = [_mixer(x_prompt, mod[:bp], p), _mixer(x_sample, mod[bp:bp + bs], p)]
    x1s, h2s, logits = zip(*mixed)
    y_prompt, y_sample = _moe_ln2(x1s, h2s, logits, mod, p)
    return (y_prompt, y_sample)
```

```python
import functools

import numpy as np
import jax
import jax.numpy as jnp
from jax import lax
from jax.experimental import pallas as pl
from jax.experimental.pallas import tpu as pltpu

f32 = jnp.float32
bf16 = jnp.bfloat16
i32 = jnp.int32

D = 1024
CHUNK = 128
HEADS = 8
HD = D // HEADS
N_SEG = 8
E = 32
TOPK = 4
FF = 1024
ROPE_BASE = 10000.0
SWIGLU_LIMIT = 7.0
SWIGLU_ALPHA = 1.702
LN_EPS = 1e-5
GN_EPS = 1e-6
DEPTH = 1
DN_ALPHA = (2 * DEPTH) ** 0.25

LANES = 128
SUBLANES = 8

MIX_TOKENS = 512
SORT_TOKENS = 512
UNIT = SUBLANES
SORT_ROWS = -(-(SORT_TOKENS * TOPK + E * (UNIT - 1)) // LANES) * LANES
MAX_UNITS = SORT_ROWS // UNIT
ROUTE_TILES = 4
EXPERT_ROWS = 512
VMEM_LIMIT = 56 * 1024 * 1024

_HI = lax.Precision.HIGHEST


def _const_spec(shape):
    nd = len(shape)
    return pl.BlockSpec(shape, lambda *_: (0,) * nd, pipeline_mode=pl.Buffered(1))


def _gelu(x):
    return 0.5 * x * (1.0 + lax.erf(x * (2.0 ** -0.5)))


def _normalize(x, eps):
    mu = jnp.mean(x, axis=-1, keepdims=True)
    xc = x - mu
    var = jnp.mean(xc * xc, axis=-1, keepdims=True)
    return xc * lax.rsqrt(var + eps)


def _silu(x):
    return x * jax.nn.sigmoid(x)


def _head(h):
    return slice(h * HD, (h + 1) * HD)


def _adaln_body(c_ref, w_ref, b_ref, o_ref):
    s = _silu(c_ref[...])
    o_ref[...] = jnp.dot(s, w_ref[...], preferred_element_type=f32, precision=_HI) + b_ref[...]


def _adaln(c, w_ada, b_ada):
    rows = c.shape[0]
    return pl.pallas_call(
        _adaln_body,
        grid=(6,),
        in_specs=[pl.BlockSpec((rows, D), lambda j: (0, 0)),
                  pl.BlockSpec((D, D), lambda j: (0, j)),
                  pl.BlockSpec((1, D), lambda j: (0, j))],
        out_specs=pl.BlockSpec((rows, D), lambda j: (0, j)),
        out_shape=jax.ShapeDtypeStruct((rows, 6 * D), f32),
        compiler_params=pltpu.CompilerParams(vmem_limit_bytes=VMEM_LIMIT),
        name="adaln",
    )(c, w_ada, b_ada.reshape(1, 6 * D))


def _mixer_a_body(nck, x_ref, mod_ref, win_ref, glnw_ref, glnb_ref, wsp_ref, bsp_ref,
                  cos_ref, sin_ref, mtab_ref, zf_ref, zb_ref, xib_ref, cdb_ref, gnw_ref,
                  p1_ref, cr_ref, yp_ref, qr_ref, kvf_ref, sb_ref):
    @pl.when(pl.program_id(1) == 0)
    def _():
        sb_ref[...] = jnp.zeros_like(sb_ref)

    h = (x_ref[...] * (1.0 + mod_ref[1:2, :]) + mod_ref[0:1, :]).astype(bf16)

    def seg(j):
        return jnp.dot(h, win_ref[:, j * D:(j + 1) * D], preferred_element_type=f32)

    vn = _normalize(_gelu(seg(1)), LN_EPS) * glnw_ref[...] + glnb_ref[...]
    vnb = vn.astype(bf16)
    ug = _gelu(seg(0))
    ga = jax.nn.sigmoid(seg(6))
    for c in range(nck):
        rows = slice(c * CHUNK, (c + 1) * CHUNK)
        mixed = jnp.concatenate(
            [jnp.dot(wsp_ref[g], vnb[rows, _head(g)], preferred_element_type=f32) for g in range(HEADS)], axis=1)
        p1_ref[rows, :] = (ga[rows, :] * (ug[rows, :] * (mixed + bsp_ref[...]))).astype(bf16)

    q = seg(2)
    k = seg(3)
    v = seg(4).astype(bf16)
    cos, sin = cos_ref[...], sin_ref[...]

    def rotary(t):
        return jnp.concatenate([t[:, _head(h_)] * cos + pltpu.roll(t[:, _head(h_)], HD // 2, 1) * sin
                                for h_ in range(HEADS)], axis=1)

    qf = rotary(q)
    kr = rotary(k)
    qr = qf.astype(bf16)
    qr_ref[...] = qr
    for c in reversed(range(nck)):
        rows = slice(c * CHUNK, (c + 1) * CHUNK)
        kc = kr[rows, :]
        kcb = kc.astype(bf16)
        kzf = (kc * zf_ref[...]).astype(bf16)
        kzb = (kc * zb_ref[...]).astype(bf16)
        qx = (qf[rows, :] * xib_ref[...]).astype(bf16)
        sb = sb_ref[...]
        sbb = sb.astype(bf16)
        yps, kvfs, kvbs = [], [], []
        for h_ in range(HEADS):
            hs = _head(h_)
            vh = v[rows, hs]
            sc = lax.dot_general(qr[rows, hs], kcb[:, hs], (((1,), (1,)), ((), ())), preferred_element_type=f32)
            sc = (sc * mtab_ref[:, hs]).astype(bf16)
            yps.append(jnp.dot(sc, vh, preferred_element_type=f32)
                       + jnp.dot(qx[:, hs], sbb[:, hs], preferred_element_type=f32))
            kvfs.append(lax.dot_general(kzf[:, hs], vh, (((0,), (0,)), ((), ())), preferred_element_type=f32))
            kvbs.append(lax.dot_general(kzb[:, hs], vh, (((0,), (0,)), ((), ())), preferred_element_type=f32))
        yp_ref[rows, :] = jnp.concatenate(yps, axis=1).astype(bf16)
        kvf_ref[c] = jnp.concatenate(kvfs, axis=1).astype(bf16)
        sb_ref[...] = sb * cdb_ref[...] + jnp.concatenate(kvbs, axis=1)

    cr_ref[...] = (jax.nn.sigmoid(seg(7)) * _silu(seg(5)) * gnw_ref[...]).astype(bf16)


def _mixer_a(x, mod, win, glnw, glnb, wsp, bsp, rope, tabs, gnw):
    b, s, _ = x.shape
    t = MIX_TOKENS
    nt, nck = s // t, t // CHUNK
    cos, sin = rope
    mtab, zf, zb, xib, cdb = tabs
    rev = lambda bi, i: (bi, nt - 1 - i, 0)
    rope_spec = pl.BlockSpec((t, HD), lambda bi, i: (nt - 1 - i, 0))
    tok = lambda dt: jax.ShapeDtypeStruct((b, s, D), dt)
    return pl.pallas_call(
        functools.partial(_mixer_a_body, nck),
        grid=(b, nt),
        in_specs=[pl.BlockSpec((None, t, D), rev),
                  pl.BlockSpec((None, 6, D), lambda bi, i: (bi, 0, 0)),
                  _const_spec((D, N_SEG * D)),
                  _const_spec((1, D)), _const_spec((1, D)),
                  _const_spec((HEADS, CHUNK, CHUNK)), _const_spec((CHUNK, D)),
                  rope_spec, rope_spec,
                  _const_spec((CHUNK, D)), _const_spec((CHUNK, D)), _const_spec((CHUNK, D)),
                  _const_spec((CHUNK, D)), _const_spec((1, D)), _const_spec((1, D))],
        out_specs=[pl.BlockSpec((None, t, D), rev)] * 4
                  + [pl.BlockSpec((None, nck, CHUNK, D), lambda bi, i: (bi, nt - 1 - i, 0, 0))],
        out_shape=[tok(bf16), tok(bf16), tok(bf16), tok(bf16),
                   jax.ShapeDtypeStruct((b, s // CHUNK, CHUNK, D), bf16)],
        scratch_shapes=[pltpu.VMEM((CHUNK, D), f32)],
        compiler_params=pltpu.CompilerParams(dimension_semantics=("arbitrary", "arbitrary"),
                                             vmem_limit_bytes=VMEM_LIMIT),
        name="mixer_a",
    )(x, mod, win, glnw, glnb, wsp, bsp, cos, sin, mtab, zf, zb, xib, cdb, gnw)


def _mixer_b_body(nck, yp_ref, qr_ref, kvf_ref, p1_ref, cr_ref, x_ref, mod_ref, xif_ref, cdf_ref,
                  wout_ref, ln1w_ref, ln1b_ref, wr_ref, br_ref,
                  x1_ref, h2_ref, lg_ref, sf_ref):
    @pl.when(pl.program_id(1) == 0)
    def _():
        sf_ref[...] = jnp.zeros_like(sf_ref)

    ys = []
    for c in range(nck):
        rows = slice(c * CHUNK, (c + 1) * CHUNK)
        sf = sf_ref[...]
        sfb = sf.astype(bf16)
        cross = jnp.concatenate(
            [jnp.dot(qr_ref[rows, _head(h_)], sfb[:, _head(h_)], preferred_element_type=f32)
             for h_ in range(HEADS)], axis=1)
        ys.append(yp_ref[rows, :].astype(f32) + cross * xif_ref[...])
        sf_ref[...] = sf * cdf_ref[...] + kvf_ref[c].astype(f32)
    y = jnp.concatenate(ys, axis=0)
    yn = jnp.concatenate([_normalize(y[:, _head(h_)], GN_EPS) for h_ in range(HEADS)], axis=1)
    merged = p1_ref[...].astype(f32) + cr_ref[...].astype(f32) * yn
    mix = jnp.dot(merged.astype(bf16), wout_ref[...], preferred_element_type=f32)
    x1 = _normalize(DN_ALPHA * x_ref[...] + mod_ref[2:3, :] * mix, LN_EPS) * ln1w_ref[...] + ln1b_ref[...]
    x1_ref[...] = x1
    h2 = x1 * (1.0 + mod_ref[4:5, :]) + mod_ref[3:4, :]
    h2_hi = h2.astype(bf16)
    h2_ref[...] = h2_hi
    h2_lo = (h2 - h2_hi.astype(f32)).astype(bf16)
    hi_terms = jnp.dot(h2_hi, wr_ref[...], preferred_element_type=f32)
    lo_term = jnp.dot(h2_lo, wr_ref[:, :LANES], preferred_element_type=f32)
    lg_ref[...] = hi_terms[:, :LANES] + hi_terms[:, LANES:] + lo_term + br_ref[...]


def _mixer_b(yp, qr, kvf, p1, cr, x, mod, xif, cdf, wout, ln1w, ln1b, wr, br):
    b, s, _ = x.shape
    t = MIX_TOKENS
    nt, nck = s // t, t // CHUNK
    fwd = lambda bi, i: (bi, i, 0)
    flat = lambda bi, i: (bi * nt + i, 0)
    return pl.pallas_call(
        functools.partial(_mixer_b_body, nck),
        grid=(b, nt),
        in_specs=[pl.BlockSpec((None, t, D), fwd), pl.BlockSpec((None, t, D), fwd),
                  pl.BlockSpec((None, nck, CHUNK, D), lambda bi, i: (bi, i, 0, 0)),
                  pl.BlockSpec((None, t, D), fwd), pl.BlockSpec((None, t, D), fwd),
                  pl.BlockSpec((None, t, D), fwd),
                  pl.BlockSpec((None, 6, D), lambda bi, i: (bi, 0, 0)),
                  _const_spec((CHUNK, D)), _const_spec((1, D)),
                  _const_spec((D, D)), _const_spec((1, D)), _const_spec((1, D)),
                  _const_spec((D, 2 * LANES)), _const_spec((1, LANES))],
        out_specs=[pl.BlockSpec((None, t, D), fwd),
                   pl.BlockSpec((t, D), flat),
                   pl.BlockSpec((t, LANES), flat)],
        out_shape=[jax.ShapeDtypeStruct((b, s, D), f32),
                   jax.ShapeDtypeStruct((b * s, D), bf16),
                   jax.ShapeDtypeStruct((b * s, LANES), f32)],
        scratch_shapes=[pltpu.VMEM((CHUNK, D), f32)],
        compiler_params=pltpu.CompilerParams(dimension_semantics=("arbitrary", "arbitrary"),
                                             vmem_limit_bytes=VMEM_LIMIT),
        name="mixer_b",
    )(yp, qr, kvf, p1, cr, x, mod, xif, cdf, wout, ln1w, ln1b, wr, br)


def _route_body(lg_ref, idx_ref, w_ref, rank_ref, cnt_ref):
    tt = SORT_TOKENS
    for j in range(lg_ref.shape[0] // tt):
        cols = slice(j * tt, (j + 1) * tt)
        idx, w, rank, cnt = _route_tile(lg_ref[cols, :])
        idx_ref[:, cols] = idx
        w_ref[:, cols] = w
        rank_ref[:, cols] = rank
        cnt_ref[j] = cnt


def _route_tile(logits):
    tt = logits.shape[0]
    l = logits.T[:E, :]
    iota_e = lax.broadcasted_iota(i32, (E, tt), 0)
    vals, idxs, hots = [], [], []
    for _ in range(TOPK):
        m = jnp.max(l, axis=0, keepdims=True)
        idx = jnp.min(jnp.where(l == m, iota_e, E), axis=0, keepdims=True)
        hot = iota_e == idx
        l = jnp.where(hot, -jnp.inf, l)
        vals.append(m)
        idxs.append(idx)
        hots.append(hot)
    exps = [jnp.exp(v_ - vals[0]) for v_ in vals]
    tot = exps[0] + exps[1] + exps[2] + exps[3]
    w = jnp.concatenate([e_ / tot for e_ in exps], axis=0)

    sel = jnp.zeros((E, tt), f32)
    for hot in hots:
        sel = sel + hot.astype(f32)
    upper = (lax.broadcasted_iota(i32, (tt, tt), 0) < lax.broadcasted_iota(i32, (tt, tt), 1)).astype(bf16)
    rank_e = jnp.dot(sel.astype(bf16), upper, preferred_element_type=f32)
    ranks = [jnp.sum(jnp.where(hot, rank_e, 0.0), axis=0, keepdims=True) for hot in hots]
    cnt = jnp.broadcast_to(jnp.sum(sel, axis=1, keepdims=True), (E, LANES))
    return jnp.concatenate(idxs, axis=0), w, jnp.concatenate(ranks, axis=0).astype(i32), cnt


def _route(logits):
    n = logits.shape[0]
    tt = SORT_TOKENS * ROUTE_TILES
    col = lambda i: (0, i)
    return pl.pallas_call(
        _route_body,
        grid=(n // tt,),
        in_specs=[pl.BlockSpec((tt, LANES), lambda i: (i, 0))],
        out_specs=[pl.BlockSpec((TOPK, tt), col), pl.BlockSpec((TOPK, tt), col), pl.BlockSpec((TOPK, tt), col),
                   pl.BlockSpec((ROUTE_TILES, E, LANES), lambda i: (i, 0, 0))],
        out_shape=[jax.ShapeDtypeStruct((TOPK, n), i32), jax.ShapeDtypeStruct((TOPK, n), f32),
                   jax.ShapeDtypeStruct((TOPK, n), i32), jax.ShapeDtypeStruct((n // SORT_TOKENS, E, LANES), f32)],
        compiler_params=pltpu.CompilerParams(dimension_semantics=("arbitrary",), vmem_limit_bytes=VMEM_LIMIT),
        name="route",
    )(logits)


def _unit_copy(src, dst, sem):
    return pltpu.make_async_copy(src, dst, sem)


def _start_runs(tile, runs_ref, start):
    def per_run(e_, carry):
        k_ = (tile * E + e_) * 3
        units = runs_ref[k_]
        lrow = pl.multiple_of(runs_ref[k_ + 1], UNIT)
        srow = pl.multiple_of(runs_ref[k_ + 2], UNIT)

        def large(c, cc):
            off = pl.multiple_of(c * (4 * UNIT), 4 * UNIT)
            start(lrow + off, srow + off, 4 * UNIT, 0)
            return cc

        lax.fori_loop(0, units // 4, large, 0)
        off2 = pl.multiple_of((units // 4) * (4 * UNIT), 4 * UNIT)

        @pl.when(units & 2 != 0)
        def _():
            start(lrow + off2, srow + off2, 2 * UNIT, 1)

        off1 = pl.multiple_of(off2 + (units & 2) * UNIT, UNIT)

        @pl.when(units & 1 != 0)
        def _():
            start(lrow + off1, srow + off1, UNIT, 1)

        return carry

    lax.fori_loop(0, E, per_run, 0)


def _wait_units(n_units, wait_rows):
    for bit in reversed(range(MAX_UNITS.bit_length())):
        @pl.when((n_units >> bit) & 1 == 1)
        def _():
            wait_rows((1 << bit) * UNIT)


def _in_group(i, bound):
    return (i >= bound[0]) & (i < bound[1])


def _group_block(i, bound):
    return jnp.clip(i - bound[0], 0, bound[1] - bound[0] - 1)


def _group_value(i, bounds, refs):
    value = refs[-1][...]
    for ref, bound in zip(reversed(refs[:-1]), reversed(bounds[:-1])):
        value = jnp.where(_in_group(i, bound), ref[...], value)
    return value


def _dispatch_body(bounds, runs_ref, tunits_ref, zstart_ref, zunits_ref, lpos_ref, *refs):
    h2_refs = refs[:len(bounds)]
    xbuf_ref, xs_ref, zero_ref, sem, zsem = refs[len(bounds):]
    tt = SORT_TOKENS
    i = pl.program_id(0)
    last = pl.num_programs(0) - 1
    slot = i % 2

    def start(lrow, srow, rows, priority):
        _unit_copy(xs_ref.at[slot, pl.ds(pl.multiple_of(lrow, UNIT), rows)],
                   xbuf_ref.at[pl.ds(pl.multiple_of(srow, UNIT), rows)], sem.at[slot]).start(priority=priority)

    def drain(tile, slot_):
        _wait_units(tunits_ref[tile], lambda r: _unit_copy(
            xs_ref.at[slot_, pl.ds(0, r)], xbuf_ref.at[pl.ds(0, r)], sem.at[slot_]).wait())

    @pl.when(i >= 2)
    def _():
        drain(i - 2, slot)

    lpos = lpos_ref[...]
    rows = lax.broadcasted_iota(i32, (SORT_ROWS, tt), 0)
    perm = jnp.zeros((SORT_ROWS, tt), f32)
    for k_ in range(TOPK):
        perm = jnp.where(rows == lpos[k_:k_ + 1, :], 1.0, perm)
    xs_ref[slot] = jnp.dot(perm.astype(bf16), _group_value(i, bounds, h2_refs), preferred_element_type=f32)
    _start_runs(i, runs_ref, start)

    @pl.when(i == last)
    def _():
        zero_ref[...] = jnp.zeros_like(zero_ref)

        def zero_copy(srow):
            return _unit_copy(zero_ref, xbuf_ref.at[pl.ds(srow, UNIT)], zsem)

        def fill(e_, carry):
            def start_one(u, c):
                zero_copy(pl.multiple_of(zstart_ref[e_] + u * UNIT, UNIT)).start()
                return c

            lax.fori_loop(0, zunits_ref[e_], start_one, 0)
            return carry

        def fill_wait(e_, carry):
            def wait_one(u, c):
                zero_copy(0).wait()
                return c

            lax.fori_loop(0, zunits_ref[e_], wait_one, 0)
            return carry

        lax.fori_loop(0, E + 1, fill, 0)
        lax.fori_loop(0, E + 1, fill_wait, 0)
        drain(i, slot)

        @pl.when(i >= 1)
        def _():
            drain(i - 1, 1 - slot)


def _tile_bounds(sizes):
    ends = np.cumsum([n // SORT_TOKENS for n in sizes])
    return tuple((int(e - n // SORT_TOKENS), int(e)) for e, n in zip(ends, sizes))


def _dispatch(h2s, lpos, runs, tunits, zstart, zunits, n_rows):
    tt = SORT_TOKENS
    bounds = _tile_bounds([h.shape[0] for h in h2s])
    return pl.pallas_call(
        functools.partial(_dispatch_body, bounds),
        grid_spec=pltpu.PrefetchScalarGridSpec(
            num_scalar_prefetch=4,
            grid=(bounds[-1][1],),
            in_specs=[pl.BlockSpec((TOPK, tt), lambda i, *_: (0, i))]
                     + [pl.BlockSpec((tt, D), lambda i, *_, bound=bound: (_group_block(i, bound), 0))
                        for bound in bounds],
            out_specs=pl.BlockSpec(memory_space=pl.ANY),
            scratch_shapes=[pltpu.VMEM((2, SORT_ROWS, D), f32), pltpu.VMEM((UNIT, D), f32),
                            pltpu.SemaphoreType.DMA((2,)), pltpu.SemaphoreType.DMA(())]),
        out_shape=jax.ShapeDtypeStruct((n_rows, D), f32),
        compiler_params=pltpu.CompilerParams(dimension_semantics=("arbitrary",), vmem_limit_bytes=VMEM_LIMIT),
        name="dispatch",
    )(runs, tunits, zstart, zunits, lpos, *h2s)


def _expert_body(be_ref, nused_ref, x_ref, wgu_ref, bgu_ref, wd_ref, bd_ref, y_ref, wgu_bf_ref, wd_bf_ref):
    b_ = pl.program_id(0)

    @pl.when(b_ >= nused_ref[0])
    def _():
        y_ref[...] = jnp.zeros_like(y_ref)

    @pl.when((b_ == 0) | ((b_ < nused_ref[0]) & (be_ref[b_] != be_ref[jnp.maximum(b_ - 1, 0)])))
    def _():
        wgu_bf_ref[...] = wgu_ref[...].astype(bf16)
        wd_bf_ref[...] = wd_ref[...].astype(bf16)

    @pl.when(b_ < nused_ref[0])
    def _():
        gu = jnp.dot(x_ref[...].astype(bf16), wgu_bf_ref[...], preferred_element_type=f32) + bgu_ref[...]
        gate = jnp.minimum(gu[:, :FF], SWIGLU_LIMIT)
        up = jnp.clip(gu[:, FF:], -SWIGLU_LIMIT, SWIGLU_LIMIT)
        act = (up + 1.0) * gate * jax.nn.sigmoid(SWIGLU_ALPHA * gate)
        y_ref[...] = jnp.dot(act.astype(bf16), wd_bf_ref[...], preferred_element_type=f32) + bd_ref[...]


def _experts(block_e, n_used, xbuf, wgu, bgu, wd, bd):
    n_rows = xbuf.shape[0]
    r = EXPERT_ROWS
    blk = lambda b_, nu: jnp.minimum(b_, nu[0] - 1)
    wspec = lambda shape: pl.BlockSpec((None,) + shape, lambda b_, be, nu: (be[blk(b_, nu)], 0, 0))
    return pl.pallas_call(
        _expert_body,
        grid_spec=pltpu.PrefetchScalarGridSpec(
            num_scalar_prefetch=2,
            grid=(n_rows // r,),
            in_specs=[pl.BlockSpec((r, D), lambda b_, be, nu: (blk(b_, nu), 0)),
                      wspec((D, 2 * FF)), wspec((1, 2 * FF)), wspec((FF, D)), wspec((1, D))],
            out_specs=pl.BlockSpec((r, D), lambda b_, be, nu: (b_, 0)),
            scratch_shapes=[pltpu.VMEM((D, 2 * FF), bf16), pltpu.VMEM((FF, D), bf16)]),
        out_shape=jax.ShapeDtypeStruct((n_rows, D), f32),
        compiler_params=pltpu.CompilerParams(dimension_semantics=("arbitrary",), vmem_limit_bytes=VMEM_LIMIT),
        name="experts",
    )(block_e, n_used, xbuf, wgu, bgu, wd, bd)


def _combine_body(bounds, runs_ref, tunits_ref,
                  lpos_ref, w_ref, mod_ref, ln2w_ref, ln2b_ref, ybuf_ref, *refs):
    n_groups = len(bounds)
    x1_refs, o_refs = refs[:n_groups], refs[n_groups:2 * n_groups]
    ys_ref, sem = refs[2 * n_groups:]
    tt = SORT_TOKENS
    i = pl.program_id(0)
    slot = i % 2

    def fetch(tile, slot_):
        def start(lrow, srow, rows, priority):
            _unit_copy(ybuf_ref.at[pl.ds(pl.multiple_of(srow, UNIT), rows)],
                       ys_ref.at[slot_, pl.ds(pl.multiple_of(lrow, UNIT), rows)],
                       sem.at[slot_]).start(priority=priority)

        _start_runs(tile, runs_ref, start)

    @pl.when(i == 0)
    def _():
        ys_ref[...] = jnp.zeros_like(ys_ref)
        fetch(0, 0)

    @pl.when(i + 1 < pl.num_programs(0))
    def _():
        fetch(i + 1, 1 - slot)

    _wait_units(tunits_ref[i], lambda r: _unit_copy(
        ybuf_ref.at[pl.ds(0, r)], ys_ref.at[slot, pl.ds(0, r)], sem.at[slot]).wait())

    lpos, w = lpos_ref[...], w_ref[...]
    cols = lax.broadcasted_iota(i32, (tt, SORT_ROWS), 1)
    wm = jnp.zeros((tt, SORT_ROWS), f32)
    for k_ in range(TOPK):
        wm = jnp.where(cols == lpos[:, k_:k_ + 1], w[:, k_:k_ + 1], wm)
    ff = jnp.dot(wm.astype(bf16), ys_ref[slot].astype(bf16), preferred_element_type=f32)
    z = DN_ALPHA * _group_value(i, bounds, x1_refs) + mod_ref[5:6, :] * ff
    out = _normalize(z, LN_EPS) * ln2w_ref[...] + ln2b_ref[...]
    for o_ref, bound in zip(o_refs, bounds):
        @pl.when(_in_group(i, bound))
        def _():
            o_ref[...] = out


def _combine(runs, tunits, lpos_t, w_t, mod, ln2w, ln2b, ybuf, x1s, seqs):
    tt = SORT_TOKENS
    bounds = _tile_bounds([x.shape[0] for x in x1s])
    tok = lambda width: pl.BlockSpec((tt, width), lambda i, *_: (i, 0))
    grp = [pl.BlockSpec((tt, D), lambda i, *_, bound=bound: (_group_block(i, bound), 0)) for bound in bounds]

    def mod_row(i, *_):
        row, base = 0, 0
        for bound, x, seq in zip(bounds, x1s, seqs):
            row = jnp.where(_in_group(i, bound), base + (i - bound[0]) * tt // seq, row)
            base += x.shape[0] // seq
        return (row, 0, 0)

    return pl.pallas_call(
        functools.partial(_combine_body, bounds),
        grid_spec=pltpu.PrefetchScalarGridSpec(
            num_scalar_prefetch=2,
            grid=(bounds[-1][1],),
            in_specs=[tok(TOPK), tok(TOPK), pl.BlockSpec((None, 6, D), mod_row),
                      _const_spec((1, D)), _const_spec((1, D)), pl.BlockSpec(memory_space=pl.ANY)] + grp,
            out_specs=grp,
            scratch_shapes=[pltpu.VMEM((2, SORT_ROWS, D), f32), pltpu.SemaphoreType.DMA((2,))]),
        out_shape=[jax.ShapeDtypeStruct(x.shape, f32) for x in x1s],
        compiler_params=pltpu.CompilerParams(dimension_semantics=("arbitrary",), vmem_limit_bytes=VMEM_LIMIT),
        name="combine",
    )(runs, tunits, lpos_t, w_t, mod, ln2w, ln2b, ybuf, *x1s)


def _rope_tables(s):
    half = HD // 2
    nf32 = np.float32
    inv_freq = np.power(nf32(ROPE_BASE), (nf32(-2.0) * np.arange(half, dtype=nf32) / nf32(HD)).astype(nf32)).astype(nf32)
    ang = (np.arange(s, dtype=nf32)[:, None] * inv_freq[None, :]).astype(nf32).astype(np.float64)
    cos, sin = np.cos(ang).astype(nf32), np.sin(ang).astype(nf32)
    return jnp.asarray(np.concatenate([cos, cos], axis=1)), jnp.asarray(np.concatenate([-sin, sin], axis=1))


def _retention_tables(theta_f, theta_b):
    lg_f = jax.nn.log_sigmoid(theta_f.astype(f32))
    lg_b = jax.nn.log_sigmoid(theta_b.astype(f32))
    idx = jnp.arange(CHUNK, dtype=f32)
    diff = idx[:, None] - idx[None, :]
    dec_f = jnp.where(diff >= 0, jnp.exp(jnp.maximum(diff, 0.0)[None] * lg_f[:, None, None]), 0.0)
    dec_b = jnp.where(diff < 0, jnp.exp(jnp.maximum(-diff, 0.0)[None] * lg_b[:, None, None]), 0.0)
    kscale = HD ** -0.5
    mtab = ((dec_f + dec_b) * kscale).transpose(1, 0, 2).reshape(CHUNK, D)
    lanes = lambda t: jnp.repeat(t.T, HD, axis=1)
    zf = lanes(jnp.exp((CHUNK - 1.0 - idx)[None, :] * lg_f[:, None]) * kscale)
    zb = lanes(jnp.exp(idx[None, :] * lg_b[:, None]) * kscale)
    xif = lanes(jnp.exp((idx + 1.0)[None, :] * lg_f[:, None]))
    xib = lanes(jnp.exp((CHUNK - idx)[None, :] * lg_b[:, None]))
    cdf = jnp.repeat(jnp.exp(CHUNK * lg_f), HD)[None, :]
    cdb = jnp.repeat(jnp.exp(CHUNK * lg_b), HD)[None, :]
    return (mtab, zf, zb, xib, cdb), (xif, cdf)


def _hi_lo(a):
    hi = a.astype(bf16)
    return jnp.concatenate([hi, (a - hi.astype(f32)).astype(bf16)], axis=1)


def _mixer(x, mod, p):
    p1, cr, yp, qr, kvf = _mixer_a(x, mod, p["win"], p["glnw"], p["glnb"], p["wsp"], p["bsp"],
                                   _rope_tables(x.shape[1]), p["tabs_a"], p["gnw"])
    xif, cdf = p["tabs_b"]
    return _mixer_b(yp, qr, kvf, p1, cr, x, mod, xif, cdf, p["wout"], p["ln1w"], p["ln1b"], p["wr"], p["br"])


def _moe_ln2(x1s, h2s, logits, mod, p):
    routed = [_route(lg) for lg in logits]
    idx, w, lrank = (jnp.concatenate([r[j] for r in routed], axis=1) for j in range(3))
    cnt = jnp.concatenate([r[3] for r in routed], axis=0)
    n = idx.shape[1]

    tt = SORT_TOKENS
    nt = n // tt
    counts = cnt[:, :, 0].astype(i32)
    run = (counts + UNIT - 1) // UNIT * UNIT
    loff = jnp.cumsum(run, axis=1) - run
    total = jnp.sum(run, axis=0)
    padded = (total + EXPERT_ROWS - 1) // EXPERT_ROWS * EXPERT_ROWS
    pends = jnp.cumsum(padded)
    pstarts = pends - padded
    seg = pstarts[None, :] + jnp.cumsum(run, axis=0) - run
    n_blocks = (n * TOPK + nt * E * (UNIT - 1)) // EXPERT_ROWS + E
    n_used = (pends[E - 1:] // EXPERT_ROWS).astype(i32)
    block_start = jnp.arange(n_blocks, dtype=i32) * EXPERT_ROWS
    block_e = jnp.minimum(jnp.sum((pends[None, :] <= block_start[:, None]).astype(i32), axis=1), E - 1)
    lpos = lrank.reshape(TOPK, nt, tt)
    idx3 = idx.reshape(TOPK, nt, tt)
    for e_ in range(E):
        lpos = lpos + jnp.where(idx3 == e_, loff[None, :, e_, None], 0)
    lpos = lpos.reshape(TOPK, n)
    flat = lambda a: a.reshape(-1).astype(i32)
    runs = flat(jnp.stack([run // UNIT, loff, seg], axis=-1))
    tunits = flat(jnp.sum(run, axis=1) // UNIT)

    n_rows = n_blocks * EXPERT_ROWS
    zstart = jnp.concatenate([pstarts + total, pends[E - 1:]])
    zunits = jnp.concatenate([padded - total, n_rows - pends[E - 1:]]) // UNIT
    xbuf = _dispatch(h2s, lpos, runs, tunits, flat(zstart), flat(zunits), n_rows)
    ybuf = _experts(block_e.astype(i32), n_used, xbuf, p["wgu"], p["bgu"], p["wd"], p["bd"])
    outs = _combine(runs, tunits, lpos.T, w.T, mod, p["ln2w"], p["ln2b"], ybuf,
                    [x.reshape(-1, D) for x in x1s], [x.shape[1] for x in x1s])
    return [o.reshape(x.shape) for o, x in zip(outs, x1s)]


def kernel(x_prompt, x_sample, c_prompt, c_sample, w_ada, b_ada, w_in, gmlp_ln_w, gmlp_ln_b, w_spatial, b_spatial, ret_theta_fwd, ret_theta_bwd, ret_gn_w, w_out, ln1_w, ln1_b, w_router, b_router, w_gate_up, b_gate_up, w_down, b_down, ln2_w, ln2_b):
    l = 0
    bp, bs = x_prompt.shape[0], x_sample.shape[0]
    c_all = jnp.concatenate([c_prompt, c_sample], axis=0)
    c_rows = -(-c_all.shape[0] // SUBLANES) * SUBLANES
    c_all = jnp.pad(c_all, ((0, c_rows - c_all.shape[0]), (0, 0)))
    mod = _adaln(c_all, w_ada[l], b_ada[l]).reshape(c_rows, 6, D)

    tabs_a, tabs_b = _retention_tables(ret_theta_fwd[l], ret_theta_bwd[l])
    row = lambda a: a.reshape(1, -1)
    p = dict(
        win=w_in[l].astype(bf16), glnw=row(gmlp_ln_w[l]), glnb=row(gmlp_ln_b[l]),
        wsp=w_spatial[l].astype(bf16), bsp=jnp.repeat(b_spatial[l].T, HD, axis=1),
        tabs_a=tabs_a, tabs_b=tabs_b, gnw=row(ret_gn_w[l]),
        wout=w_out[l].astype(bf16), ln1w=row(ln1_w[l]), ln1b=row(ln1_b[l]),
        wr=_hi_lo(jnp.pad(w_router[l], ((0, 0), (0, LANES - E)))), br=jnp.pad(row(b_router[l]), ((0, 0), (0, LANES - E))),
        wgu=w_gate_up[l], bgu=b_gate_up[l].reshape(E, 1, 2 * FF),
        wd=w_down[l], bd=b_down[l].reshape(E, 1, D),
        ln2w=row(ln2_w[l]), ln2b=row(ln2_b[l]),
    )
    mixed = [_mixer(x_prompt, mod[:bp], p), _mixer(x_sample, mod[bp:bp + bs], p)]
    x1s, h2s, logits = zip(*mixed)
    y_prompt, y_sample = _moe_ln2(x1s, h2s, logits, mod, p)
    return (y_prompt, y_sample)
```

```python
import functools

import numpy as np
import jax
import jax.numpy as jnp
from jax import lax
from jax.experimental import pallas as pl
from jax.experimental.pallas import tpu as pltpu

f32 = jnp.float32
bf16 = jnp.bfloat16
i32 = jnp.int32

D = 1024
CHUNK = 128
HEADS = 8
HD = D // HEADS
N_SEG = 8
E = 32
TOPK = 4
FF = 1024
ROPE_BASE = 10000.0
SWIGLU_LIMIT = 7.0
SWIGLU_ALPHA = 1.702
LN_EPS = 1e-5
GN_EPS = 1e-6
DEPTH = 1
DN_ALPHA = (2 * DEPTH) ** 0.25

LANES = 128
SUBLANES = 8

MIX_TOKENS = 512
SORT_TOKENS = 512
UNIT = SUBLANES
SORT_ROWS = -(-(SORT_TOKENS * TOPK + E * (UNIT - 1)) // LANES) * LANES
MAX_UNITS = SORT_ROWS // UNIT
EXPERT_ROWS = 512
VMEM_LIMIT = 56 * 1024 * 1024

_HI = lax.Precision.HIGHEST


def _const_spec(shape):
    nd = len(shape)
    return pl.BlockSpec(shape, lambda *_: (0,) * nd, pipeline_mode=pl.Buffered(1))


def _gelu(x):
    return 0.5 * x * (1.0 + lax.erf(x * (2.0 ** -0.5)))


def _normalize(x, eps):
    mu = jnp.mean(x, axis=-1, keepdims=True)
    xc = x - mu
    var = jnp.mean(xc * xc, axis=-1, keepdims=True)
    return xc * lax.rsqrt(var + eps)


def _silu(x):
    return x * jax.nn.sigmoid(x)


def _head(h):
    return slice(h * HD, (h + 1) * HD)


def _adaln_body(c_ref, w_ref, b_ref, o_ref):
    s = _silu(c_ref[...])
    o_ref[...] = jnp.dot(s, w_ref[...], preferred_element_type=f32, precision=_HI) + b_ref[...]


def _adaln(c, w_ada, b_ada):
    rows = c.shape[0]
    return pl.pallas_call(
        _adaln_body,
        grid=(6,),
        in_specs=[pl.BlockSpec((rows, D), lambda j: (0, 0)),
                  pl.BlockSpec((D, D), lambda j: (0, j)),
                  pl.BlockSpec((1, D), lambda j: (0, j))],
        out_specs=pl.BlockSpec((rows, D), lambda j: (0, j)),
        out_shape=jax.ShapeDtypeStruct((rows, 6 * D), f32),
        compiler_params=pltpu.CompilerParams(vmem_limit_bytes=VMEM_LIMIT),
        name="adaln",
    )(c, w_ada, b_ada.reshape(1, 6 * D))


def _mixer_a_body(nck, x_ref, mod_ref, win_ref, glnw_ref, glnb_ref, wsp_ref, bsp_ref,
                  cos_ref, sin_ref, mtab_ref, zf_ref, zb_ref, xib_ref, cdb_ref, gnw_ref,
                  p1_ref, cr_ref, yp_ref, qr_ref, kvf_ref, sb_ref):
    @pl.when(pl.program_id(1) == 0)
    def _():
        sb_ref[...] = jnp.zeros_like(sb_ref)

    h = (x_ref[...] * (1.0 + mod_ref[1:2, :]) + mod_ref[0:1, :]).astype(bf16)

    def seg(j):
        return jnp.dot(h, win_ref[:, j * D:(j + 1) * D], preferred_element_type=f32)

    vn = _normalize(_gelu(seg(1)), LN_EPS) * glnw_ref[...] + glnb_ref[...]
    vnb = vn.astype(bf16)
    ug = _gelu(seg(0))
    ga = jax.nn.sigmoid(seg(6))
    for c in range(nck):
        rows = slice(c * CHUNK, (c + 1) * CHUNK)
        mixed = jnp.concatenate(
            [jnp.dot(wsp_ref[g], vnb[rows, _head(g)], preferred_element_type=f32) for g in range(HEADS)], axis=1)
        p1_ref[rows, :] = (ga[rows, :] * (ug[rows, :] * (mixed + bsp_ref[...]))).astype(bf16)

    q = seg(2)
    k = seg(3)
    v = seg(4).astype(bf16)
    cos, sin = cos_ref[...], sin_ref[...]

    def rotary(t):
        return jnp.concatenate([t[:, _head(h_)] * cos + pltpu.roll(t[:, _head(h_)], HD // 2, 1) * sin
                                for h_ in range(HEADS)], axis=1)

    qf = rotary(q)
    kr = rotary(k)
    qr = qf.astype(bf16)
    qr_ref[...] = qr
    for c in reversed(range(nck)):
        rows = slice(c * CHUNK, (c + 1) * CHUNK)
        kc = kr[rows, :]
        kcb = kc.astype(bf16)
        kzf = (kc * zf_ref[...]).astype(bf16)
        kzb = (kc * zb_ref[...]).astype(bf16)
        qx = (qf[rows, :] * xib_ref[...]).astype(bf16)
        sb = sb_ref[...]
        sbb = sb.astype(bf16)
        yps, kvfs, kvbs = [], [], []
        for h_ in range(HEADS):
            hs = _head(h_)
            vh = v[rows, hs]
            sc = lax.dot_general(qr[rows, hs], kcb[:, hs], (((1,), (1,)), ((), ())), preferred_element_type=f32)
            sc = (sc * mtab_ref[:, hs]).astype(bf16)
            yps.append(jnp.dot(sc, vh, preferred_element_type=f32)
                       + jnp.dot(qx[:, hs], sbb[:, hs], preferred_element_type=f32))
            kvfs.append(lax.dot_general(kzf[:, hs], vh, (((0,), (0,)), ((), ())), preferred_element_type=f32))
            kvbs.append(lax.dot_general(kzb[:, hs], vh, (((0,), (0,)), ((), ())), preferred_element_type=f32))
        yp_ref[rows, :] = jnp.concatenate(yps, axis=1).astype(bf16)
        kvf_ref[c] = jnp.concatenate(kvfs, axis=1).astype(bf16)
        sb_ref[...] = sb * cdb_ref[...] + jnp.concatenate(kvbs, axis=1)

    cr_ref[...] = (jax.nn.sigmoid(seg(7)) * _silu(seg(5)) * gnw_ref[...]).astype(bf16)


def _mixer_a(x, mod, win, glnw, glnb, wsp, bsp, rope, tabs, gnw):
    b, s, _ = x.shape
    t = MIX_TOKENS
    nt, nck = s // t, t // CHUNK
    cos, sin = rope
    mtab, zf, zb, xib, cdb = tabs
    rev = lambda bi, i: (bi, nt - 1 - i, 0)
    rope_spec = pl.BlockSpec((t, HD), lambda bi, i: (nt - 1 - i, 0))
    tok = lambda dt: jax.ShapeDtypeStruct((b, s, D), dt)
    return pl.pallas_call(
        functools.partial(_mixer_a_body, nck),
        grid=(b, nt),
        in_specs=[pl.BlockSpec((None, t, D), rev),
                  pl.BlockSpec((None, 6, D), lambda bi, i: (bi, 0, 0)),
                  _const_spec((D, N_SEG * D)),
                  _const_spec((1, D)), _const_spec((1, D)),
                  _const_spec((HEADS, CHUNK, CHUNK)), _const_spec((CHUNK, D)),
                  rope_spec, rope_spec,
                  _const_spec((CHUNK, D)), _const_spec((CHUNK, D)), _const_spec((CHUNK, D)),
                  _const_spec((CHUNK, D)), _const_spec((1, D)), _const_spec((1, D))],
        out_specs=[pl.BlockSpec((None, t, D), rev)] * 4
                  + [pl.BlockSpec((None, nck, CHUNK, D), lambda bi, i: (bi, nt - 1 - i, 0, 0))],
        out_shape=[tok(bf16), tok(bf16), tok(bf16), tok(bf16),
                   jax.ShapeDtypeStruct((b, s // CHUNK, CHUNK, D), bf16)],
        scratch_shapes=[pltpu.VMEM((CHUNK, D), f32)],
        compiler_params=pltpu.CompilerParams(dimension_semantics=("arbitrary", "arbitrary"),
                                             vmem_limit_bytes=VMEM_LIMIT),
        name="mixer_a",
    )(x, mod, win, glnw, glnb, wsp, bsp, cos, sin, mtab, zf, zb, xib, cdb, gnw)


def _mixer_b_body(nck, yp_ref, qr_ref, kvf_ref, p1_ref, cr_ref, x_ref, mod_ref, xif_ref, cdf_ref,
                  wout_ref, ln1w_ref, ln1b_ref, wr_ref, br_ref,
                  x1_ref, h2_ref, idx_ref, w_ref, rank_ref, cnt_ref, sf_ref):
    @pl.when(pl.program_id(1) == 0)
    def _():
        sf_ref[...] = jnp.zeros_like(sf_ref)

    ys = []
    for c in range(nck):
        rows = slice(c * CHUNK, (c + 1) * CHUNK)
        sf = sf_ref[...]
        sfb = sf.astype(bf16)
        cross = jnp.concatenate(
            [jnp.dot(qr_ref[rows, _head(h_)], sfb[:, _head(h_)], preferred_element_type=f32)
             for h_ in range(HEADS)], axis=1)
        ys.append(yp_ref[rows, :].astype(f32) + cross * xif_ref[...])
        sf_ref[...] = sf * cdf_ref[...] + kvf_ref[c].astype(f32)
    y = jnp.concatenate(ys, axis=0)
    yn = jnp.concatenate([_normalize(y[:, _head(h_)], GN_EPS) for h_ in range(HEADS)], axis=1)
    merged = p1_ref[...].astype(f32) + cr_ref[...].astype(f32) * yn
    mix = jnp.dot(merged.astype(bf16), wout_ref[...], preferred_element_type=f32)
    x1 = _normalize(DN_ALPHA * x_ref[...] + mod_ref[2:3, :] * mix, LN_EPS) * ln1w_ref[...] + ln1b_ref[...]
    x1_ref[...] = x1
    h2 = x1 * (1.0 + mod_ref[4:5, :]) + mod_ref[3:4, :]
    h2_hi = h2.astype(bf16)
    h2_ref[...] = h2_hi
    h2_lo = (h2 - h2_hi.astype(f32)).astype(bf16)
    hi_terms = jnp.dot(h2_hi, wr_ref[...], preferred_element_type=f32)
    lo_term = jnp.dot(h2_lo, wr_ref[:, :LANES], preferred_element_type=f32)
    logits = hi_terms[:, :LANES] + hi_terms[:, LANES:] + lo_term + br_ref[...]
    idx_ref[...], w_ref[...], rank_ref[...], cnt_ref[...] = _route_tile(logits)


def _mixer_b(yp, qr, kvf, p1, cr, x, mod, xif, cdf, wout, ln1w, ln1b, wr, br):
    b, s, _ = x.shape
    t = MIX_TOKENS
    nt, nck = s // t, t // CHUNK
    assert t == SORT_TOKENS, "a mixer grid step routes exactly one sort tile"
    fwd = lambda bi, i: (bi, i, 0)
    flat = lambda bi, i: (bi * nt + i, 0)
    col = pl.BlockSpec((TOPK, t), lambda bi, i: (0, bi * nt + i))
    return pl.pallas_call(
        functools.partial(_mixer_b_body, nck),
        grid=(b, nt),
        in_specs=[pl.BlockSpec((None, t, D), fwd), pl.BlockSpec((None, t, D), fwd),
                  pl.BlockSpec((None, nck, CHUNK, D), lambda bi, i: (bi, i, 0, 0)),
                  pl.BlockSpec((None, t, D), fwd), pl.BlockSpec((None, t, D), fwd),
                  pl.BlockSpec((None, t, D), fwd),
                  pl.BlockSpec((None, 6, D), lambda bi, i: (bi, 0, 0)),
                  _const_spec((CHUNK, D)), _const_spec((1, D)),
                  _const_spec((D, D)), _const_spec((1, D)), _const_spec((1, D)),
                  _const_spec((D, 2 * LANES)), _const_spec((1, LANES))],
        out_specs=[pl.BlockSpec((None, t, D), fwd),
                   pl.BlockSpec((t, D), flat),
                   col, col, col,
                   pl.BlockSpec((None, E, LANES), lambda bi, i: (bi * nt + i, 0, 0))],
        out_shape=[jax.ShapeDtypeStruct((b, s, D), f32),
                   jax.ShapeDtypeStruct((b * s, D), bf16),
                   jax.ShapeDtypeStruct((TOPK, b * s), i32), jax.ShapeDtypeStruct((TOPK, b * s), f32),
                   jax.ShapeDtypeStruct((TOPK, b * s), i32),
                   jax.ShapeDtypeStruct((b * s // t, E, LANES), f32)],
        scratch_shapes=[pltpu.VMEM((CHUNK, D), f32)],
        compiler_params=pltpu.CompilerParams(dimension_semantics=("arbitrary", "arbitrary"),
                                             vmem_limit_bytes=VMEM_LIMIT),
        name="mixer_b",
    )(yp, qr, kvf, p1, cr, x, mod, xif, cdf, wout, ln1w, ln1b, wr, br)


def _route_tile(logits):
    tt = logits.shape[0]
    l = logits.T[:E, :]
    iota_e = lax.broadcasted_iota(i32, (E, tt), 0)
    vals, idxs, hots = [], [], []
    for _ in range(TOPK):
        m = jnp.max(l, axis=0, keepdims=True)
        idx = jnp.min(jnp.where(l == m, iota_e, E), axis=0, keepdims=True)
        hot = iota_e == idx
        l = jnp.where(hot, -jnp.inf, l)
        vals.append(m)
        idxs.append(idx)
        hots.append(hot)
    exps = [jnp.exp(v_ - vals[0]) for v_ in vals]
    tot = exps[0] + exps[1] + exps[2] + exps[3]
    w = jnp.concatenate([e_ / tot for e_ in exps], axis=0)

    sel = jnp.zeros((E, tt), f32)
    for hot in hots:
        sel = sel + hot.astype(f32)
    upper = (lax.broadcasted_iota(i32, (tt, tt), 0) < lax.broadcasted_iota(i32, (tt, tt), 1)).astype(bf16)
    rank_e = jnp.dot(sel.astype(bf16), upper, preferred_element_type=f32)
    ranks = [jnp.sum(jnp.where(hot, rank_e, 0.0), axis=0, keepdims=True) for hot in hots]
    cnt = jnp.broadcast_to(jnp.sum(sel, axis=1, keepdims=True), (E, LANES))
    return jnp.concatenate(idxs, axis=0), w, jnp.concatenate(ranks, axis=0).astype(i32), cnt


def _unit_copy(src, dst, sem):
    return pltpu.make_async_copy(src, dst, sem)


def _start_runs(tile, runs_ref, start):
    def per_run(e_, carry):
        k_ = (tile * E + e_) * 3
        units = runs_ref[k_]
        lrow = pl.multiple_of(runs_ref[k_ + 1], UNIT)
        srow = pl.multiple_of(runs_ref[k_ + 2], UNIT)

        def large(c, cc):
            off = pl.multiple_of(c * (4 * UNIT), 4 * UNIT)
            start(lrow + off, srow + off, 4 * UNIT, 0)
            return cc

        lax.fori_loop(0, units // 4, large, 0)
        off2 = pl.multiple_of((units // 4) * (4 * UNIT), 4 * UNIT)

        @pl.when(units & 2 != 0)
        def _():
            start(lrow + off2, srow + off2, 2 * UNIT, 1)

        off1 = pl.multiple_of(off2 + (units & 2) * UNIT, UNIT)

        @pl.when(units & 1 != 0)
        def _():
            start(lrow + off1, srow + off1, UNIT, 1)

        return carry

    lax.fori_loop(0, E, per_run, 0)


def _wait_units(n_units, wait_rows):
    for bit in reversed(range(MAX_UNITS.bit_length())):
        @pl.when((n_units >> bit) & 1 == 1)
        def _():
            wait_rows((1 << bit) * UNIT)


def _in_group(i, bound):
    return (i >= bound[0]) & (i < bound[1])


def _group_block(i, bound):
    return jnp.clip(i - bound[0], 0, bound[1] - bound[0] - 1)


def _group_value(i, bounds, refs):
    value = refs[-1][...]
    for ref, bound in zip(reversed(refs[:-1]), reversed(bounds[:-1])):
        value = jnp.where(_in_group(i, bound), ref[...], value)
    return value


def _dispatch_body(bounds, runs_ref, tunits_ref, zstart_ref, zunits_ref, lpos_ref, *refs):
    h2_refs = refs[:len(bounds)]
    xbuf_ref, xs_ref, zero_ref, sem, zsem = refs[len(bounds):]
    tt = SORT_TOKENS
    i = pl.program_id(0)
    last = pl.num_programs(0) - 1
    slot = i % 2

    def start(lrow, srow, rows, priority):
        _unit_copy(xs_ref.at[slot, pl.ds(pl.multiple_of(lrow, UNIT), rows)],
                   xbuf_ref.at[pl.ds(pl.multiple_of(srow, UNIT), rows)], sem.at[slot]).start(priority=priority)

    def drain(tile, slot_):
        _wait_units(tunits_ref[tile], lambda r: _unit_copy(
            xs_ref.at[slot_, pl.ds(0, r)], xbuf_ref.at[pl.ds(0, r)], sem.at[slot_]).wait())

    @pl.when(i >= 2)
    def _():
        drain(i - 2, slot)

    lpos = lpos_ref[...]
    rows = lax.broadcasted_iota(i32, (SORT_ROWS, tt), 0)
    perm = jnp.zeros((SORT_ROWS, tt), f32)
    for k_ in range(TOPK):
        perm = jnp.where(rows == lpos[k_:k_ + 1, :], 1.0, perm)
    xs_ref[slot] = jnp.dot(perm.astype(bf16), _group_value(i, bounds, h2_refs), preferred_element_type=f32)
    _start_runs(i, runs_ref, start)

    @pl.when(i == last)
    def _():
        zero_ref[...] = jnp.zeros_like(zero_ref)

        def zero_copy(srow):
            return _unit_copy(zero_ref, xbuf_ref.at[pl.ds(srow, UNIT)], zsem)

        def fill(e_, carry):
            def start_one(u, c):
                zero_copy(pl.multiple_of(zstart_ref[e_] + u * UNIT, UNIT)).start()
                return c

            lax.fori_loop(0, zunits_ref[e_], start_one, 0)
            return carry

        def fill_wait(e_, carry):
            def wait_one(u, c):
                zero_copy(0).wait()
                return c

            lax.fori_loop(0, zunits_ref[e_], wait_one, 0)
            return carry

        lax.fori_loop(0, E + 1, fill, 0)
        lax.fori_loop(0, E + 1, fill_wait, 0)
        drain(i, slot)

        @pl.when(i >= 1)
        def _():
            drain(i - 1, 1 - slot)


def _tile_bounds(sizes):
    ends = np.cumsum([n // SORT_TOKENS for n in sizes])
    return tuple((int(e - n // SORT_TOKENS), int(e)) for e, n in zip(ends, sizes))


def _dispatch(h2s, lpos, runs, tunits, zstart, zunits, n_rows):
    tt = SORT_TOKENS
    bounds = _tile_bounds([h.shape[0] for h in h2s])
    return pl.pallas_call(
        functools.partial(_dispatch_body, bounds),
        grid_spec=pltpu.PrefetchScalarGridSpec(
            num_scalar_prefetch=4,
            grid=(bounds[-1][1],),
            in_specs=[pl.BlockSpec((TOPK, tt), lambda i, *_: (0, i))]
                     + [pl.BlockSpec((tt, D), lambda i, *_, bound=bound: (_group_block(i, bound), 0))
                        for bound in bounds],
            out_specs=pl.BlockSpec(memory_space=pl.ANY),
            scratch_shapes=[pltpu.VMEM((2, SORT_ROWS, D), f32), pltpu.VMEM((UNIT, D), f32),
                            pltpu.SemaphoreType.DMA((2,)), pltpu.SemaphoreType.DMA(())]),
        out_shape=jax.ShapeDtypeStruct((n_rows, D), f32),
        compiler_params=pltpu.CompilerParams(dimension_semantics=("arbitrary",), vmem_limit_bytes=VMEM_LIMIT),
        name="dispatch",
    )(runs, tunits, zstart, zunits, lpos, *h2s)


def _expert_body(be_ref, nused_ref, x_ref, wgu_ref, bgu_ref, wd_ref, bd_ref, y_ref, wgu_bf_ref, wd_bf_ref):
    b_ = pl.program_id(0)

    @pl.when(b_ >= nused_ref[0])
    def _():
        y_ref[...] = jnp.zeros_like(y_ref)

    @pl.when((b_ == 0) | ((b_ < nused_ref[0]) & (be_ref[b_] != be_ref[jnp.maximum(b_ - 1, 0)])))
    def _():
        wgu_bf_ref[...] = wgu_ref[...].astype(bf16)
        wd_bf_ref[...] = wd_ref[...].astype(bf16)

    @pl.when(b_ < nused_ref[0])
    def _():
        gu = jnp.dot(x_ref[...].astype(bf16), wgu_bf_ref[...], preferred_element_type=f32) + bgu_ref[...]
        gate = jnp.minimum(gu[:, :FF], SWIGLU_LIMIT)
        up = jnp.clip(gu[:, FF:], -SWIGLU_LIMIT, SWIGLU_LIMIT)
        act = (up + 1.0) * gate * jax.nn.sigmoid(SWIGLU_ALPHA * gate)
        y_ref[...] = jnp.dot(act.astype(bf16), wd_bf_ref[...], preferred_element_type=f32) + bd_ref[...]


def _experts(block_e, n_used, xbuf, wgu, bgu, wd, bd):
    n_rows = xbuf.shape[0]
    r = EXPERT_ROWS
    blk = lambda b_, nu: jnp.minimum(b_, nu[0] - 1)
    wspec = lambda shape: pl.BlockSpec((None,) + shape, lambda b_, be, nu: (be[blk(b_, nu)], 0, 0))
    return pl.pallas_call(
        _expert_body,
        grid_spec=pltpu.PrefetchScalarGridSpec(
            num_scalar_prefetch=2,
            grid=(n_rows // r,),
            in_specs=[pl.BlockSpec((r, D), lambda b_, be, nu: (blk(b_, nu), 0)),
                      wspec((D, 2 * FF)), wspec((1, 2 * FF)), wspec((FF, D)), wspec((1, D))],
            out_specs=pl.BlockSpec((r, D), lambda b_, be, nu: (b_, 0)),
            scratch_shapes=[pltpu.VMEM((D, 2 * FF), bf16), pltpu.VMEM((FF, D), bf16)]),
        out_shape=jax.ShapeDtypeStruct((n_rows, D), f32),
        compiler_params=pltpu.CompilerParams(dimension_semantics=("arbitrary",), vmem_limit_bytes=VMEM_LIMIT),
        name="experts",
    )(block_e, n_used, xbuf, wgu, bgu, wd, bd)


def _combine_body(bounds, runs_ref, tunits_ref,
                  lpos_ref, w_ref, mod_ref, ln2w_ref, ln2b_ref, ybuf_ref, *refs):
    n_groups = len(bounds)
    x1_refs, o_refs = refs[:n_groups], refs[n_groups:2 * n_groups]
    ys_ref, sem = refs[2 * n_groups:]
    tt = SORT_TOKENS
    i = pl.program_id(0)
    slot = i % 2

    def fetch(tile, slot_):
        def start(lrow, srow, rows, priority):
            _unit_copy(ybuf_ref.at[pl.ds(pl.multiple_of(srow, UNIT), rows)],
                       ys_ref.at[slot_, pl.ds(pl.multiple_of(lrow, UNIT), rows)],
                       sem.at[slot_]).start(priority=priority)

        _start_runs(tile, runs_ref, start)

    @pl.when(i == 0)
    def _():
        ys_ref[...] = jnp.zeros_like(ys_ref)
        fetch(0, 0)

    @pl.when(i + 1 < pl.num_programs(0))
    def _():
        fetch(i + 1, 1 - slot)

    _wait_units(tunits_ref[i], lambda r: _unit_copy(
        ybuf_ref.at[pl.ds(0, r)], ys_ref.at[slot, pl.ds(0, r)], sem.at[slot]).wait())

    lpos, w = lpos_ref[...], w_ref[...]
    cols = lax.broadcasted_iota(i32, (tt, SORT_ROWS), 1)
    wm = jnp.zeros((tt, SORT_ROWS), f32)
    for k_ in range(TOPK):
        wm = jnp.where(cols == lpos[:, k_:k_ + 1], w[:, k_:k_ + 1], wm)
    ff = jnp.dot(wm.astype(bf16), ys_ref[slot].astype(bf16), preferred_element_type=f32)
    z = DN_ALPHA * _group_value(i, bounds, x1_refs) + mod_ref[5:6, :] * ff
    out = _normalize(z, LN_EPS) * ln2w_ref[...] + ln2b_ref[...]
    for o_ref, bound in zip(o_refs, bounds):
        @pl.when(_in_group(i, bound))
        def _():
            o_ref[...] = out


def _combine(runs, tunits, lpos_t, w_t, mod, ln2w, ln2b, ybuf, x1s, seqs):
    tt = SORT_TOKENS
    bounds = _tile_bounds([x.shape[0] for x in x1s])
    tok = lambda width: pl.BlockSpec((tt, width), lambda i, *_: (i, 0))
    grp = [pl.BlockSpec((tt, D), lambda i, *_, bound=bound: (_group_block(i, bound), 0)) for bound in bounds]

    def mod_row(i, *_):
        row, base = 0, 0
        for bound, x, seq in zip(bounds, x1s, seqs):
            row = jnp.where(_in_group(i, bound), base + (i - bound[0]) * tt // seq, row)
            base += x.shape[0] // seq
        return (row, 0, 0)

    return pl.pallas_call(
        functools.partial(_combine_body, bounds),
        grid_spec=pltpu.PrefetchScalarGridSpec(
            num_scalar_prefetch=2,
            grid=(bounds[-1][1],),
            in_specs=[tok(TOPK), tok(TOPK), pl.BlockSpec((None, 6, D), mod_row),
                      _const_spec((1, D)), _const_spec((1, D)), pl.BlockSpec(memory_space=pl.ANY)] + grp,
            out_specs=grp,
            scratch_shapes=[pltpu.VMEM((2, SORT_ROWS, D), f32), pltpu.SemaphoreType.DMA((2,))]),
        out_shape=[jax.ShapeDtypeStruct(x.shape, f32) for x in x1s],
        compiler_params=pltpu.CompilerParams(dimension_semantics=("arbitrary",), vmem_limit_bytes=VMEM_LIMIT),
        name="combine",
    )(runs, tunits, lpos_t, w_t, mod, ln2w, ln2b, ybuf, *x1s)


def _rope_tables(s):
    half = HD // 2
    nf32 = np.float32
    inv_freq = np.power(nf32(ROPE_BASE), (nf32(-2.0) * np.arange(half, dtype=nf32) / nf32(HD)).astype(nf32)).astype(nf32)
    ang = (np.arange(s, dtype=nf32)[:, None] * inv_freq[None, :]).astype(nf32).astype(np.float64)
    cos, sin = np.cos(ang).astype(nf32), np.sin(ang).astype(nf32)
    return jnp.asarray(np.concatenate([cos, cos], axis=1)), jnp.asarray(np.concatenate([-sin, sin], axis=1))


def _retention_tables(theta_f, theta_b):
    lg_f = jax.nn.log_sigmoid(theta_f.astype(f32))
    lg_b = jax.nn.log_sigmoid(theta_b.astype(f32))
    idx = jnp.arange(CHUNK, dtype=f32)
    diff = idx[:, None] - idx[None, :]
    dec_f = jnp.where(diff >= 0, jnp.exp(jnp.maximum(diff, 0.0)[None] * lg_f[:, None, None]), 0.0)
    dec_b = jnp.where(diff < 0, jnp.exp(jnp.maximum(-diff, 0.0)[None] * lg_b[:, None, None]), 0.0)
    kscale = HD ** -0.5
    mtab = ((dec_f + dec_b) * kscale).transpose(1, 0, 2).reshape(CHUNK, D)
    lanes = lambda t: jnp.repeat(t.T, HD, axis=1)
    zf = lanes(jnp.exp((CHUNK - 1.0 - idx)[None, :] * lg_f[:, None]) * kscale)
    zb = lanes(jnp.exp(idx[None, :] * lg_b[:, None]) * kscale)
    xif = lanes(jnp.exp((idx + 1.0)[None, :] * lg_f[:, None]))
    xib = lanes(jnp.exp((CHUNK - idx)[None, :] * lg_b[:, None]))
    cdf = jnp.repeat(jnp.exp(CHUNK * lg_f), HD)[None, :]
    cdb = jnp.repeat(jnp.exp(CHUNK * lg_b), HD)[None, :]
    return (mtab, zf, zb, xib, cdb), (xif, cdf)


def _hi_lo(a):
    hi = a.astype(bf16)
    return jnp.concatenate([hi, (a - hi.astype(f32)).astype(bf16)], axis=1)


def _mixer(x, mod, p):
    p1, cr, yp, qr, kvf = _mixer_a(x, mod, p["win"], p["glnw"], p["glnb"], p["wsp"], p["bsp"],
                                   _rope_tables(x.shape[1]), p["tabs_a"], p["gnw"])
    xif, cdf = p["tabs_b"]
    return _mixer_b(yp, qr, kvf, p1, cr, x, mod, xif, cdf, p["wout"], p["ln1w"], p["ln1b"], p["wr"], p["br"])


def _moe_ln2(x1s, h2s, routed, mod, p):
    idx, w, lrank = (jnp.concatenate([r[j] for r in routed], axis=1) for j in range(3))
    cnt = jnp.concatenate([r[3] for r in routed], axis=0)
    n = idx.shape[1]

    tt = SORT_TOKENS
    nt = n // tt
    counts = cnt[:, :, 0].astype(i32)
    run = (counts + UNIT - 1) // UNIT * UNIT
    loff = jnp.cumsum(run, axis=1) - run
    total = jnp.sum(run, axis=0)
    padded = (total + EXPERT_ROWS - 1) // EXPERT_ROWS * EXPERT_ROWS
    pends = jnp.cumsum(padded)
    pstarts = pends - padded
    seg = pstarts[None, :] + jnp.cumsum(run, axis=0) - run
    n_blocks = (n * TOPK + nt * E * (UNIT - 1)) // EXPERT_ROWS + E
    n_used = (pends[E - 1:] // EXPERT_ROWS).astype(i32)
    block_start = jnp.arange(n_blocks, dtype=i32) * EXPERT_ROWS
    block_e = jnp.minimum(jnp.sum((pends[None, :] <= block_start[:, None]).astype(i32), axis=1), E - 1)
    lpos = lrank.reshape(TOPK, nt, tt)
    idx3 = idx.reshape(TOPK, nt, tt)
    for e_ in range(E):
        lpos = lpos + jnp.where(idx3 == e_, loff[None, :, e_, None], 0)
    lpos = lpos.reshape(TOPK, n)
    flat = lambda a: a.reshape(-1).astype(i32)
    runs = flat(jnp.stack([run // UNIT, loff, seg], axis=-1))
    tunits = flat(jnp.sum(run, axis=1) // UNIT)

    n_rows = n_blocks * EXPERT_ROWS
    zstart = jnp.concatenate([pstarts + total, pends[E - 1:]])
    zunits = jnp.concatenate([padded - total, n_rows - pends[E - 1:]]) // UNIT
    xbuf = _dispatch(h2s, lpos, runs, tunits, flat(zstart), flat(zunits), n_rows)
    ybuf = _experts(block_e.astype(i32), n_used, xbuf, p["wgu"], p["bgu"], p["wd"], p["bd"])
    outs = _combine(runs, tunits, lpos.T, w.T, mod, p["ln2w"], p["ln2b"], ybuf,
                    [x.reshape(-1, D) for x in x1s], [x.shape[1] for x in x1s])
    return [o.reshape(x.shape) for o, x in zip(outs, x1s)]


def kernel(x_prompt, x_sample, c_prompt, c_sample, w_ada, b_ada, w_in, gmlp_ln_w, gmlp_ln_b, w_spatial, b_spatial, ret_theta_fwd, ret_theta_bwd, ret_gn_w, w_out, ln1_w, ln1_b, w_router, b_router, w_gate_up, b_gate_up, w_down, b_down, ln2_w, ln2_b):
    l = 0
    bp, bs = x_prompt.shape[0], x_sample.shape[0]
    c_all = jnp.concatenate([c_prompt, c_sample], axis=0)
    c_rows = -(-c_all.shape[0] // SUBLANES) * SUBLANES
    c_all = jnp.pad(c_all, ((0, c_rows - c_all.shape[0]), (0, 0)))
    mod = _adaln(c_all, w_ada[l], b_ada[l]).reshape(c_rows, 6, D)

    tabs_a, tabs_b = _retention_tables(ret_theta_fwd[l], ret_theta_bwd[l])
    row = lambda a: a.reshape(1, -1)
    p = dict(
        win=w_in[l].astype(bf16), glnw=row(gmlp_ln_w[l]), glnb=row(gmlp_ln_b[l]),
        wsp=w_spatial[l].astype(bf16), bsp=jnp.repeat(b_spatial[l].T, HD, axis=1),
        tabs_a=tabs_a, tabs_b=tabs_b, gnw=row(ret_gn_w[l]),
        wout=w_out[l].astype(bf16), ln1w=row(ln1_w[l]), ln1b=row(ln1_b[l]),
        wr=_hi_lo(jnp.pad(w_router[l], ((0, 0), (0, LANES - E)))), br=jnp.pad(row(b_router[l]), ((0, 0), (0, LANES - E))),
        wgu=w_gate_up[l], bgu=b_gate_up[l].reshape(E, 1, 2 * FF),
        wd=w_down[l], bd=b_down[l].reshape(E, 1, D),
        ln2w=row(ln2_w[l]), ln2b=row(ln2_b[l]),
    )
    mixed = [_mixer(x_prompt, mod[:bp], p), _mixer(x_sample, mod[bp:bp + bs], p)]
    x1s, h2s = [m[0] for m in mixed], [m[1] for m in mixed]
    y_prompt, y_sample = _moe_ln2(x1s, h2s, [m[2:] for m in mixed], mod, p)
    return (y_prompt, y_sample)
```
